```python
import math
import jax, jax.numpy as jnp
from jax import lax
import numpy as np

D_MODEL = 1024
BATCH = 2
SEQ = 8192
DEPTH = 1

EPS = 1e-6
A_GROUPS = 8
A_GROUP_DIM = D_MODEL // A_GROUPS
A_WIDTH = A_GROUPS * A_GROUP_DIM
CHUNK = 128
B_HEADS = 8
B_HEAD_DIM = D_MODEL // 16
B_V_DIM = 2 * B_HEAD_DIM
QK_WIDTH = B_HEADS * 2 * B_HEAD_DIM
B_WIDTH = B_HEADS * B_V_DIM
Q_BLOCK = 128
IN_COLS = 2 * A_WIDTH + 2 * QK_WIDTH + B_WIDTH
N_GROUPS = 4
EXPERTS_PER_GROUP = 8
N_EXPERTS = N_GROUPS * EXPERTS_PER_GROUP
TOP_K = 2
D_EXPERT = D_MODEL // 4
MOE_BLOCK = 128

kernel_name = "hybrid_gmlp_diffattn_hiermoe"


def rmsnorm(x, g):
    xf = x.astype(jnp.float32)
    r = lax.rsqrt(jnp.mean(xf * xf, axis=-1, keepdims=True) + EPS)
    return (xf * r).astype(x.dtype) * g


def chunked_spatial_gating(u, v, g_v, w_s, b_s):
    bsz, s, _ = u.shape
    nc = s // CHUNK
    v = rmsnorm(v.reshape(bsz, s, A_GROUPS, A_GROUP_DIM), g_v)
    v = v.reshape(bsz, nc, CHUNK, A_GROUPS, A_GROUP_DIM)
    causal = jnp.tril(jnp.ones((CHUNK, CHUNK), dtype=bool))
    ws = jnp.where(causal[None], w_s, 0.0)
    z = jnp.einsum('gts,bcsgd->bctgd', ws, v) + b_s.T[None, None, :, :, None]
    return u * z.reshape(bsz, s, A_WIDTH)


def diff_attention(q, k, v, lam, slopes):
    bsz, nh, _, s, hd = q.shape
    nb = s // Q_BLOCK
    key_pos = jnp.arange(s)
    scale = hd ** -0.5

    def block(i):
        start = i * Q_BLOCK
        qb = lax.dynamic_slice_in_dim(q, start, Q_BLOCK, axis=3)
        sc = jnp.einsum('bhcqd,bhckd->bhcqk', qb, k,
                        preferred_element_type=jnp.float32) * scale
        q_pos = start + jnp.arange(Q_BLOCK)
        dist = (q_pos[:, None] - key_pos[None, :]).astype(jnp.float32)
        alibi = -slopes[:, None, None] * jnp.abs(dist)[None]
        sc = jnp.where((dist >= 0)[None, None, None], sc + alibi[None, :, None], -jnp.inf)
        p = jax.nn.softmax(sc, axis=-1)
        a = p[:, :, 0] - lam * p[:, :, 1]
        return jnp.einsum('bhqk,bhkd->bhqd', a.astype(v.dtype), v)

    out = lax.map(block, jnp.arange(nb))
    return out.transpose(1, 2, 0, 3, 4).reshape(bsz, nh, s, v.shape[-1])


def hierarchical_moe(h, w_rg, b_rg, w_re, b_re, w1, w3, w2):
    bsz, s, d = h.shape
    t = h.reshape(-1, d)
    n_tok = t.shape[0]
    g_prob = jax.nn.softmax((t @ w_rg + b_rg).astype(jnp.float32), axis=-1)
    g_gate, g_idx = lax.top_k(g_prob, 1)
    e_logits = (t @ w_re + b_re).astype(jnp.float32).reshape(n_tok, N_GROUPS, EXPERTS_PER_GROUP)
    e_logits = jnp.take_along_axis(e_logits, g_idx[:, :, None], axis=1)[:, 0]
    e_w, e_idx = lax.top_k(jax.nn.softmax(e_logits, axis=-1), TOP_K)
    e_w = e_w / jnp.sum(e_w, axis=-1, keepdims=True)
    weight = g_gate * e_w
    expert_id = g_idx * EXPERTS_PER_GROUP + e_idx
    combine = jnp.sum(jax.nn.one_hot(expert_id, N_EXPERTS, dtype=jnp.float32)
                      * weight[..., None], axis=1)

    def expert_block(args):
        tk, c = args
        hid = jax.nn.silu(jnp.einsum('td,edf->tef', tk, w1)) * jnp.einsum('td,edf->tef', tk, w3)
        hid = hid * c[:, :, None].astype(hid.dtype)
        return jnp.einsum('tef,efd->td', hid, w2)

    y = lax.map(expert_block, (t.reshape(-1, MOE_BLOCK, d),
                               combine.reshape(-1, MOE_BLOCK, N_EXPERTS)))
    return y.reshape(bsz, s, d)


def setup_inputs(seed: int = 0) -> dict:
    key = jax.random.key(seed)
    ks = jax.random.split(key, 32)
    f32 = jnp.float32
    L, D = DEPTH, D_MODEL

    def nrm(k, shape, scale):
        return jax.random.normal(k, shape, f32) * scale

    def gain(k, shape):
        return 1.0 + 0.02 * jax.random.normal(k, shape, f32)

    return {
        "x": jax.random.normal(ks[0], (BATCH, SEQ, D), f32),
        "norm1_g": gain(ks[1], (L, D)),
        "w_in": nrm(ks[2], (L, D, IN_COLS), D ** -0.5),
        "v_norm_g": gain(ks[3], (L, A_GROUPS, A_GROUP_DIM)),
        "w_s": nrm(ks[4], (L, A_GROUPS, CHUNK, CHUNK), CHUNK ** -0.5),
        "b_s": 1.0 + 0.02 * jax.random.normal(ks[5], (L, A_GROUPS, CHUNK), f32),
        "q_norm_g": gain(ks[6], (L, B_HEAD_DIM)),
        "k_norm_g": gain(ks[7], (L, B_HEAD_DIM)),
        "lambda_q1": nrm(ks[8], (L, B_HEAD_DIM), 0.1),
        "lambda_k1": nrm(ks[9], (L, B_HEAD_DIM), 0.1),
        "lambda_q2": nrm(ks[10], (L, B_HEAD_DIM), 0.1),
        "lambda_k2": nrm(ks[11], (L, B_HEAD_DIM), 0.1),
        "sub_norm_g": gain(ks[12], (L, B_V_DIM)),
        "w_up_a": nrm(ks[13], (L, A_WIDTH, D), A_WIDTH ** -0.5),
        "w_up_b": nrm(ks[14], (L, B_WIDTH, D), B_WIDTH ** -0.5),
        "w_gate": nrm(ks[15], (L, D, 2 * D), D ** -0.5),
        "b_gate": nrm(ks[16], (L, 2 * D), 0.02),
        "w_out": nrm(ks[17], (L, D, D), D ** -0.5),
        "norm2_g": gain(ks[18], (L, D)),
        "w_rg": nrm(ks[19], (L, D, N_GROUPS), D ** -0.5),
        "b_rg": nrm(ks[20], (L, N_GROUPS), 0.01),
        "w_re": nrm(ks[21], (L, D, N_EXPERTS), D ** -0.5),
        "b_re": nrm(ks[22], (L, N_EXPERTS), 0.01),
        "w1": nrm(ks[23], (L, N_EXPERTS, D, D_EXPERT), D ** -0.5),
        "w3": nrm(ks[24], (L, N_EXPERTS, D, D_EXPERT), D ** -0.5),
        "w2": nrm(ks[25], (L, N_EXPERTS, D_EXPERT, D), D_EXPERT ** -0.5),
    }


def reference(x, norm1_g, w_in, v_norm_g, w_s, b_s, q_norm_g, k_norm_g,
              lambda_q1, lambda_k1, lambda_q2, lambda_k2, sub_norm_g,
              w_up_a, w_up_b, w_gate, b_gate, w_out, norm2_g,
              w_rg, b_rg, w_re, b_re, w1, w3, w2):
    bsz, s, d = x.shape
    slopes = jnp.exp2(-8.0 * (jnp.arange(B_HEADS) + 1) / B_HEADS)
    splits = [A_WIDTH, 2 * A_WIDTH, 2 * A_WIDTH + QK_WIDTH, 2 * A_WIDTH + 2 * QK_WIDTH]
    for l in range(DEPTH):
        lam_init = 0.8 - 0.6 * math.exp(-0.3 * l)
        h = rmsnorm(x, norm1_g[l])
        proj = h @ w_in[l]
        u, v_a, q, k, v_b = jnp.split(proj, splits, axis=-1)

        y_a = chunked_spatial_gating(u, v_a, v_norm_g[l], w_s[l], b_s[l])

        q = rmsnorm(q.reshape(bsz, s, B_HEADS, 2, B_HEAD_DIM), q_norm_g[l]).transpose(0, 2, 3, 1, 4)
        k = rmsnorm(k.reshape(bsz, s, B_HEADS, 2, B_HEAD_DIM), k_norm_g[l]).transpose(0, 2, 3, 1, 4)
        vb = v_b.reshape(bsz, s, B_HEADS, B_V_DIM).transpose(0, 2, 1, 3)
        lam = (jnp.exp(jnp.sum(lambda_q1[l].astype(jnp.float32) * lambda_k1[l].astype(jnp.float32)))
               - jnp.exp(jnp.sum(lambda_q2[l].astype(jnp.float32) * lambda_k2[l].astype(jnp.float32)))
               + lam_init)
        o = diff_attention(q, k, vb, lam, slopes)
        o = rmsnorm(o, sub_norm_g[l]) * (1.0 - lam_init)
        y_b = o.transpose(0, 2, 1, 3).reshape(bsz, s, B_WIDTH)

        gates = jax.nn.sigmoid(h @ w_gate[l] + b_gate[l])
        g_a, g_b = jnp.split(gates, 2, axis=-1)
        merged = g_a * (y_a @ w_up_a[l]) + g_b * (y_b @ w_up_b[l])
        x = x + merged @ w_out[l]

        h2 = rmsnorm(x, norm2_g[l])
        x = x + hierarchical_moe(h2, w_rg[l], b_rg[l], w_re[l], b_re[l], w1[l], w3[l], w2[l])
    return x
```

```python
import functools
import math

import jax
import jax.numpy as jnp
from jax import lax
from jax.experimental import pallas as pl
from jax.experimental.pallas import tpu as pltpu

F32 = jnp.float32
BF16 = jnp.bfloat16

D_MODEL = 1024
EPS = 1e-6
A_GROUPS = 8
CHUNK = 128
B_HEADS = 8
B_HEAD_DIM = 64
B_V_DIM = 128
IN_COLS = 5 * D_MODEL
N_GROUPS = 4
EXPERTS_PER_GROUP = 8
N_EXPERTS = 32
D_EXPERT = 256
LAM_INIT = 0.8 - 0.6 * math.exp(-0.3 * 0)

LANES = 128
MXU_DIM = 256
VMEM_LIMIT_BYTES = 56 * 1024 * 1024

NEG_BIG = -1e30

PROJ_TM = 512
PROJ_TN = 1024
ATT_TQ = 512
ATT_TK = 512
MERGE_TM = 512
MOE_TM = 1024
MOE_NE = 4
ROUTER_ROWS = 48


def _segment_rms_scale(acc, seg_ones_ref, seg):
    sq = (acc * acc).astype(BF16)
    parts = []
    for p in range(acc.shape[1] // MXU_DIM):
        parts.append(jnp.dot(sq[:, p * MXU_DIM:(p + 1) * MXU_DIM], seg_ones_ref[...],
                             preferred_element_type=F32))
    ss = jnp.concatenate(parts, axis=1)
    return lax.rsqrt(ss * (1.0 / seg) + EPS)


def _inproj_kernel(x_ref, g1_ref, w_ref, bias_ref, gv_ref, gq_ref, gk_ref, ones64_ref, ones128_ref,
                   u_ref, va_ref, q_ref, k_ref, vb_ref, ga_ref, gb_ref, h_scr, *, seq_blocks):
    i = pl.program_id(0)
    j = pl.program_id(1)
    tm = x_ref.shape[0]

    @pl.when(j == 0)
    def _():
        x = x_ref[...]
        r = lax.rsqrt(jnp.mean(x * x, axis=-1, keepdims=True) + EPS)
        h_scr[...] = ((x * r) * g1_ref[...]).astype(BF16)

    acc = jnp.dot(h_scr[...], w_ref[...], preferred_element_type=F32)
    lane = lax.broadcasted_iota(jnp.int32, (1, LANES), 1)
    lo_half = lane < B_HEAD_DIM

    @pl.when(j == 0)
    def _():
        u_ref[...] = acc.astype(BF16)

    @pl.when(j == 1)
    def _():
        r = _segment_rms_scale(acc, ones128_ref, LANES)
        va_ref[...] = ((acc * r) * gv_ref[...]).astype(BF16)

    @pl.when(j == 2)
    def _():
        r = _segment_rms_scale(acc, ones64_ref, B_HEAD_DIM)
        qn = (acc * r) * gq_ref[...]
        for h in range(B_HEADS):
            slope = 2.0 ** (-8.0 * (h + 1) / B_HEADS)
            blk = qn[:, h * LANES:(h + 1) * LANES]
            c0 = jnp.where((lane == B_HEAD_DIM) | (lane == B_HEAD_DIM + 1), slope, 0.0)
            c1 = jnp.where((lane == 0) | (lane == 1), slope, 0.0)
            q_ref[0, 2 * h] = jnp.where(lo_half, blk, c0).astype(BF16)
            q_ref[0, 2 * h + 1] = jnp.where(lo_half, c1, blk).astype(BF16)

    @pl.when(j == 3)
    def _():
        r = _segment_rms_scale(acc, ones64_ref, B_HEAD_DIM)
        kn = (acc * r) * gk_ref[...]
        pos = (i % seq_blocks) * tm + lax.broadcasted_iota(jnp.int32, (tm, LANES), 0)
        pos_lo = (pos & (MXU_DIM - 1)).astype(F32)
        pos_hi = (pos - (pos & (MXU_DIM - 1))).astype(F32)
        p0 = jnp.where(lane == B_HEAD_DIM, pos_lo, jnp.where(lane == B_HEAD_DIM + 1, pos_hi, 0.0))
        p1 = jnp.where(lane == 0, pos_lo, jnp.where(lane == 1, pos_hi, 0.0))
        for h in range(B_HEADS):
            blk = kn[:, h * LANES:(h + 1) * LANES]
            k_ref[0, 2 * h] = jnp.where(lo_half, blk, p0).astype(BF16)
            k_ref[0, 2 * h + 1] = jnp.where(lo_half, p1, blk).astype(BF16)

    @pl.when(j == 4)
    def _():
        for h in range(B_HEADS):
            vb_ref[0, h] = acc[:, h * LANES:(h + 1) * LANES].astype(BF16)

    @pl.when(j == 5)
    def _():
        ga_ref[...] = (1.0 / (1.0 + jnp.exp(-(acc + bias_ref[...])))).astype(BF16)

    @pl.when(j == 6)
    def _():
        gb_ref[...] = (1.0 / (1.0 + jnp.exp(-(acc + bias_ref[...])))).astype(BF16)


def _input_projection(x2, g1, w_cat, bias_cat, gv, gq, gk, ones64, ones128, bsz, seq):
    n_tok = x2.shape[0]
    tm, tn = PROJ_TM, PROJ_TN
    seq_blocks = seq // tm
    n_col = w_cat.shape[1] // tn
    row_spec = pl.BlockSpec((tm, D_MODEL), lambda i, j: (i, 0))
    vec_spec = pl.BlockSpec((1, D_MODEL), lambda i, j: (0, 0))
    head_map = lambda i, j: (i // seq_blocks, 0, i % seq_blocks, 0)
    tok_bf16 = jax.ShapeDtypeStruct((n_tok, D_MODEL), BF16)
    return pl.pallas_call(
        functools.partial(_inproj_kernel, seq_blocks=seq_blocks),
        grid=(n_tok // tm, n_col),
        in_specs=[
            row_spec,
            vec_spec,
            pl.BlockSpec((D_MODEL, tn), lambda i, j: (0, j)),
            pl.BlockSpec((1, tn), lambda i, j: (0, j)),
            vec_spec, vec_spec, vec_spec,
            pl.BlockSpec((MXU_DIM, MXU_DIM), lambda i, j: (0, 0)),
            pl.BlockSpec((MXU_DIM, MXU_DIM), lambda i, j: (0, 0)),
        ],
        out_specs=[
            row_spec, row_spec,
            pl.BlockSpec((1, 2 * B_HEADS, tm, LANES), head_map),
            pl.BlockSpec((1, 2 * B_HEADS, tm, LANES), head_map),
            pl.BlockSpec((1, B_HEADS, tm, LANES), head_map),
            row_spec, row_spec,
        ],
        out_shape=[
            tok_bf16, tok_bf16,
            jax.ShapeDtypeStruct((bsz, 2 * B_HEADS, seq, LANES), BF16),
            jax.ShapeDtypeStruct((bsz, 2 * B_HEADS, seq, LANES), BF16),
            jax.ShapeDtypeStruct((bsz, B_HEADS, seq, LANES), BF16),
            tok_bf16, tok_bf16,
        ],
        scratch_shapes=[pltpu.VMEM((tm, D_MODEL), BF16)],
        compiler_params=pltpu.CompilerParams(
            dimension_semantics=("arbitrary", "arbitrary"),
            vmem_limit_bytes=VMEM_LIMIT_BYTES),
        name="input_projection",
    )(x2, g1, w_cat, bias_cat, gv, gq, gk, ones64, ones128)


def _attn_kernel(lam_ref, sg_ref, q_ref, k_ref, v_ref, o_ref, acc_scr):
    qi = pl.program_id(2)
    tq = q_ref.shape[2]
    tk = ATT_TK
    q0 = q_ref[0, 0]
    q1 = q_ref[0, 1]
    nt_dims = (((1,), (1,)), ((), ()))
    tn_dims = (((0,), (0,)), ((), ()))

    acc_scr[...] = jnp.zeros_like(acc_scr)

    def step(ki, carry, masked):
        m0, l0, m1, l1 = carry
        ks = pl.multiple_of(ki * tk, tk)
        v = v_ref[0, 0, pl.ds(ks, tk), :]
        s0 = lax.dot_general(k_ref[0, 0, pl.ds(ks, tk), :], q0, nt_dims, preferred_element_type=F32)
        s1 = lax.dot_general(k_ref[0, 1, pl.ds(ks, tk), :], q1, nt_dims, preferred_element_type=F32)
        if masked:
            key_i = lax.broadcasted_iota(jnp.int32, (tk, tq), 0)
            qry_i = lax.broadcasted_iota(jnp.int32, (tk, tq), 1)
            keep = key_i <= qry_i
            s0 = jnp.where(keep, s0, NEG_BIG)
            s1 = jnp.where(keep, s1, NEG_BIG)
        m0n = jnp.maximum(m0, jnp.max(s0, axis=0, keepdims=True))
        m1n = jnp.maximum(m1, jnp.max(s1, axis=0, keepdims=True))
        a0 = jnp.exp(m0 - m0n)
        a1 = jnp.exp(m1 - m1n)
        p0 = jnp.exp(s0 - m0n)
        p1 = jnp.exp(s1 - m1n)
        l0n = a0 * l0 + jnp.sum(p0, axis=0, keepdims=True)
        l1n = a1 * l1 + jnp.sum(p1, axis=0, keepdims=True)
        p = jnp.concatenate([p0.astype(BF16), p1.astype(BF16)], axis=1)
        pv = lax.dot_general(v, p, tn_dims, preferred_element_type=F32)
        acc_scr[...] = acc_scr[...] * jnp.concatenate([a0, a1], axis=1) + pv
        return m0n, l0n, m1n, l1n

    init = (jnp.full((1, tq), NEG_BIG, F32), jnp.zeros((1, tq), F32),
            jnp.full((1, tq), NEG_BIG, F32), jnp.zeros((1, tq), F32))
    carry = lax.fori_loop(0, qi, lambda ki, c: step(ki, c, False), init)
    _, l0, _, l1 = step(qi, carry, True)

    lam_v = lam_ref[...]
    lam = (jnp.exp(jnp.sum(lam_v[0:1] * lam_v[1:2], axis=1, keepdims=True))
           - jnp.exp(jnp.sum(lam_v[2:3] * lam_v[3:4], axis=1, keepdims=True)) + LAM_INIT)
    acc = acc_scr[...]
    o = acc[:, :tq] / l0 - lam * (acc[:, tq:] / l1)
    r = lax.rsqrt(jnp.mean(o * o, axis=0, keepdims=True) + EPS)
    o = ((o * r) * sg_ref[...]) * (1.0 - LAM_INIT)
    o_ref[0] = o.T.astype(BF16)


def _diff_attention(lam_vecs, sub_g, q_arr, k_arr, vb_arr):
    bsz, _, seq, _ = q_arr.shape
    tq = ATT_TQ
    return pl.pallas_call(
        _attn_kernel,
        grid=(bsz, B_HEADS, seq // tq),
        in_specs=[
            pl.BlockSpec((4, B_HEAD_DIM), lambda b, h, qi: (0, 0)),
            pl.BlockSpec((B_V_DIM, 1), lambda b, h, qi: (0, 0)),
            pl.BlockSpec((1, 2, tq, LANES), lambda b, h, qi: (b, h, qi, 0)),
            pl.BlockSpec((1, 2, seq, LANES), lambda b, h, qi: (b, h, 0, 0)),
            pl.BlockSpec((1, 1, seq, LANES), lambda b, h, qi: (b, h, 0, 0)),
        ],
        out_specs=pl.BlockSpec((1, tq, B_V_DIM), lambda b, h, qi: (b, qi, h)),
        out_shape=jax.ShapeDtypeStruct((bsz, seq, B_HEADS * B_V_DIM), BF16),
        scratch_shapes=[pltpu.VMEM((B_V_DIM, 2 * tq), F32)],
        compiler_params=pltpu.CompilerParams(
            dimension_semantics=("arbitrary", "arbitrary", "arbitrary"),
            vmem_limit_bytes=VMEM_LIMIT_BYTES),
        name="diff_attention",
    )(lam_vecs, sub_g, q_arr, k_arr, vb_arr)


def _first_argmax(vals, iota, top):
    return jnp.min(jnp.where(vals == top, iota, vals.shape[0]), axis=0, keepdims=True)


def _merge_kernel(u_ref, va_ref, yb_ref, ga_ref, gb_ref, x_ref, ws_ref, bs_ref,
                  wua_ref, wub_ref, wo_ref, g2_ref, wr_hi_ref, wr_lo_ref, br_ref,
                  x1_ref, h2_ref, comb_ref):
    tm = x_ref.shape[0]
    row = lax.broadcasted_iota(jnp.int32, (CHUNK, CHUNK), 0)
    col = lax.broadcasted_iota(jnp.int32, (CHUNK, CHUNK), 1)
    causal = row >= col

    group_cols = []
    for g in range(A_GROUPS):
        w_g = jnp.where(causal, ws_ref[g], 0.0).astype(BF16)
        b_g = bs_ref[g]
        chunks = []
        for c in range(tm // CHUNK):
            v_cg = va_ref[c * CHUNK:(c + 1) * CHUNK, g * LANES:(g + 1) * LANES]
            z = jnp.dot(w_g, v_cg, preferred_element_type=F32) + b_g
            u_cg = u_ref[c * CHUNK:(c + 1) * CHUNK, g * LANES:(g + 1) * LANES].astype(F32)
            chunks.append((u_cg * z).astype(BF16))
        group_cols.append(jnp.concatenate(chunks, axis=0))
    y_a = jnp.concatenate(group_cols, axis=1)

    up_a = jnp.dot(y_a, wua_ref[...], preferred_element_type=F32)
    up_b = jnp.dot(yb_ref[...], wub_ref[...], preferred_element_type=F32)
    merged = ga_ref[...].astype(F32) * up_a + gb_ref[...].astype(F32) * up_b
    x1 = x_ref[...] + jnp.dot(merged.astype(BF16), wo_ref[...], preferred_element_type=F32)
    x1_ref[...] = x1

    r = lax.rsqrt(jnp.mean(x1 * x1, axis=-1, keepdims=True) + EPS)
    h2 = (x1 * r) * g2_ref[...]
    h2_hi = h2.astype(BF16)
    h2_ref[...] = h2_hi
    h2_lo = (h2 - h2_hi.astype(F32)).astype(BF16)

    nt_dims = (((1,), (1,)), ((), ()))
    lt = (lax.dot_general(wr_hi_ref[...], h2_hi, nt_dims, preferred_element_type=F32)
          + lax.dot_general(wr_hi_ref[...], h2_lo, nt_dims, preferred_element_type=F32)
          + lax.dot_general(wr_lo_ref[...], h2_hi, nt_dims, preferred_element_type=F32)
          + br_ref[...])

    iota8 = lax.broadcasted_iota(jnp.int32, (8, tm), 0)
    gl = lt[0:8]
    ge = jnp.exp(gl - jnp.max(gl, axis=0, keepdims=True))
    gp = ge / jnp.sum(ge, axis=0, keepdims=True)
    g_gate = jnp.max(gp, axis=0, keepdims=True)
    g_idx = _first_argmax(gp, iota8, g_gate)

    el = jnp.zeros((EXPERTS_PER_GROUP, tm), F32)
    for gi in range(N_GROUPS):
        el = jnp.where(g_idx == gi, lt[8 + 8 * gi:16 + 8 * gi], el)
    ee = jnp.exp(el - jnp.max(el, axis=0, keepdims=True))
    ep = ee / jnp.sum(ee, axis=0, keepdims=True)
    p_top1 = jnp.max(ep, axis=0, keepdims=True)
    i_top1 = _first_argmax(ep, iota8, p_top1)
    ep_rest = jnp.where(iota8 == i_top1, -1.0, ep)
    p_top2 = jnp.max(ep_rest, axis=0, keepdims=True)
    i_top2 = _first_argmax(ep_rest, iota8, p_top2)
    den = p_top1 + p_top2
    w_top1 = g_gate * (p_top1 / den)
    w_top2 = g_gate * (p_top2 / den)
    in_group = jnp.where(iota8 == i_top1, w_top1, 0.0) + jnp.where(iota8 == i_top2, w_top2, 0.0)
    blocks = [jnp.where(g_idx == gi, in_group, 0.0) for gi in range(N_GROUPS)]
    blocks.append(jnp.zeros((LANES - N_EXPERTS, tm), F32))
    comb_t = jnp.concatenate(blocks, axis=0)
    comb_ref[...] = comb_t.T


def _merge_and_route(u, va, yb, ga, gb, x2, ws, bs, wua, wub, wo, g2, wr_hi, wr_lo, br):
    n_tok = x2.shape[0]
    tm = MERGE_TM
    row_spec = pl.BlockSpec((tm, D_MODEL), lambda i: (i, 0))
    full = lambda shape: pl.BlockSpec(shape, lambda i: (0,) * len(shape))
    return pl.pallas_call(
        _merge_kernel,
        grid=(n_tok // tm,),
        in_specs=[
            row_spec, row_spec, row_spec, row_spec, row_spec, row_spec,
            full((A_GROUPS, CHUNK, CHUNK)), full((A_GROUPS, CHUNK, 1)),
            full((D_MODEL, D_MODEL)), full((D_MODEL, D_MODEL)), full((D_MODEL, D_MODEL)),
            full((1, D_MODEL)),
            full((ROUTER_ROWS, D_MODEL)), full((ROUTER_ROWS, D_MODEL)), full((ROUTER_ROWS, 1)),
        ],
        out_specs=[row_spec, row_spec, pl.BlockSpec((tm, LANES), lambda i: (i, 0))],
        out_shape=[
            jax.ShapeDtypeStruct((n_tok, D_MODEL), F32),
            jax.ShapeDtypeStruct((n_tok, D_MODEL), BF16),
            jax.ShapeDtypeStruct((n_tok, LANES), F32),
        ],
        compiler_params=pltpu.CompilerParams(
            dimension_semantics=("arbitrary",),
            vmem_limit_bytes=VMEM_LIMIT_BYTES),
        name="merge_and_route",
    )(u, va, yb, ga, gb, x2, ws, bs, wua, wub, wo, g2, wr_hi, wr_lo, br)


def _moe_kernel(h_ref, c_ref, x1_ref, w13_ref, w2_ref, o_ref, acc_scr, csel_scr):
    e = pl.program_id(1)
    n_steps = pl.num_programs(1)

    @pl.when(e == 0)
    def _():
        acc_scr[...] = x1_ref[...]

    for eb in range(N_EXPERTS // MOE_NE):
        @pl.when(e == eb)
        def _():
            csel_scr[...] = c_ref[:, eb * MOE_NE:(eb + 1) * MOE_NE]

    h = h_ref[...]
    hids = []
    for n in range(MOE_NE):
        ab = jnp.dot(h, w13_ref[n], preferred_element_type=F32)
        a = ab[:, :D_EXPERT]
        b = ab[:, D_EXPERT:]
        hid = (a * (1.0 / (1.0 + jnp.exp(-a)))) * b
        hids.append((hid * csel_scr[:, n:n + 1]).astype(BF16))
    hid_all = jnp.concatenate(hids, axis=1)
    acc_scr[...] += jnp.dot(hid_all, w2_ref[...], preferred_element_type=F32)

    @pl.when(e == n_steps - 1)
    def _():
        o_ref[...] = acc_scr[...]


def _moe(h2, comb, x1, w13, w2f):
    n_tok = h2.shape[0]
    tm = MOE_TM
    return pl.pallas_call(
        _moe_kernel,
        grid=(n_tok // tm, N_EXPERTS // MOE_NE),
        in_specs=[
            pl.BlockSpec((tm, D_MODEL), lambda i, e: (i, 0)),
            pl.BlockSpec((tm, LANES), lambda i, e: (i, 0)),
            pl.BlockSpec((tm, D_MODEL), lambda i, e: (i, 0)),
            pl.BlockSpec((MOE_NE, D_MODEL, 2 * D_EXPERT), lambda i, e: (e, 0, 0)),
            pl.BlockSpec((MOE_NE * D_EXPERT, D_MODEL), lambda i, e: (e, 0)),
        ],
        out_specs=pl.BlockSpec((tm, D_MODEL), lambda i, e: (i, 0)),
        out_shape=jax.ShapeDtypeStruct((n_tok, D_MODEL), F32),
        scratch_shapes=[pltpu.VMEM((tm, D_MODEL), F32), pltpu.VMEM((tm, MOE_NE), F32)],
        compiler_params=pltpu.CompilerParams(
            dimension_semantics=("arbitrary", "arbitrary"),
            vmem_limit_bytes=VMEM_LIMIT_BYTES),
        name="moe_experts",
    )(h2, comb, x1, w13, w2f)


def _segment_ones(seg):
    idx = jnp.arange(MXU_DIM) // seg
    return (idx[:, None] == idx[None, :]).astype(BF16)


def kernel(x, norm1_g, w_in, v_norm_g, w_s, b_s, q_norm_g, k_norm_g, lambda_q1, lambda_k1, lambda_q2, lambda_k2, sub_norm_g, w_up_a, w_up_b, w_gate, b_gate, w_out, norm2_g, w_rg, b_rg, w_re, b_re, w1, w3, w2):
    bsz, seq, d = x.shape
    assert d == D_MODEL and seq % PROJ_TM == 0 and seq % ATT_TQ == 0 and ATT_TQ == ATT_TK
    assert norm1_g.shape[0] == 1, "single layer"
    n_tok = bsz * seq
    assert n_tok % MOE_TM == 0 and n_tok % MERGE_TM == 0
    x2 = x.reshape(n_tok, d)

    w_cat = jnp.concatenate([w_in[0], w_gate[0]], axis=1).astype(BF16)
    bias_cat = jnp.concatenate([jnp.zeros((IN_COLS,), F32), b_gate[0]])[None, :]
    gv = v_norm_g[0].reshape(1, D_MODEL)
    gq = jnp.tile(q_norm_g[0] * (B_HEAD_DIM ** -0.5), 2 * B_HEADS)[None, :]
    gk = jnp.tile(k_norm_g[0], 2 * B_HEADS)[None, :]
    u, va, q_arr, k_arr, vb_arr, ga, gb = _input_projection(
        x2, norm1_g, w_cat, bias_cat, gv, gq, gk, _segment_ones(B_HEAD_DIM), _segment_ones(LANES),
        bsz, seq)

    lam_vecs = jnp.concatenate([lambda_q1, lambda_k1, lambda_q2, lambda_k2], axis=0)
    yb = _diff_attention(lam_vecs, sub_norm_g[0][:, None], q_arr, k_arr, vb_arr)
    yb = yb.reshape(n_tok, D_MODEL)

    pad_rows = ROUTER_ROWS - 8 - N_EXPERTS
    wr = jnp.concatenate([w_rg[0].T, jnp.zeros((4, d), F32), w_re[0].T,
                          jnp.zeros((pad_rows, d), F32)], axis=0)
    br = jnp.concatenate([b_rg[0], jnp.full((4,), NEG_BIG, F32), b_re[0],
                          jnp.zeros((pad_rows,), F32)])[:, None]
    wr_hi = wr.astype(BF16)
    wr_lo = (wr - wr_hi.astype(F32)).astype(BF16)
    x1, h2, comb = _merge_and_route(
        u, va, yb, ga, gb, x2, w_s[0], b_s[0][:, :, None],
        w_up_a[0].astype(BF16), w_up_b[0].astype(BF16), w_out[0].astype(BF16),
        norm2_g, wr_hi, wr_lo, br)

    w13 = jnp.concatenate([w1[0], w3[0]], axis=2).astype(BF16)
    w2f = w2[0].astype(BF16).reshape(N_EXPERTS * D_EXPERT, D_MODEL)
    out = _moe(h2, comb, x1, w13, w2f)
    return out.reshape(bsz, seq, d)
```

```python
import functools
import math

import jax
import jax.numpy as jnp
import numpy as np
from jax import lax
from jax.experimental import pallas as pl
from jax.experimental.pallas import tpu as pltpu

F32 = jnp.float32
BF16 = jnp.bfloat16

D_MODEL = 1024
EPS = 1e-6
A_GROUPS = 8
CHUNK = 128
B_HEADS = 8
B_HEAD_DIM = 64
B_V_DIM = 128
IN_COLS = 5 * D_MODEL
N_GROUPS = 4
EXPERTS_PER_GROUP = 8
N_EXPERTS = 32
D_EXPERT = 256
LAM_INIT = 0.8 - 0.6 * math.exp(-0.3 * 0)

LANES = 128
MXU_DIM = 256
VMEM_LIMIT_BYTES = 56 * 1024 * 1024

NEG_BIG = -1e30
LOG2_E = math.log2(math.e)
ALIBI_SLOPES = tuple(2.0 ** (-8.0 * (h + 1) / B_HEADS) for h in range(B_HEADS))
SLOPE_PARTS = 4
VT_ROWS = B_V_DIM + 16

PROJ_TM = 512
PROJ_TN = 1024
ATT_TQ = 1024
ATT_TK = 512
MERGE_TM = 512
MOE_TM = 1024
MOE_NE = 4
ROUTER_ROWS = 48


def _bf16_parts(value, n_parts):
    parts, rest = [], np.float64(value)
    for _ in range(n_parts):
        part = np.float64(np.asarray(rest, np.float32).astype(BF16).astype(np.float32))
        parts.append(float(part))
        rest = rest - part
    return parts


def _segment_rms_scale(acc, seg_ones_ref, seg):
    sq = (acc * acc).astype(BF16)
    parts = []
    for p in range(acc.shape[1] // MXU_DIM):
        parts.append(jnp.dot(sq[:, p * MXU_DIM:(p + 1) * MXU_DIM], seg_ones_ref[...],
                             preferred_element_type=F32))
    ss = jnp.concatenate(parts, axis=1)
    return lax.rsqrt(ss * (1.0 / seg) + EPS)


def _inproj_kernel(x_ref, g1_ref, w_ref, bias_ref, gv_ref, gq_ref, gk_ref, ones64_ref, ones128_ref,
                   u_ref, va_ref, q_ref, k_ref, vt_ref, ga_ref, gb_ref, h_scr, *, seq_blocks):
    i = pl.program_id(0)
    j = pl.program_id(1)
    tm = x_ref.shape[0]

    @pl.when(j == 0)
    def _():
        x = x_ref[...]
        r = lax.rsqrt(jnp.mean(x * x, axis=-1, keepdims=True) + EPS)
        h_scr[...] = ((x * r) * g1_ref[...]).astype(BF16)

    acc = jnp.dot(h_scr[...], w_ref[...], preferred_element_type=F32)
    lane = lax.broadcasted_iota(jnp.int32, (1, LANES), 1)
    lo_half = lane < B_HEAD_DIM

    @pl.when(j == 0)
    def _():
        u_ref[...] = acc.astype(BF16)

    @pl.when(j == 1)
    def _():
        r = _segment_rms_scale(acc, ones128_ref, LANES)
        va_ref[...] = ((acc * r) * gv_ref[...]).astype(BF16)

    spare_col = lane & (B_HEAD_DIM - 1)

    @pl.when(j == 2)
    def _():
        r = _segment_rms_scale(acc, ones64_ref, B_HEAD_DIM)
        qn = (acc * r) * gq_ref[...]
        for h in range(B_HEADS):
            parts = _bf16_parts(ALIBI_SLOPES[h] * LOG2_E, SLOPE_PARTS)
            cvec = jnp.zeros((1, LANES), F32)
            for n, part in enumerate(parts):
                cvec = jnp.where((spare_col == n) | (spare_col == n + SLOPE_PARTS), part, cvec)
            blk = qn[:, h * LANES:(h + 1) * LANES]
            q_ref[0, 2 * h] = jnp.where(lo_half, blk, cvec).astype(BF16)
            q_ref[0, 2 * h + 1] = jnp.where(lo_half, cvec, blk).astype(BF16)

    @pl.when(j == 3)
    def _():
        r = _segment_rms_scale(acc, ones64_ref, B_HEAD_DIM)
        kn = (acc * r) * gk_ref[...]
        pos = (i % seq_blocks) * tm + lax.broadcasted_iota(jnp.int32, (tm, LANES), 0)
        pos_lo = (pos & (MXU_DIM - 1)).astype(F32)
        pos_hi = (pos - (pos & (MXU_DIM - 1))).astype(F32)
        pvec = jnp.where(spare_col < SLOPE_PARTS, pos_lo,
                         jnp.where(spare_col < 2 * SLOPE_PARTS, pos_hi, 0.0))
        for h in range(B_HEADS):
            blk = kn[:, h * LANES:(h + 1) * LANES]
            k_ref[0, 2 * h] = jnp.where(lo_half, blk, pvec).astype(BF16)
            k_ref[0, 2 * h + 1] = jnp.where(lo_half, pvec, blk).astype(BF16)

    @pl.when(j == 4)
    def _():
        for h in range(B_HEADS):
            vt_ref[0, h, 0:B_V_DIM, :] = acc[:, h * LANES:(h + 1) * LANES].T.astype(BF16)
            vt_ref[0, h, B_V_DIM:VT_ROWS, :] = jnp.ones((VT_ROWS - B_V_DIM, tm), BF16)

    @pl.when(j == 5)
    def _():
        ga_ref[...] = (1.0 / (1.0 + jnp.exp(-(acc + bias_ref[...])))).astype(BF16)

    @pl.when(j == 6)
    def _():
        gb_ref[...] = (1.0 / (1.0 + jnp.exp(-(acc + bias_ref[...])))).astype(BF16)


def _input_projection(x2, g1, w_cat, bias_cat, gv, gq, gk, ones64, ones128, bsz, seq):
    n_tok = x2.shape[0]
    tm, tn = PROJ_TM, PROJ_TN
    seq_blocks = seq // tm
    n_col = w_cat.shape[1] // tn
    row_spec = pl.BlockSpec((tm, D_MODEL), lambda i, j: (i, 0))
    vec_spec = pl.BlockSpec((1, D_MODEL), lambda i, j: (0, 0))
    head_map = lambda i, j: (i // seq_blocks, 0, i % seq_blocks, 0)
    tok_bf16 = jax.ShapeDtypeStruct((n_tok, D_MODEL), BF16)
    return pl.pallas_call(
        functools.partial(_inproj_kernel, seq_blocks=seq_blocks),
        grid=(n_tok // tm, n_col),
        in_specs=[
            row_spec,
            vec_spec,
            pl.BlockSpec((D_MODEL, tn), lambda i, j: (0, j)),
            pl.BlockSpec((1, tn), lambda i, j: (0, j)),
            vec_spec, vec_spec, vec_spec,
            pl.BlockSpec((MXU_DIM, MXU_DIM), lambda i, j: (0, 0)),
            pl.BlockSpec((MXU_DIM, MXU_DIM), lambda i, j: (0, 0)),
        ],
        out_specs=[
            row_spec, row_spec,
            pl.BlockSpec((1, 2 * B_HEADS, tm, LANES), head_map),
            pl.BlockSpec((1, 2 * B_HEADS, tm, LANES), head_map),
            pl.BlockSpec((1, B_HEADS, VT_ROWS, tm),
                         lambda i, j: (i // seq_blocks, 0, 0, i % seq_blocks)),
            row_spec, row_spec,
        ],
        out_shape=[
            tok_bf16, tok_bf16,
            jax.ShapeDtypeStruct((bsz, 2 * B_HEADS, seq, LANES), BF16),
            jax.ShapeDtypeStruct((bsz, 2 * B_HEADS, seq, LANES), BF16),
            jax.ShapeDtypeStruct((bsz, B_HEADS, VT_ROWS, seq), BF16),
            tok_bf16, tok_bf16,
        ],
        scratch_shapes=[pltpu.VMEM((tm, D_MODEL), BF16)],
        compiler_params=pltpu.CompilerParams(
            dimension_semantics=("arbitrary", "arbitrary"),
            vmem_limit_bytes=VMEM_LIMIT_BYTES),
        name="input_projection",
    )(x2, g1, w_cat, bias_cat, gv, gq, gk, ones64, ones128)


def _attn_kernel(lam_ref, sg_ref, q_ref, k_ref, vt_ref, o_ref, acc_scr, sa_scr, sb_scr):
    qi = pl.program_id(2)
    tq = q_ref.shape[2]
    tk = ATT_TK
    nt_dims = (((1,), (1,)), ((), ()))

    def scores(kb, s_scr):
        ks = pl.multiple_of(kb * tk, tk)
        for c in range(2):
            s_scr[c] = lax.dot_general(k_ref[0, c, pl.ds(ks, tk), :], q_ref[0, c], nt_dims,
                                       preferred_element_type=F32)

    def consume(kb, s_scr, m_old, mask_offset):
        ks = pl.multiple_of(kb * tk, tk)
        vt = vt_ref[0, 0, :, pl.ds(ks, tk)]
        m_new = []
        for c in range(2):
            s = s_scr[c]
            if mask_offset is not None:
                key_i = lax.broadcasted_iota(jnp.int32, (tk, tq), 0) + mask_offset
                qry_i = lax.broadcasted_iota(jnp.int32, (tk, tq), 1)
                s = jnp.where(key_i <= qry_i, s, NEG_BIG)
            m_c = jnp.maximum(m_old[c], jnp.max(s, axis=0, keepdims=True))
            p = jnp.exp2(s - m_c).astype(BF16)
            pv = jnp.dot(vt, p, preferred_element_type=F32)
            cols = slice(c * tq, (c + 1) * tq)
            acc_scr[:, cols] = acc_scr[:, cols] * jnp.exp2(m_old[c] - m_c) + pv
            m_new.append(m_c)
        return tuple(m_new)

    acc_scr[...] = jnp.zeros_like(acc_scr)
    m = (jnp.full((1, tq), NEG_BIG, F32), jnp.full((1, tq), NEG_BIG, F32))
    assert tq == 2 * tk
    scores(0, sa_scr)

    def trip(t, m):
        scores(2 * t + 1, sb_scr)
        m = consume(2 * t, sa_scr, m, None)
        scores(2 * t + 2, sa_scr)
        return consume(2 * t + 1, sb_scr, m, None)

    m = lax.fori_loop(0, qi, trip, m)
    scores(2 * qi + 1, sb_scr)
    m = consume(2 * qi, sa_scr, m, 0)
    consume(2 * qi + 1, sb_scr, m, tk)

    lam_v = lam_ref[...]
    lam = (jnp.exp(jnp.sum(lam_v[0:1] * lam_v[1:2], axis=1, keepdims=True))
           - jnp.exp(jnp.sum(lam_v[2:3] * lam_v[3:4], axis=1, keepdims=True)) + LAM_INIT)
    o0 = acc_scr[0:B_V_DIM, 0:tq] / acc_scr[B_V_DIM:B_V_DIM + 1, 0:tq]
    o1 = acc_scr[0:B_V_DIM, tq:2 * tq] / acc_scr[B_V_DIM:B_V_DIM + 1, tq:2 * tq]
    o = o0 - lam * o1
    r = lax.rsqrt(jnp.mean(o * o, axis=0, keepdims=True) + EPS)
    o = ((o * r) * sg_ref[...]) * (1.0 - LAM_INIT)
    o_ref[0] = o.T.astype(BF16)


def _diff_attention(lam_vecs, sub_g, q_arr, k_arr, vt_arr):
    bsz, _, seq, _ = q_arr.shape
    tq = ATT_TQ
    return pl.pallas_call(
        _attn_kernel,
        grid=(bsz, B_HEADS, seq // tq),
        in_specs=[
            pl.BlockSpec((4, B_HEAD_DIM), lambda b, h, qi: (0, 0)),
            pl.BlockSpec((B_V_DIM, 1), lambda b, h, qi: (0, 0)),
            pl.BlockSpec((1, 2, tq, LANES), lambda b, h, qi: (b, h, qi, 0)),
            pl.BlockSpec((1, 2, seq, LANES), lambda b, h, qi: (b, h, 0, 0)),
            pl.BlockSpec((1, 1, VT_ROWS, seq), lambda b, h, qi: (b, h, 0, 0)),
        ],
        out_specs=pl.BlockSpec((1, tq, B_V_DIM), lambda b, h, qi: (b, qi, h)),
        out_shape=jax.ShapeDtypeStruct((bsz, seq, B_HEADS * B_V_DIM), BF16),
        scratch_shapes=[pltpu.VMEM((VT_ROWS, 2 * tq), F32),
                        pltpu.VMEM((2, ATT_TK, tq), F32),
                        pltpu.VMEM((2, ATT_TK, tq), F32)],
        compiler_params=pltpu.CompilerParams(
            dimension_semantics=("arbitrary", "arbitrary", "arbitrary"),
            vmem_limit_bytes=VMEM_LIMIT_BYTES),
        name="diff_attention",
    )(lam_vecs, sub_g, q_arr, k_arr, vt_arr)


def _first_argmax(vals, iota, top):
    return jnp.min(jnp.where(vals == top, iota, vals.shape[0]), axis=0, keepdims=True)


def _merge_kernel(u_ref, va_ref, yb_ref, ga_ref, gb_ref, x_ref, ws_ref, bs_ref,
                  wua_ref, wub_ref, wo_ref, g2_ref, wr_hi_ref, wr_lo_ref, br_ref,
                  x1_ref, h2_ref, comb_ref):
    tm = x_ref.shape[0]
    row = lax.broadcasted_iota(jnp.int32, (CHUNK, CHUNK), 0)
    col = lax.broadcasted_iota(jnp.int32, (CHUNK, CHUNK), 1)
    causal = row >= col

    group_cols = []
    for g in range(A_GROUPS):
        w_g = jnp.where(causal, ws_ref[g], 0.0).astype(BF16)
        b_g = bs_ref[g]
        chunks = []
        for c in range(tm // CHUNK):
            v_cg = va_ref[c * CHUNK:(c + 1) * CHUNK, g * LANES:(g + 1) * LANES]
            z = jnp.dot(w_g, v_cg, preferred_element_type=F32) + b_g
            u_cg = u_ref[c * CHUNK:(c + 1) * CHUNK, g * LANES:(g + 1) * LANES].astype(F32)
            chunks.append((u_cg * z).astype(BF16))
        group_cols.append(jnp.concatenate(chunks, axis=0))
    y_a = jnp.concatenate(group_cols, axis=1)

    up_a = jnp.dot(y_a, wua_ref[...], preferred_element_type=F32)
    up_b = jnp.dot(yb_ref[...], wub_ref[...], preferred_element_type=F32)
    merged = ga_ref[...].astype(F32) * up_a + gb_ref[...].astype(F32) * up_b
    x1 = x_ref[...] + jnp.dot(merged.astype(BF16), wo_ref[...], preferred_element_type=F32)
    x1_ref[...] = x1

    r = lax.rsqrt(jnp.mean(x1 * x1, axis=-1, keepdims=True) + EPS)
    h2 = (x1 * r) * g2_ref[...]
    h2_hi = h2.astype(BF16)
    h2_ref[...] = h2_hi
    h2_lo = (h2 - h2_hi.astype(F32)).astype(BF16)

    nt_dims = (((1,), (1,)), ((), ()))
    lt = (lax.dot_general(wr_hi_ref[...], h2_hi, nt_dims, preferred_element_type=F32)
          + lax.dot_general(wr_hi_ref[...], h2_lo, nt_dims, preferred_element_type=F32)
          + lax.dot_general(wr_lo_ref[...], h2_hi, nt_dims, preferred_element_type=F32)
          + br_ref[...])

    iota8 = lax.broadcasted_iota(jnp.int32, (8, tm), 0)
    gl = lt[0:8]
    ge = jnp.exp(gl - jnp.max(gl, axis=0, keepdims=True))
    gp = ge / jnp.sum(ge, axis=0, keepdims=True)
    g_gate = jnp.max(gp, axis=0, keepdims=True)
    g_idx = _first_argmax(gp, iota8, g_gate)

    el = jnp.zeros((EXPERTS_PER_GROUP, tm), F32)
    for gi in range(N_GROUPS):
        el = jnp.where(g_idx == gi, lt[8 + 8 * gi:16 + 8 * gi], el)
    ee = jnp.exp(el - jnp.max(el, axis=0, keepdims=True))
    ep = ee / jnp.sum(ee, axis=0, keepdims=True)
    p_top1 = jnp.max(ep, axis=0, keepdims=True)
    i_top1 = _first_argmax(ep, iota8, p_top1)
    ep_rest = jnp.where(iota8 == i_top1, -1.0, ep)
    p_top2 = jnp.max(ep_rest, axis=0, keepdims=True)
    i_top2 = _first_argmax(ep_rest, iota8, p_top2)
    den = p_top1 + p_top2
    w_top1 = g_gate * (p_top1 / den)
    w_top2 = g_gate * (p_top2 / den)
    in_group = jnp.where(iota8 == i_top1, w_top1, 0.0) + jnp.where(iota8 == i_top2, w_top2, 0.0)
    blocks = [jnp.where(g_idx == gi, in_group, 0.0) for gi in range(N_GROUPS)]
    blocks.append(jnp.zeros((LANES - N_EXPERTS, tm), F32))
    comb_t = jnp.concatenate(blocks, axis=0)
    comb_ref[...] = comb_t.T


def _merge_and_route(u, va, yb, ga, gb, x2, ws, bs, wua, wub, wo, g2, wr_hi, wr_lo, br):
    n_tok = x2.shape[0]
    tm = MERGE_TM
    row_spec = pl.BlockSpec((tm, D_MODEL), lambda i: (i, 0))
    full = lambda shape: pl.BlockSpec(shape, lambda i: (0,) * len(shape))
    return pl.pallas_call(
        _merge_kernel,
        grid=(n_tok // tm,),
        in_specs=[
            row_spec, row_spec, row_spec, row_spec, row_spec, row_spec,
            full((A_GROUPS, CHUNK, CHUNK)), full((A_GROUPS, CHUNK, 1)),
            full((D_MODEL, D_MODEL)), full((D_MODEL, D_MODEL)), full((D_MODEL, D_MODEL)),
            full((1, D_MODEL)),
            full((ROUTER_ROWS, D_MODEL)), full((ROUTER_ROWS, D_MODEL)), full((ROUTER_ROWS, 1)),
        ],
        out_specs=[row_spec, row_spec, pl.BlockSpec((tm, LANES), lambda i: (i, 0))],
        out_shape=[
            jax.ShapeDtypeStruct((n_tok, D_MODEL), F32),
            jax.ShapeDtypeStruct((n_tok, D_MODEL), BF16),
            jax.ShapeDtypeStruct((n_tok, LANES), F32),
        ],
        compiler_params=pltpu.CompilerParams(
            dimension_semantics=("arbitrary",),
            vmem_limit_bytes=VMEM_LIMIT_BYTES),
        name="merge_and_route",
    )(u, va, yb, ga, gb, x2, ws, bs, wua, wub, wo, g2, wr_hi, wr_lo, br)


def _moe_kernel(h_ref, c_ref, x1_ref, w13_ref, w2_ref, o_ref, acc_scr, csel_scr):
    e = pl.program_id(1)
    n_steps = pl.num_programs(1)

    @pl.when(e == 0)
    def _():
        acc_scr[...] = x1_ref[...]

    for eb in range(N_EXPERTS // MOE_NE):
        @pl.when(e == eb)
        def _():
            csel_scr[...] = c_ref[:, eb * MOE_NE:(eb + 1) * MOE_NE]

    h = h_ref[...]
    hids = []
    for n in range(MOE_NE):
        ab = jnp.dot(h, w13_ref[n], preferred_element_type=F32)
        a = ab[:, :D_EXPERT]
        b = ab[:, D_EXPERT:]
        hid = (a * (1.0 / (1.0 + jnp.exp(-a)))) * b
        hids.append((hid * csel_scr[:, n:n + 1]).astype(BF16))
    hid_all = jnp.concatenate(hids, axis=1)
    acc_scr[...] += jnp.dot(hid_all, w2_ref[...], preferred_element_type=F32)

    @pl.when(e == n_steps - 1)
    def _():
        o_ref[...] = acc_scr[...]


def _moe(h2, comb, x1, w13, w2f):
    n_tok = h2.shape[0]
    tm = MOE_TM
    return pl.pallas_call(
        _moe_kernel,
        grid=(n_tok // tm, N_EXPERTS // MOE_NE),
        in_specs=[
            pl.BlockSpec((tm, D_MODEL), lambda i, e: (i, 0)),
            pl.BlockSpec((tm, LANES), lambda i, e: (i, 0)),
            pl.BlockSpec((tm, D_MODEL), lambda i, e: (i, 0)),
            pl.BlockSpec((MOE_NE, D_MODEL, 2 * D_EXPERT), lambda i, e: (e, 0, 0)),
            pl.BlockSpec((MOE_NE * D_EXPERT, D_MODEL), lambda i, e: (e, 0)),
        ],
        out_specs=pl.BlockSpec((tm, D_MODEL), lambda i, e: (i, 0)),
        out_shape=jax.ShapeDtypeStruct((n_tok, D_MODEL), F32),
        scratch_shapes=[pltpu.VMEM((tm, D_MODEL), F32), pltpu.VMEM((tm, MOE_NE), F32)],
        compiler_params=pltpu.CompilerParams(
            dimension_semantics=("arbitrary", "arbitrary"),
            vmem_limit_bytes=VMEM_LIMIT_BYTES),
        name="moe_experts",
    )(h2, comb, x1, w13, w2f)


def _segment_ones(seg):
    idx = jnp.arange(MXU_DIM) // seg
    return (idx[:, None] == idx[None, :]).astype(BF16)


def kernel(x, norm1_g, w_in, v_norm_g, w_s, b_s, q_norm_g, k_norm_g, lambda_q1, lambda_k1, lambda_q2, lambda_k2, sub_norm_g, w_up_a, w_up_b, w_gate, b_gate, w_out, norm2_g, w_rg, b_rg, w_re, b_re, w1, w3, w2):
    bsz, seq, d = x.shape
    assert d == D_MODEL and seq % PROJ_TM == 0 and seq % ATT_TQ == 0 and ATT_TQ == 2 * ATT_TK
    assert norm1_g.shape[0] == 1, "single layer"
    n_tok = bsz * seq
    assert n_tok % MOE_TM == 0 and n_tok % MERGE_TM == 0
    x2 = x.reshape(n_tok, d)

    w_cat = jnp.concatenate([w_in[0], w_gate[0]], axis=1).astype(BF16)
    bias_cat = jnp.concatenate([jnp.zeros((IN_COLS,), F32), b_gate[0]])[None, :]
    gv = v_norm_g[0].reshape(1, D_MODEL)
    gq = jnp.tile(q_norm_g[0] * (B_HEAD_DIM ** -0.5 * LOG2_E), 2 * B_HEADS)[None, :]
    gk = jnp.tile(k_norm_g[0], 2 * B_HEADS)[None, :]
    u, va, q_arr, k_arr, vt_arr, ga, gb = _input_projection(
        x2, norm1_g, w_cat, bias_cat, gv, gq, gk, _segment_ones(B_HEAD_DIM), _segment_ones(LANES),
        bsz, seq)

    lam_vecs = jnp.concatenate([lambda_q1, lambda_k1, lambda_q2, lambda_k2], axis=0)
    yb = _diff_attention(lam_vecs, sub_norm_g[0][:, None], q_arr, k_arr, vt_arr)
    yb = yb.reshape(n_tok, D_MODEL)

    pad_rows = ROUTER_ROWS - 8 - N_EXPERTS
    wr = jnp.concatenate([w_rg[0].T, jnp.zeros((4, d), F32), w_re[0].T,
                          jnp.zeros((pad_rows, d), F32)], axis=0)
    br = jnp.concatenate([b_rg[0], jnp.full((4,), NEG_BIG, F32), b_re[0],
                          jnp.zeros((pad_rows,), F32)])[:, None]
    wr_hi = wr.astype(BF16)
    wr_lo = (wr - wr_hi.astype(F32)).astype(BF16)
    x1, h2, comb = _merge_and_route(
        u, va, yb, ga, gb, x2, w_s[0], b_s[0][:, :, None],
        w_up_a[0].astype(BF16), w_up_b[0].astype(BF16), w_out[0].astype(BF16),
        norm2_g, wr_hi, wr_lo, br)

    w13 = jnp.concatenate([w1[0], w3[0]], axis=2).astype(BF16)
    w2f = w2[0].astype(BF16).reshape(N_EXPERTS * D_EXPERT, D_MODEL)
    out = _moe(h2, comb, x1, w13, w2f)
    return out.reshape(bsz, seq, d)
```

```python
import functools
import math

import jax
import jax.numpy as jnp
import numpy as np
from jax import lax
from jax.experimental import pallas as pl
from jax.experimental.pallas import tpu as pltpu

F32 = jnp.float32
BF16 = jnp.bfloat16

D_MODEL = 1024
EPS = 1e-6
A_GROUPS = 8
CHUNK = 128
B_HEADS = 8
B_HEAD_DIM = 64
B_V_DIM = 128
IN_COLS = 5 * D_MODEL
N_GROUPS = 4
EXPERTS_PER_GROUP = 8
N_EXPERTS = 32
D_EXPERT = 256
LAM_INIT = 0.8 - 0.6 * math.exp(-0.3 * 0)

LANES = 128
MXU_DIM = 256
VMEM_LIMIT_BYTES = 56 * 1024 * 1024

NEG_BIG = -1e30
LOG2_E = math.log2(math.e)
ALIBI_SLOPES = tuple(2.0 ** (-8.0 * (h + 1) / B_HEADS) for h in range(B_HEADS))
SLOPE_PARTS = 4
VT_ROWS = B_V_DIM + 16

PROJ_TM = 512
PROJ_TN = 1024
ATT_TQ = 1024
ATT_TK = 512
MERGE_TM = 512
MOE_TM = 1024
MOE_NE = 4
ROUTER_ROWS = 48


def _bf16_parts(value, n_parts):
    parts, rest = [], np.float64(value)
    for _ in range(n_parts):
        part = np.float64(np.asarray(rest, np.float32).astype(BF16).astype(np.float32))
        parts.append(float(part))
        rest = rest - part
    return parts


def _segment_rms_scale(acc, seg_ones_ref, seg):
    sq = (acc * acc).astype(BF16)
    parts = []
    for p in range(acc.shape[1] // MXU_DIM):
        parts.append(jnp.dot(sq[:, p * MXU_DIM:(p + 1) * MXU_DIM], seg_ones_ref[...],
                             preferred_element_type=F32))
    ss = jnp.concatenate(parts, axis=1)
    return lax.rsqrt(ss * (1.0 / seg) + EPS)


def _inproj_kernel(x_ref, g1_ref, w_ref, bias_ref, gv_ref, gq_ref, gk_ref, ones64_ref, ones128_ref,
                   u_ref, va_ref, q_ref, k_ref, vt_ref, ga_ref, gb_ref, h_scr, *, seq_blocks):
    i = pl.program_id(0)
    j = pl.program_id(1)
    tm = x_ref.shape[0]

    @pl.when(j == 0)
    def _():
        x = x_ref[...]
        r = lax.rsqrt(jnp.mean(x * x, axis=-1, keepdims=True) + EPS)
        h_scr[...] = ((x * r) * g1_ref[...]).astype(BF16)

    acc = jnp.dot(h_scr[...], w_ref[...], preferred_element_type=F32)
    lane = lax.broadcasted_iota(jnp.int32, (1, LANES), 1)
    lo_half = lane < B_HEAD_DIM

    @pl.when(j == 0)
    def _():
        u_ref[...] = acc.astype(BF16)

    @pl.when(j == 1)
    def _():
        r = _segment_rms_scale(acc, ones128_ref, LANES)
        va_ref[...] = ((acc * r) * gv_ref[...]).astype(BF16)

    spare_col = lane & (B_HEAD_DIM - 1)

    @pl.when(j == 2)
    def _():
        r = _segment_rms_scale(acc, ones64_ref, B_HEAD_DIM)
        qn = (acc * r) * gq_ref[...]
        for h in range(B_HEADS):
            parts = _bf16_parts(ALIBI_SLOPES[h] * LOG2_E, SLOPE_PARTS)
            cvec = jnp.zeros((1, LANES), F32)
            for n, part in enumerate(parts):
                cvec = jnp.where((spare_col == n) | (spare_col == n + SLOPE_PARTS), part, cvec)
            blk = qn[:, h * LANES:(h + 1) * LANES]
            q_ref[0, 2 * h] = jnp.where(lo_half, blk, cvec).astype(BF16)
            q_ref[0, 2 * h + 1] = jnp.where(lo_half, cvec, blk).astype(BF16)

    @pl.when(j == 3)
    def _():
        r = _segment_rms_scale(acc, ones64_ref, B_HEAD_DIM)
        kn = (acc * r) * gk_ref[...]
        pos = (i % seq_blocks) * tm + lax.broadcasted_iota(jnp.int32, (tm, LANES), 0)
        pos_lo = (pos & (MXU_DIM - 1)).astype(F32)
        pos_hi = (pos - (pos & (MXU_DIM - 1))).astype(F32)
        pvec = jnp.where(spare_col < SLOPE_PARTS, pos_lo,
                         jnp.where(spare_col < 2 * SLOPE_PARTS, pos_hi, 0.0))
        for h in range(B_HEADS):
            blk = kn[:, h * LANES:(h + 1) * LANES]
            k_ref[0, 2 * h] = jnp.where(lo_half, blk, pvec).astype(BF16)
            k_ref[0, 2 * h + 1] = jnp.where(lo_half, pvec, blk).astype(BF16)

    @pl.when(j == 4)
    def _():
        for h in range(B_HEADS):
            vt_ref[0, h, 0:B_V_DIM, :] = acc[:, h * LANES:(h + 1) * LANES].T.astype(BF16)
            vt_ref[0, h, B_V_DIM:VT_ROWS, :] = jnp.ones((VT_ROWS - B_V_DIM, tm), BF16)

    @pl.when(j == 5)
    def _():
        ga_ref[...] = (1.0 / (1.0 + jnp.exp(-(acc + bias_ref[...])))).astype(BF16)

    @pl.when(j == 6)
    def _():
        gb_ref[...] = (1.0 / (1.0 + jnp.exp(-(acc + bias_ref[...])))).astype(BF16)


def _input_projection(x2, g1, w_cat, bias_cat, gv, gq, gk, ones64, ones128, bsz, seq):
    n_tok = x2.shape[0]
    tm, tn = PROJ_TM, PROJ_TN
    seq_blocks = seq // tm
    n_col = w_cat.shape[1] // tn
    row_spec = pl.BlockSpec((tm, D_MODEL), lambda i, j: (i, 0))
    vec_spec = pl.BlockSpec((1, D_MODEL), lambda i, j: (0, 0))
    head_map = lambda i, j: (i // seq_blocks, 0, i % seq_blocks, 0)
    tok_bf16 = jax.ShapeDtypeStruct((n_tok, D_MODEL), BF16)
    return pl.pallas_call(
        functools.partial(_inproj_kernel, seq_blocks=seq_blocks),
        grid=(n_tok // tm, n_col),
        in_specs=[
            row_spec,
            vec_spec,
            pl.BlockSpec((D_MODEL, tn), lambda i, j: (0, j)),
            pl.BlockSpec((1, tn), lambda i, j: (0, j)),
            vec_spec, vec_spec, vec_spec,
            pl.BlockSpec((MXU_DIM, MXU_DIM), lambda i, j: (0, 0)),
            pl.BlockSpec((MXU_DIM, MXU_DIM), lambda i, j: (0, 0)),
        ],
        out_specs=[
            row_spec, row_spec,
            pl.BlockSpec((1, 2 * B_HEADS, tm, LANES), head_map),
            pl.BlockSpec((1, 2 * B_HEADS, tm, LANES), head_map),
            pl.BlockSpec((1, B_HEADS, VT_ROWS, tm),
                         lambda i, j: (i // seq_blocks, 0, 0, i % seq_blocks)),
            row_spec, row_spec,
        ],
        out_shape=[
            tok_bf16, tok_bf16,
            jax.ShapeDtypeStruct((bsz, 2 * B_HEADS, seq, LANES), BF16),
            jax.ShapeDtypeStruct((bsz, 2 * B_HEADS, seq, LANES), BF16),
            jax.ShapeDtypeStruct((bsz, B_HEADS, VT_ROWS, seq), BF16),
            tok_bf16, tok_bf16,
        ],
        scratch_shapes=[pltpu.VMEM((tm, D_MODEL), BF16)],
        compiler_params=pltpu.CompilerParams(
            dimension_semantics=("arbitrary", "arbitrary"),
            vmem_limit_bytes=VMEM_LIMIT_BYTES),
        name="input_projection",
    )(x2, g1, w_cat, bias_cat, gv, gq, gk, ones64, ones128)


def _attn_kernel(lam_ref, sg_ref, q_ref, k_ref, vt_ref, o_ref, acc_scr, sa_scr, sb_scr):
    qi = pl.program_id(2)
    tq = q_ref.shape[2]
    tk = ATT_TK
    nt_dims = (((1,), (1,)), ((), ()))

    assert tq == 2 * tk
    tri = (lax.broadcasted_iota(jnp.int32, (tk, tk), 0)
           <= lax.broadcasted_iota(jnp.int32, (tk, tk), 1))

    def scores(kb, s_scr, diag=None):
        ks = pl.multiple_of(kb * tk, tk)
        maxima = []
        for c in range(2):
            k_blk = k_ref[0, c, pl.ds(ks, tk), :]
            if diag is None:
                s = lax.dot_general(k_blk, q_ref[0, c], nt_dims, preferred_element_type=F32)
                s_scr[c] = s
                maxima.append(jnp.max(s, axis=0, keepdims=True))
                continue
            parts = []
            for half in range(diag, 2):
                cols = slice(half * tk, (half + 1) * tk)
                s = lax.dot_general(k_blk, q_ref[0, c, cols, :], nt_dims,
                                    preferred_element_type=F32)
                if half == diag:
                    s = jnp.where(tri, s, NEG_BIG)
                s_scr[c, :, cols] = s
                parts.append(jnp.max(s, axis=0, keepdims=True))
            maxima.append(parts[0] if len(parts) == 1 else jnp.concatenate(parts, axis=1))
        return tuple(maxima)

    def consume(kb, s_scr, m_old, blk_max, col0=0, first=False):
        ks = pl.multiple_of(kb * tk, tk)
        vt = vt_ref[0, 0, :, pl.ds(ks, tk)]
        m_new = []
        for c in range(2):
            m_c = blk_max[c] if first else jnp.maximum(m_old[c][:, col0:], blk_max[c])
            p = jnp.exp2(s_scr[c, :, col0:] - m_c).astype(BF16)
            pv = jnp.dot(vt, p, preferred_element_type=F32)
            cols = slice(c * tq + col0, (c + 1) * tq)
            if first:
                acc_scr[:, cols] = pv
            else:
                acc_scr[:, cols] = acc_scr[:, cols] * jnp.exp2(m_old[c][:, col0:] - m_c) + pv
            m_new.append(m_c)
        return tuple(m_new)

    max_a = scores(2 * qi, sa_scr, diag=0)
    max_b = scores(2 * qi + 1, sb_scr, diag=1)
    m = consume(2 * qi, sa_scr, None, max_a, first=True)
    max_a = scores(0, sa_scr)
    m_right = consume(2 * qi + 1, sb_scr, m, max_b, col0=tk)
    m = tuple(jnp.concatenate([m[c][:, :tk], m_right[c]], axis=1) for c in range(2))

    def trip(t, carry):
        m, max_a = carry
        max_b = scores(2 * t + 1, sb_scr)
        m = consume(2 * t, sa_scr, m, max_a)
        max_a = scores(2 * t + 2, sa_scr)
        m = consume(2 * t + 1, sb_scr, m, max_b)
        return m, max_a

    m, max_a = lax.fori_loop(0, qi - 1, trip, (m, max_a))

    @pl.when(qi > 0)
    def _():
        max_b = scores(2 * qi - 1, sb_scr)
        m_mid = consume(2 * qi - 2, sa_scr, m, max_a)
        consume(2 * qi - 1, sb_scr, m_mid, max_b)

    lam_v = lam_ref[...]
    lam = (jnp.exp(jnp.sum(lam_v[0:1] * lam_v[1:2], axis=1, keepdims=True))
           - jnp.exp(jnp.sum(lam_v[2:3] * lam_v[3:4], axis=1, keepdims=True)) + LAM_INIT)
    o0 = acc_scr[0:B_V_DIM, 0:tq] / acc_scr[B_V_DIM:B_V_DIM + 1, 0:tq]
    o1 = acc_scr[0:B_V_DIM, tq:2 * tq] / acc_scr[B_V_DIM:B_V_DIM + 1, tq:2 * tq]
    o = o0 - lam * o1
    r = lax.rsqrt(jnp.mean(o * o, axis=0, keepdims=True) + EPS)
    o = ((o * r) * sg_ref[...]) * (1.0 - LAM_INIT)
    o_ref[0] = o.T.astype(BF16)


def _diff_attention(lam_vecs, sub_g, q_arr, k_arr, vt_arr):
    bsz, _, seq, _ = q_arr.shape
    tq = ATT_TQ
    return pl.pallas_call(
        _attn_kernel,
        grid=(bsz, B_HEADS, seq // tq),
        in_specs=[
            pl.BlockSpec((4, B_HEAD_DIM), lambda b, h, qi: (0, 0)),
            pl.BlockSpec((B_V_DIM, 1), lambda b, h, qi: (0, 0)),
            pl.BlockSpec((1, 2, tq, LANES), lambda b, h, qi: (b, h, qi, 0)),
            pl.BlockSpec((1, 2, seq, LANES), lambda b, h, qi: (b, h, 0, 0)),
            pl.BlockSpec((1, 1, VT_ROWS, seq), lambda b, h, qi: (b, h, 0, 0)),
        ],
        out_specs=pl.BlockSpec((1, tq, B_V_DIM), lambda b, h, qi: (b, qi, h)),
        out_shape=jax.ShapeDtypeStruct((bsz, seq, B_HEADS * B_V_DIM), BF16),
        scratch_shapes=[pltpu.VMEM((VT_ROWS, 2 * tq), F32),
                        pltpu.VMEM((2, ATT_TK, tq), F32),
                        pltpu.VMEM((2, ATT_TK, tq), F32)],
        compiler_params=pltpu.CompilerParams(
            dimension_semantics=("arbitrary", "arbitrary", "arbitrary"),
            vmem_limit_bytes=VMEM_LIMIT_BYTES),
        name="diff_attention",
    )(lam_vecs, sub_g, q_arr, k_arr, vt_arr)


def _first_argmax(vals, iota, top):
    return jnp.min(jnp.where(vals == top, iota, vals.shape[0]), axis=0, keepdims=True)


def _merge_kernel(u_ref, va_ref, yb_ref, ga_ref, gb_ref, x_ref, ws_ref, bs_ref,
                  wua_ref, wub_ref, wo_ref, g2_ref, wr_hi_ref, wr_lo_ref, br_ref,
                  x1_ref, h2_ref, comb_ref):
    tm = x_ref.shape[0]
    row = lax.broadcasted_iota(jnp.int32, (CHUNK, CHUNK), 0)
    col = lax.broadcasted_iota(jnp.int32, (CHUNK, CHUNK), 1)
    causal = row >= col

    group_cols = []
    for g in range(A_GROUPS):
        w_g = jnp.where(causal, ws_ref[g], 0.0).astype(BF16)
        b_g = bs_ref[g]
        chunks = []
        for c in range(tm // CHUNK):
            v_cg = va_ref[c * CHUNK:(c + 1) * CHUNK, g * LANES:(g + 1) * LANES]
            z = jnp.dot(w_g, v_cg, preferred_element_type=F32) + b_g
            u_cg = u_ref[c * CHUNK:(c + 1) * CHUNK, g * LANES:(g + 1) * LANES].astype(F32)
            chunks.append((u_cg * z).astype(BF16))
        group_cols.append(jnp.concatenate(chunks, axis=0))
    y_a = jnp.concatenate(group_cols, axis=1)

    up_a = jnp.dot(y_a, wua_ref[...], preferred_element_type=F32)
    up_b = jnp.dot(yb_ref[...], wub_ref[...], preferred_element_type=F32)
    merged = ga_ref[...].astype(F32) * up_a + gb_ref[...].astype(F32) * up_b
    x1 = x_ref[...] + jnp.dot(merged.astype(BF16), wo_ref[...], preferred_element_type=F32)
    x1_ref[...] = x1

    r = lax.rsqrt(jnp.mean(x1 * x1, axis=-1, keepdims=True) + EPS)
    h2 = (x1 * r) * g2_ref[...]
    h2_hi = h2.astype(BF16)
    h2_ref[...] = h2_hi
    h2_lo = (h2 - h2_hi.astype(F32)).astype(BF16)

    nt_dims = (((1,), (1,)), ((), ()))
    lt = (lax.dot_general(wr_hi_ref[...], h2_hi, nt_dims, preferred_element_type=F32)
          + lax.dot_general(wr_hi_ref[...], h2_lo, nt_dims, preferred_element_type=F32)
          + lax.dot_general(wr_lo_ref[...], h2_hi, nt_dims, preferred_element_type=F32)
          + br_ref[...])

    iota8 = lax.broadcasted_iota(jnp.int32, (8, tm), 0)
    gl = lt[0:8]
    ge = jnp.exp(gl - jnp.max(gl, axis=0, keepdims=True))
    gp = ge / jnp.sum(ge, axis=0, keepdims=True)
    g_gate = jnp.max(gp, axis=0, keepdims=True)
    g_idx = _first_argmax(gp, iota8, g_gate)

    el = jnp.zeros((EXPERTS_PER_GROUP, tm), F32)
    for gi in range(N_GROUPS):
        el = jnp.where(g_idx == gi, lt[8 + 8 * gi:16 + 8 * gi], el)
    ee = jnp.exp(el - jnp.max(el, axis=0, keepdims=True))
    ep = ee / jnp.sum(ee, axis=0, keepdims=True)
    p_top1 = jnp.max(ep, axis=0, keepdims=True)
    i_top1 = _first_argmax(ep, iota8, p_top1)
    ep_rest = jnp.where(iota8 == i_top1, -1.0, ep)
    p_top2 = jnp.max(ep_rest, axis=0, keepdims=True)
    i_top2 = _first_argmax(ep_rest, iota8, p_top2)
    den = p_top1 + p_top2
    w_top1 = g_gate * (p_top1 / den)
    w_top2 = g_gate * (p_top2 / den)
    in_group = jnp.where(iota8 == i_top1, w_top1, 0.0) + jnp.where(iota8 == i_top2, w_top2, 0.0)
    blocks = [jnp.where(g_idx == gi, in_group, 0.0) for gi in range(N_GROUPS)]
    blocks.append(jnp.zeros((LANES - N_EXPERTS, tm), F32))
    comb_t = jnp.concatenate(blocks, axis=0)
    comb_ref[...] = comb_t.T


def _merge_and_route(u, va, yb, ga, gb, x2, ws, bs, wua, wub, wo, g2, wr_hi, wr_lo, br):
    n_tok = x2.shape[0]
    tm = MERGE_TM
    row_spec = pl.BlockSpec((tm, D_MODEL), lambda i: (i, 0))
    full = lambda shape: pl.BlockSpec(shape, lambda i: (0,) * len(shape))
    return pl.pallas_call(
        _merge_kernel,
        grid=(n_tok // tm,),
        in_specs=[
            row_spec, row_spec, row_spec, row_spec, row_spec, row_spec,
            full((A_GROUPS, CHUNK, CHUNK)), full((A_GROUPS, CHUNK, 1)),
            full((D_MODEL, D_MODEL)), full((D_MODEL, D_MODEL)), full((D_MODEL, D_MODEL)),
            full((1, D_MODEL)),
            full((ROUTER_ROWS, D_MODEL)), full((ROUTER_ROWS, D_MODEL)), full((ROUTER_ROWS, 1)),
        ],
        out_specs=[row_spec, row_spec, pl.BlockSpec((tm, LANES), lambda i: (i, 0))],
        out_shape=[
            jax.ShapeDtypeStruct((n_tok, D_MODEL), F32),
            jax.ShapeDtypeStruct((n_tok, D_MODEL), BF16),
            jax.ShapeDtypeStruct((n_tok, LANES), F32),
        ],
        compiler_params=pltpu.CompilerParams(
            dimension_semantics=("arbitrary",),
            vmem_limit_bytes=VMEM_LIMIT_BYTES),
        name="merge_and_route",
    )(u, va, yb, ga, gb, x2, ws, bs, wua, wub, wo, g2, wr_hi, wr_lo, br)


def _moe_kernel(h_ref, c_ref, x1_ref, w13_ref, w2_ref, o_ref, acc_scr, csel_scr):
    e = pl.program_id(1)
    n_steps = pl.num_programs(1)

    @pl.when(e == 0)
    def _():
        acc_scr[...] = x1_ref[...]

    for eb in range(N_EXPERTS // MOE_NE):
        @pl.when(e == eb)
        def _():
            csel_scr[...] = c_ref[:, eb * MOE_NE:(eb + 1) * MOE_NE]

    h = h_ref[...]
    hids = []
    for n in range(MOE_NE):
        ab = jnp.dot(h, w13_ref[n], preferred_element_type=F32)
        a = ab[:, :D_EXPERT]
        b = ab[:, D_EXPERT:]
        hid = (a * (1.0 / (1.0 + jnp.exp(-a)))) * b
        hids.append((hid * csel_scr[:, n:n + 1]).astype(BF16))
    hid_all = jnp.concatenate(hids, axis=1)
    acc_scr[...] += jnp.dot(hid_all, w2_ref[...], preferred_element_type=F32)

    @pl.when(e == n_steps - 1)
    def _():
        o_ref[...] = acc_scr[...]


def _moe(h2, comb, x1, w13, w2f):
    n_tok = h2.shape[0]
    tm = MOE_TM
    return pl.pallas_call(
        _moe_kernel,
        grid=(n_tok // tm, N_EXPERTS // MOE_NE),
        in_specs=[
            pl.BlockSpec((tm, D_MODEL), lambda i, e: (i, 0)),
            pl.BlockSpec((tm, LANES), lambda i, e: (i, 0)),
            pl.BlockSpec((tm, D_MODEL), lambda i, e: (i, 0)),
            pl.BlockSpec((MOE_NE, D_MODEL, 2 * D_EXPERT), lambda i, e: (e, 0, 0)),
            pl.BlockSpec((MOE_NE * D_EXPERT, D_MODEL), lambda i, e: (e, 0)),
        ],
        out_specs=pl.BlockSpec((tm, D_MODEL), lambda i, e: (i, 0)),
        out_shape=jax.ShapeDtypeStruct((n_tok, D_MODEL), F32),
        scratch_shapes=[pltpu.VMEM((tm, D_MODEL), F32), pltpu.VMEM((tm, MOE_NE), F32)],
        compiler_params=pltpu.CompilerParams(
            dimension_semantics=("arbitrary", "arbitrary"),
            vmem_limit_bytes=VMEM_LIMIT_BYTES),
        name="moe_experts",
    )(h2, comb, x1, w13, w2f)


def _segment_ones(seg):
    idx = jnp.arange(MXU_DIM) // seg
    return (idx[:, None] == idx[None, :]).astype(BF16)


def kernel(x, norm1_g, w_in, v_norm_g, w_s, b_s, q_norm_g, k_norm_g, lambda_q1, lambda_k1, lambda_q2, lambda_k2, sub_norm_g, w_up_a, w_up_b, w_gate, b_gate, w_out, norm2_g, w_rg, b_rg, w_re, b_re, w1, w3, w2):
    bsz, seq, d = x.shape
    assert d == D_MODEL and seq % PROJ_TM == 0 and seq % ATT_TQ == 0 and ATT_TQ == 2 * ATT_TK
    assert norm1_g.shape[0] == 1, "single layer"
    n_tok = bsz * seq
    assert n_tok % MOE_TM == 0 and n_tok % MERGE_TM == 0
    x2 = x.reshape(n_tok, d)

    w_cat = jnp.concatenate([w_in[0], w_gate[0]], axis=1).astype(BF16)
    bias_cat = jnp.concatenate([jnp.zeros((IN_COLS,), F32), b_gate[0]])[None, :]
    gv = v_norm_g[0].reshape(1, D_MODEL)
    gq = jnp.tile(q_norm_g[0] * (B_HEAD_DIM ** -0.5 * LOG2_E), 2 * B_HEADS)[None, :]
    gk = jnp.tile(k_norm_g[0], 2 * B_HEADS)[None, :]
    u, va, q_arr, k_arr, vt_arr, ga, gb = _input_projection(
        x2, norm1_g, w_cat, bias_cat, gv, gq, gk, _segment_ones(B_HEAD_DIM), _segment_ones(LANES),
        bsz, seq)

    lam_vecs = jnp.concatenate([lambda_q1, lambda_k1, lambda_q2, lambda_k2], axis=0)
    yb = _diff_attention(lam_vecs, sub_norm_g[0][:, None], q_arr, k_arr, vt_arr)
    yb = yb.reshape(n_tok, D_MODEL)

    pad_rows = ROUTER_ROWS - 8 - N_EXPERTS
    wr = jnp.concatenate([w_rg[0].T, jnp.zeros((4, d), F32), w_re[0].T,
                          jnp.zeros((pad_rows, d), F32)], axis=0)
    br = jnp.concatenate([b_rg[0], jnp.full((4,), NEG_BIG, F32), b_re[0],
                          jnp.zeros((pad_rows,), F32)])[:, None]
    wr_hi = wr.astype(BF16)
    wr_lo = (wr - wr_hi.astype(F32)).astype(BF16)
    x1, h2, comb = _merge_and_route(
        u, va, yb, ga, gb, x2, w_s[0], b_s[0][:, :, None],
        w_up_a[0].astype(BF16), w_up_b[0].astype(BF16), w_out[0].astype(BF16),
        norm2_g, wr_hi, wr_lo, br)

    w13 = jnp.concatenate([w1[0], w3[0]], axis=2).astype(BF16)
    w2f = w2[0].astype(BF16).reshape(N_EXPERTS * D_EXPERT, D_MODEL)
    out = _moe(h2, comb, x1, w13, w2f)
    return out.reshape(bsz, seq, d)
```

```python
import functools
import math

import jax
import jax.numpy as jnp
import numpy as np
from jax import lax
from jax.experimental import pallas as pl
from jax.experimental.pallas import tpu as pltpu

F32 = jnp.float32
BF16 = jnp.bfloat16

D_MODEL = 1024
EPS = 1e-6
A_GROUPS = 8
CHUNK = 128
B_HEADS = 8
B_HEAD_DIM = 64
B_V_DIM = 128
IN_COLS = 5 * D_MODEL
N_GROUPS = 4
EXPERTS_PER_GROUP = 8
N_EXPERTS = 32
D_EXPERT = 256
LAM_INIT = 0.8 - 0.6 * math.exp(-0.3 * 0)

LANES = 128
MXU_DIM = 256
VMEM_LIMIT_BYTES = 56 * 1024 * 1024

NEG_BIG = -1e30
LOG2_E = math.log2(math.e)
ALIBI_SLOPES = tuple(2.0 ** (-8.0 * (h + 1) / B_HEADS) for h in range(B_HEADS))
SLOPE_PARTS = 4
VT_ROWS = B_V_DIM + 16

PROJ_TM = 512
PROJ_TN = 1024
ATT_TQ = 1024
ATT_TK = 512
MERGE_TM = 512
MOE_TM = 1024
MOE_NE = 4
MOE_ROW_BLOCK = 128
ROUTER_ROWS = 48


def _bf16_parts(value, n_parts):
    parts, rest = [], np.float64(value)
    for _ in range(n_parts):
        part = np.float64(np.asarray(rest, np.float32).astype(BF16).astype(np.float32))
        parts.append(float(part))
        rest = rest - part
    return parts


def _segment_rms_scale(acc, seg_ones_ref, seg):
    sq = (acc * acc).astype(BF16)
    parts = []
    for p in range(acc.shape[1] // MXU_DIM):
        parts.append(jnp.dot(sq[:, p * MXU_DIM:(p + 1) * MXU_DIM], seg_ones_ref[...],
                             preferred_element_type=F32))
    ss = jnp.concatenate(parts, axis=1)
    return lax.rsqrt(ss * (1.0 / seg) + EPS)


def _inproj_kernel(x_ref, g1_ref, w_ref, bias_ref, gv_ref, gq_ref, gk_ref, ones64_ref, ones128_ref,
                   u_ref, va_ref, q_ref, k_ref, vt_ref, ga_ref, gb_ref, h_scr, *, seq_blocks):
    i = pl.program_id(0)
    j = pl.program_id(1)
    tm = x_ref.shape[0]

    @pl.when(j == 0)
    def _():
        x = x_ref[...]
        r = lax.rsqrt(jnp.mean(x * x, axis=-1, keepdims=True) + EPS)
        h_scr[...] = ((x * r) * g1_ref[...]).astype(BF16)

    acc = jnp.dot(h_scr[...], w_ref[...], preferred_element_type=F32)
    lane = lax.broadcasted_iota(jnp.int32, (1, LANES), 1)
    lo_half = lane < B_HEAD_DIM

    @pl.when(j == 0)
    def _():
        u_ref[...] = acc.astype(BF16)

    @pl.when(j == 1)
    def _():
        r = _segment_rms_scale(acc, ones128_ref, LANES)
        va_ref[...] = ((acc * r) * gv_ref[...]).astype(BF16)

    spare_col = lane & (B_HEAD_DIM - 1)

    @pl.when(j == 2)
    def _():
        r = _segment_rms_scale(acc, ones64_ref, B_HEAD_DIM)
        qn = (acc * r) * gq_ref[...]
        for h in range(B_HEADS):
            parts = _bf16_parts(ALIBI_SLOPES[h] * LOG2_E, SLOPE_PARTS)
            cvec = jnp.zeros((1, LANES), F32)
            for n, part in enumerate(parts):
                cvec = jnp.where((spare_col == n) | (spare_col == n + SLOPE_PARTS), part, cvec)
            blk = qn[:, h * LANES:(h + 1) * LANES]
            q_ref[0, 2 * h] = jnp.where(lo_half, blk, cvec).astype(BF16)
            q_ref[0, 2 * h + 1] = jnp.where(lo_half, cvec, blk).astype(BF16)

    @pl.when(j == 3)
    def _():
        r = _segment_rms_scale(acc, ones64_ref, B_HEAD_DIM)
        kn = (acc * r) * gk_ref[...]
        pos = (i % seq_blocks) * tm + lax.broadcasted_iota(jnp.int32, (tm, LANES), 0)
        pos_lo = (pos & (MXU_DIM - 1)).astype(F32)
        pos_hi = (pos - (pos & (MXU_DIM - 1))).astype(F32)
        pvec = jnp.where(spare_col < SLOPE_PARTS, pos_lo,
                         jnp.where(spare_col < 2 * SLOPE_PARTS, pos_hi, 0.0))
        for h in range(B_HEADS):
            blk = kn[:, h * LANES:(h + 1) * LANES]
            k_ref[0, 2 * h] = jnp.where(lo_half, blk, pvec).astype(BF16)
            k_ref[0, 2 * h + 1] = jnp.where(lo_half, pvec, blk).astype(BF16)

    @pl.when(j == 4)
    def _():
        for h in range(B_HEADS):
            vt_ref[0, h, 0:B_V_DIM, :] = acc[:, h * LANES:(h + 1) * LANES].T.astype(BF16)
            vt_ref[0, h, B_V_DIM:VT_ROWS, :] = jnp.ones((VT_ROWS - B_V_DIM, tm), BF16)

    @pl.when(j == 5)
    def _():
        ga_ref[...] = (1.0 / (1.0 + jnp.exp(-(acc + bias_ref[...])))).astype(BF16)

    @pl.when(j == 6)
    def _():
        gb_ref[...] = (1.0 / (1.0 + jnp.exp(-(acc + bias_ref[...])))).astype(BF16)


def _input_projection(x2, g1, w_cat, bias_cat, gv, gq, gk, ones64, ones128, bsz, seq):
    n_tok = x2.shape[0]
    tm, tn = PROJ_TM, PROJ_TN
    seq_blocks = seq // tm
    n_col = w_cat.shape[1] // tn
    row_spec = pl.BlockSpec((tm, D_MODEL), lambda i, j: (i, 0))
    vec_spec = pl.BlockSpec((1, D_MODEL), lambda i, j: (0, 0))
    head_map = lambda i, j: (i // seq_blocks, 0, i % seq_blocks, 0)
    tok_bf16 = jax.ShapeDtypeStruct((n_tok, D_MODEL), BF16)
    return pl.pallas_call(
        functools.partial(_inproj_kernel, seq_blocks=seq_blocks),
        grid=(n_tok // tm, n_col),
        in_specs=[
            row_spec,
            vec_spec,
            pl.BlockSpec((D_MODEL, tn), lambda i, j: (0, j)),
            pl.BlockSpec((1, tn), lambda i, j: (0, j)),
            vec_spec, vec_spec, vec_spec,
            pl.BlockSpec((MXU_DIM, MXU_DIM), lambda i, j: (0, 0)),
            pl.BlockSpec((MXU_DIM, MXU_DIM), lambda i, j: (0, 0)),
        ],
        out_specs=[
            row_spec, row_spec,
            pl.BlockSpec((1, 2 * B_HEADS, tm, LANES), head_map),
            pl.BlockSpec((1, 2 * B_HEADS, tm, LANES), head_map),
            pl.BlockSpec((1, B_HEADS, VT_ROWS, tm),
                         lambda i, j: (i // seq_blocks, 0, 0, i % seq_blocks)),
            row_spec, row_spec,
        ],
        out_shape=[
            tok_bf16, tok_bf16,
            jax.ShapeDtypeStruct((bsz, 2 * B_HEADS, seq, LANES), BF16),
            jax.ShapeDtypeStruct((bsz, 2 * B_HEADS, seq, LANES), BF16),
            jax.ShapeDtypeStruct((bsz, B_HEADS, VT_ROWS, seq), BF16),
            tok_bf16, tok_bf16,
        ],
        scratch_shapes=[pltpu.VMEM((tm, D_MODEL), BF16)],
        compiler_params=pltpu.CompilerParams(
            dimension_semantics=("arbitrary", "arbitrary"),
            vmem_limit_bytes=VMEM_LIMIT_BYTES),
        name="input_projection",
    )(x2, g1, w_cat, bias_cat, gv, gq, gk, ones64, ones128)


def _attn_kernel(lam_ref, sg_ref, q_ref, k_ref, vt_ref, o_ref, acc_scr, sa_scr, sb_scr):
    qi = pl.program_id(2)
    tq = q_ref.shape[2]
    tk = ATT_TK
    nt_dims = (((1,), (1,)), ((), ()))

    assert tq == 2 * tk
    tri = (lax.broadcasted_iota(jnp.int32, (tk, tk), 0)
           <= lax.broadcasted_iota(jnp.int32, (tk, tk), 1))

    def scores(kb, s_scr, diag=None):
        ks = pl.multiple_of(kb * tk, tk)
        maxima = []
        for c in range(2):
            k_blk = k_ref[0, c, pl.ds(ks, tk), :]
            if diag is None:
                s = lax.dot_general(k_blk, q_ref[0, c], nt_dims, preferred_element_type=F32)
                s_scr[c] = s
                maxima.append(jnp.max(s, axis=0, keepdims=True))
                continue
            parts = []
            for half in range(diag, 2):
                cols = slice(half * tk, (half + 1) * tk)
                s = lax.dot_general(k_blk, q_ref[0, c, cols, :], nt_dims,
                                    preferred_element_type=F32)
                if half == diag:
                    s = jnp.where(tri, s, NEG_BIG)
                s_scr[c, :, cols] = s
                parts.append(jnp.max(s, axis=0, keepdims=True))
            maxima.append(parts[0] if len(parts) == 1 else jnp.concatenate(parts, axis=1))
        return tuple(maxima)

    def consume(kb, s_scr, m_old, blk_max, col0=0, first=False):
        ks = pl.multiple_of(kb * tk, tk)
        vt = vt_ref[0, 0, :, pl.ds(ks, tk)]
        m_new = []
        for c in range(2):
            m_c = blk_max[c] if first else jnp.maximum(m_old[c][:, col0:], blk_max[c])
            p = jnp.exp2(s_scr[c, :, col0:] - m_c).astype(BF16)
            pv = jnp.dot(vt, p, preferred_element_type=F32)
            cols = slice(c * tq + col0, (c + 1) * tq)
            if first:
                acc_scr[:, cols] = pv
            else:
                acc_scr[:, cols] = acc_scr[:, cols] * jnp.exp2(m_old[c][:, col0:] - m_c) + pv
            m_new.append(m_c)
        return tuple(m_new)

    max_a = scores(2 * qi, sa_scr, diag=0)
    max_b = scores(2 * qi + 1, sb_scr, diag=1)
    m = consume(2 * qi, sa_scr, None, max_a, first=True)
    max_a = scores(0, sa_scr)
    m_right = consume(2 * qi + 1, sb_scr, m, max_b, col0=tk)
    m = tuple(jnp.concatenate([m[c][:, :tk], m_right[c]], axis=1) for c in range(2))

    def trip(t, carry):
        m, max_a = carry
        max_b = scores(2 * t + 1, sb_scr)
        m = consume(2 * t, sa_scr, m, max_a)
        max_a = scores(2 * t + 2, sa_scr)
        m = consume(2 * t + 1, sb_scr, m, max_b)
        return m, max_a

    m, max_a = lax.fori_loop(0, qi - 1, trip, (m, max_a))

    @pl.when(qi > 0)
    def _():
        max_b = scores(2 * qi - 1, sb_scr)
        m_mid = consume(2 * qi - 2, sa_scr, m, max_a)
        consume(2 * qi - 1, sb_scr, m_mid, max_b)

    lam_v = lam_ref[...]
    lam = (jnp.exp(jnp.sum(lam_v[0:1] * lam_v[1:2], axis=1, keepdims=True))
           - jnp.exp(jnp.sum(lam_v[2:3] * lam_v[3:4], axis=1, keepdims=True)) + LAM_INIT)
    o0 = acc_scr[0:B_V_DIM, 0:tq] / acc_scr[B_V_DIM:B_V_DIM + 1, 0:tq]
    o1 = acc_scr[0:B_V_DIM, tq:2 * tq] / acc_scr[B_V_DIM:B_V_DIM + 1, tq:2 * tq]
    o = o0 - lam * o1
    r = lax.rsqrt(jnp.mean(o * o, axis=0, keepdims=True) + EPS)
    o = ((o * r) * sg_ref[...]) * (1.0 - LAM_INIT)
    o_ref[0] = o.T.astype(BF16)


def _diff_attention(lam_vecs, sub_g, q_arr, k_arr, vt_arr):
    bsz, _, seq, _ = q_arr.shape
    tq = ATT_TQ
    return pl.pallas_call(
        _attn_kernel,
        grid=(bsz, B_HEADS, seq // tq),
        in_specs=[
            pl.BlockSpec((4, B_HEAD_DIM), lambda b, h, qi: (0, 0)),
            pl.BlockSpec((B_V_DIM, 1), lambda b, h, qi: (0, 0)),
            pl.BlockSpec((1, 2, tq, LANES), lambda b, h, qi: (b, h, qi, 0)),
            pl.BlockSpec((1, 2, seq, LANES), lambda b, h, qi: (b, h, 0, 0)),
            pl.BlockSpec((1, 1, VT_ROWS, seq), lambda b, h, qi: (b, h, 0, 0)),
        ],
        out_specs=pl.BlockSpec((1, tq, B_V_DIM), lambda b, h, qi: (b, qi, h)),
        out_shape=jax.ShapeDtypeStruct((bsz, seq, B_HEADS * B_V_DIM), BF16),
        scratch_shapes=[pltpu.VMEM((VT_ROWS, 2 * tq), F32),
                        pltpu.VMEM((2, ATT_TK, tq), F32),
                        pltpu.VMEM((2, ATT_TK, tq), F32)],
        compiler_params=pltpu.CompilerParams(
            dimension_semantics=("arbitrary", "arbitrary", "arbitrary"),
            vmem_limit_bytes=VMEM_LIMIT_BYTES),
        name="diff_attention",
    )(lam_vecs, sub_g, q_arr, k_arr, vt_arr)


def _first_argmax(vals, iota, top):
    return jnp.min(jnp.where(vals == top, iota, vals.shape[0]), axis=0, keepdims=True)


def _merge_kernel(u_ref, va_ref, yb_ref, ga_ref, gb_ref, x_ref, ws_ref, bs_ref,
                  wua_ref, wub_ref, wo_ref, g2_ref, wr_hi_ref, wr_lo_ref, br_ref,
                  x1_ref, h2_ref, comb_ref, gidx_ref):
    tm = x_ref.shape[0]
    row = lax.broadcasted_iota(jnp.int32, (CHUNK, CHUNK), 0)
    col = lax.broadcasted_iota(jnp.int32, (CHUNK, CHUNK), 1)
    causal = row >= col

    group_cols = []
    for g in range(A_GROUPS):
        w_g = jnp.where(causal, ws_ref[g], 0.0).astype(BF16)
        b_g = bs_ref[g]
        chunks = []
        for c in range(tm // CHUNK):
            v_cg = va_ref[c * CHUNK:(c + 1) * CHUNK, g * LANES:(g + 1) * LANES]
            z = jnp.dot(w_g, v_cg, preferred_element_type=F32) + b_g
            u_cg = u_ref[c * CHUNK:(c + 1) * CHUNK, g * LANES:(g + 1) * LANES].astype(F32)
            chunks.append((u_cg * z).astype(BF16))
        group_cols.append(jnp.concatenate(chunks, axis=0))
    y_a = jnp.concatenate(group_cols, axis=1)

    up_a = jnp.dot(y_a, wua_ref[...], preferred_element_type=F32)
    up_b = jnp.dot(yb_ref[...], wub_ref[...], preferred_element_type=F32)
    merged = ga_ref[...].astype(F32) * up_a + gb_ref[...].astype(F32) * up_b
    x1 = x_ref[...] + jnp.dot(merged.astype(BF16), wo_ref[...], preferred_element_type=F32)
    x1_ref[...] = x1

    r = lax.rsqrt(jnp.mean(x1 * x1, axis=-1, keepdims=True) + EPS)
    h2 = (x1 * r) * g2_ref[...]
    h2_hi = h2.astype(BF16)
    h2_ref[...] = h2_hi
    h2_lo = (h2 - h2_hi.astype(F32)).astype(BF16)

    nt_dims = (((1,), (1,)), ((), ()))
    lt = (lax.dot_general(wr_hi_ref[...], h2_hi, nt_dims, preferred_element_type=F32)
          + lax.dot_general(wr_hi_ref[...], h2_lo, nt_dims, preferred_element_type=F32)
          + lax.dot_general(wr_lo_ref[...], h2_hi, nt_dims, preferred_element_type=F32)
          + br_ref[...])

    iota8 = lax.broadcasted_iota(jnp.int32, (8, tm), 0)
    gl = lt[0:8]
    ge = jnp.exp(gl - jnp.max(gl, axis=0, keepdims=True))
    gp = ge / jnp.sum(ge, axis=0, keepdims=True)
    g_gate = jnp.max(gp, axis=0, keepdims=True)
    g_idx = _first_argmax(gp, iota8, g_gate)

    el = jnp.zeros((EXPERTS_PER_GROUP, tm), F32)
    for gi in range(N_GROUPS):
        el = jnp.where(g_idx == gi, lt[8 + 8 * gi:16 + 8 * gi], el)
    ee = jnp.exp(el - jnp.max(el, axis=0, keepdims=True))
    ep = ee / jnp.sum(ee, axis=0, keepdims=True)
    p_top1 = jnp.max(ep, axis=0, keepdims=True)
    i_top1 = _first_argmax(ep, iota8, p_top1)
    ep_rest = jnp.where(iota8 == i_top1, -1.0, ep)
    p_top2 = jnp.max(ep_rest, axis=0, keepdims=True)
    i_top2 = _first_argmax(ep_rest, iota8, p_top2)
    den = p_top1 + p_top2
    w_top1 = g_gate * (p_top1 / den)
    w_top2 = g_gate * (p_top2 / den)
    in_group = jnp.where(iota8 == i_top1, w_top1, 0.0) + jnp.where(iota8 == i_top2, w_top2, 0.0)
    blocks = [jnp.where(g_idx == gi, in_group, 0.0) for gi in range(N_GROUPS)]
    blocks.append(jnp.zeros((LANES - N_EXPERTS, tm), F32))
    comb_t = jnp.concatenate(blocks, axis=0)
    comb_ref[...] = comb_t.T
    gidx_ref[...] = jnp.broadcast_to(g_idx, gidx_ref.shape)


def _merge_and_route(u, va, yb, ga, gb, x2, ws, bs, wua, wub, wo, g2, wr_hi, wr_lo, br):
    n_tok = x2.shape[0]
    tm = MERGE_TM
    row_spec = pl.BlockSpec((tm, D_MODEL), lambda i: (i, 0))
    full = lambda shape: pl.BlockSpec(shape, lambda i: (0,) * len(shape))
    return pl.pallas_call(
        _merge_kernel,
        grid=(n_tok // tm,),
        in_specs=[
            row_spec, row_spec, row_spec, row_spec, row_spec, row_spec,
            full((A_GROUPS, CHUNK, CHUNK)), full((A_GROUPS, CHUNK, 1)),
            full((D_MODEL, D_MODEL)), full((D_MODEL, D_MODEL)), full((D_MODEL, D_MODEL)),
            full((1, D_MODEL)),
            full((ROUTER_ROWS, D_MODEL)), full((ROUTER_ROWS, D_MODEL)), full((ROUTER_ROWS, 1)),
        ],
        out_specs=[row_spec, row_spec, pl.BlockSpec((tm, LANES), lambda i: (i, 0)),
                   pl.BlockSpec((8, tm), lambda i: (0, i))],
        out_shape=[
            jax.ShapeDtypeStruct((n_tok, D_MODEL), F32),
            jax.ShapeDtypeStruct((n_tok, D_MODEL), BF16),
            jax.ShapeDtypeStruct((n_tok, LANES), F32),
            jax.ShapeDtypeStruct((8, n_tok), jnp.int32),
        ],
        compiler_params=pltpu.CompilerParams(
            dimension_semantics=("arbitrary",),
            vmem_limit_bytes=VMEM_LIMIT_BYTES),
        name="merge_and_route",
    )(u, va, yb, ga, gb, x2, ws, bs, wua, wub, wo, g2, wr_hi, wr_lo, br)


def _moe_kernel(g_ref, h_ref, c_ref, x1_ref, w13_ref, w2_ref, o_ref,
                p_scr, xg_scr, cg_scr, csel_scr, y_scr, bounds_ref):
    e = pl.program_id(1)
    n_steps = pl.num_programs(1)
    tt = h_ref.shape[0]
    rows = p_scr.shape[0]
    rb = MOE_ROW_BLOCK

    @pl.when(e == 0)
    def _():
        g_idx = g_ref[0:1, :]
        slot = lax.broadcasted_iota(jnp.int32, (16, tt), 0)
        onehot = jnp.where(slot == g_idx, 1.0, 0.0)
        upper = (lax.broadcasted_iota(jnp.int32, (LANES, LANES), 0)
                 <= lax.broadcasted_iota(jnp.int32, (LANES, LANES), 1)).astype(BF16)
        carry = jnp.zeros((16, 1), F32)
        counts = []
        for j in range(tt // LANES):
            seg = onehot[:, j * LANES:(j + 1) * LANES].astype(BF16)
            cs = jnp.dot(seg, upper, preferred_element_type=F32) + carry
            counts.append(cs)
            carry = cs[:, LANES - 1:LANES]
        running = jnp.concatenate(counts, axis=1)
        n_blocks = jnp.floor((carry + (rb - 1)) * (1.0 / rb))
        slot_col = lax.broadcasted_iota(jnp.int32, (16, 1), 0)
        first_block = jnp.zeros((16, 1), F32)
        for g in range(N_GROUPS - 1):
            first_block = first_block + jnp.where(slot_col > g, n_blocks[g:g + 1, :], 0.0)
        end_block = first_block + n_blocks
        for g in range(N_GROUPS):
            bounds_ref[g] = first_block[g:g + 1, :][0, 0].astype(jnp.int32)
            bounds_ref[N_GROUPS + g] = end_block[g:g + 1, :][0, 0].astype(jnp.int32)
        pos = jnp.sum(onehot * (first_block * rb + running - 1.0), axis=0, keepdims=True)
        pos = pos.astype(jnp.int32)

        c = c_ref[...]
        c_hi = c.astype(BF16)
        c_rest = c - c_hi.astype(F32)
        c_mid = c_rest.astype(BF16)
        c_lo = (c_rest - c_mid.astype(F32)).astype(BF16)
        h_ext = jnp.concatenate([h_ref[...], c_hi, c_mid, c_lo], axis=1)
        for ch in range(rows // MXU_DIM):
            sl = slice(ch * MXU_DIM, (ch + 1) * MXU_DIM)
            row_id = lax.broadcasted_iota(jnp.int32, (MXU_DIM, tt), 0) + ch * MXU_DIM
            perm = jnp.where(row_id == pos, 1.0, 0.0).astype(BF16)
            p_scr[sl, :] = perm
            got = jnp.dot(perm, h_ext, preferred_element_type=F32)
            xg_scr[sl, :] = got[:, :D_MODEL].astype(BF16)
            cg_scr[sl, :] = (got[:, D_MODEL:D_MODEL + LANES]
                             + got[:, D_MODEL + LANES:D_MODEL + 2 * LANES]
                             + got[:, D_MODEL + 2 * LANES:])
        y_scr[...] = jnp.zeros_like(y_scr)

    for eb in range(N_EXPERTS // MOE_NE):
        @pl.when(e == eb)
        def _():
            csel_scr[...] = cg_scr[:, eb * MOE_NE:(eb + 1) * MOE_NE]

    group = e // (EXPERTS_PER_GROUP // MOE_NE)

    def row_block(b, carry):
        r0 = pl.multiple_of(b * rb, rb)
        xb = xg_scr[pl.ds(r0, rb), :]
        hids = []
        for n in range(MOE_NE):
            ab = jnp.dot(xb, w13_ref[n], preferred_element_type=F32)
            a = ab[:, :D_EXPERT]
            b3 = ab[:, D_EXPERT:]
            hid = (a * (1.0 / (1.0 + jnp.exp(-a)))) * b3
            hids.append((hid * csel_scr[pl.ds(r0, rb), n:n + 1]).astype(BF16))
        hid_all = jnp.concatenate(hids, axis=1)
        y_scr[pl.ds(r0, rb), :] += jnp.dot(hid_all, w2_ref[...], preferred_element_type=F32)
        return carry

    lax.fori_loop(bounds_ref[group], bounds_ref[N_GROUPS + group], row_block, 0)

    @pl.when(e == n_steps - 1)
    def _():
        tn_dims = (((0,), (0,)), ((), ()))
        y = lax.dot_general(p_scr[...], y_scr[...].astype(BF16), tn_dims,
                            preferred_element_type=F32)
        o_ref[...] = x1_ref[...] + y


def _moe(gidx, h2, comb, x1, w13, w2f):
    n_tok = h2.shape[0]
    tt = MOE_TM
    rows = tt + N_GROUPS * MOE_ROW_BLOCK
    return pl.pallas_call(
        _moe_kernel,
        grid=(n_tok // tt, N_EXPERTS // MOE_NE),
        in_specs=[
            pl.BlockSpec((8, tt), lambda i, e: (0, i)),
            pl.BlockSpec((tt, D_MODEL), lambda i, e: (i, 0)),
            pl.BlockSpec((tt, LANES), lambda i, e: (i, 0)),
            pl.BlockSpec((tt, D_MODEL), lambda i, e: (i, 0)),
            pl.BlockSpec((MOE_NE, D_MODEL, 2 * D_EXPERT), lambda i, e: (e, 0, 0)),
            pl.BlockSpec((MOE_NE * D_EXPERT, D_MODEL), lambda i, e: (e, 0)),
        ],
        out_specs=pl.BlockSpec((tt, D_MODEL), lambda i, e: (i, 0)),
        out_shape=jax.ShapeDtypeStruct((n_tok, D_MODEL), F32),
        scratch_shapes=[
            pltpu.VMEM((rows, tt), BF16),
            pltpu.VMEM((rows, D_MODEL), BF16),
            pltpu.VMEM((rows, LANES), F32),
            pltpu.VMEM((rows, MOE_NE), F32),
            pltpu.VMEM((rows, D_MODEL), F32),
            pltpu.SMEM((2 * N_GROUPS,), jnp.int32),
        ],
        compiler_params=pltpu.CompilerParams(
            dimension_semantics=("arbitrary", "arbitrary"),
            vmem_limit_bytes=VMEM_LIMIT_BYTES),
        name="moe_experts",
    )(gidx, h2, comb, x1, w13, w2f)


def _segment_ones(seg):
    idx = jnp.arange(MXU_DIM) // seg
    return (idx[:, None] == idx[None, :]).astype(BF16)


def kernel(x, norm1_g, w_in, v_norm_g, w_s, b_s, q_norm_g, k_norm_g, lambda_q1, lambda_k1, lambda_q2, lambda_k2, sub_norm_g, w_up_a, w_up_b, w_gate, b_gate, w_out, norm2_g, w_rg, b_rg, w_re, b_re, w1, w3, w2):
    bsz, seq, d = x.shape
    assert d == D_MODEL and seq % PROJ_TM == 0 and seq % ATT_TQ == 0 and ATT_TQ == 2 * ATT_TK
    assert norm1_g.shape[0] == 1, "single layer"
    n_tok = bsz * seq
    assert n_tok % MOE_TM == 0 and n_tok % MERGE_TM == 0
    x2 = x.reshape(n_tok, d)

    w_cat = jnp.concatenate([w_in[0], w_gate[0]], axis=1).astype(BF16)
    bias_cat = jnp.concatenate([jnp.zeros((IN_COLS,), F32), b_gate[0]])[None, :]
    gv = v_norm_g[0].reshape(1, D_MODEL)
    gq = jnp.tile(q_norm_g[0] * (B_HEAD_DIM ** -0.5 * LOG2_E), 2 * B_HEADS)[None, :]
    gk = jnp.tile(k_norm_g[0], 2 * B_HEADS)[None, :]
    u, va, q_arr, k_arr, vt_arr, ga, gb = _input_projection(
        x2, norm1_g, w_cat, bias_cat, gv, gq, gk, _segment_ones(B_HEAD_DIM), _segment_ones(LANES),
        bsz, seq)

    lam_vecs = jnp.concatenate([lambda_q1, lambda_k1, lambda_q2, lambda_k2], axis=0)
    yb = _diff_attention(lam_vecs, sub_norm_g[0][:, None], q_arr, k_arr, vt_arr)
    yb = yb.reshape(n_tok, D_MODEL)

    pad_rows = ROUTER_ROWS - 8 - N_EXPERTS
    wr = jnp.concatenate([w_rg[0].T, jnp.zeros((4, d), F32), w_re[0].T,
                          jnp.zeros((pad_rows, d), F32)], axis=0)
    br = jnp.concatenate([b_rg[0], jnp.full((4,), NEG_BIG, F32), b_re[0],
                          jnp.zeros((pad_rows,), F32)])[:, None]
    wr_hi = wr.astype(BF16)
    wr_lo = (wr - wr_hi.astype(F32)).astype(BF16)
    x1, h2, comb, gidx = _merge_and_route(
        u, va, yb, ga, gb, x2, w_s[0], b_s[0][:, :, None],
        w_up_a[0].astype(BF16), w_up_b[0].astype(BF16), w_out[0].astype(BF16),
        norm2_g, wr_hi, wr_lo, br)

    w13 = jnp.concatenate([w1[0], w3[0]], axis=2).astype(BF16)
    w2f = w2[0].astype(BF16).reshape(N_EXPERTS * D_EXPERT, D_MODEL)
    out = _moe(gidx, h2, comb, x1, w13, w2f)
    return out.reshape(bsz, seq, d)
```

```python
import functools
import math

import jax
import jax.numpy as jnp
import numpy as np
from jax import lax
from jax.experimental import pallas as pl
from jax.experimental.pallas import tpu as pltpu

F32 = jnp.float32
BF16 = jnp.bfloat16

D_MODEL = 1024
EPS = 1e-6
A_GROUPS = 8
CHUNK = 128
B_HEADS = 8
B_HEAD_DIM = 64
B_V_DIM = 128
IN_COLS = 5 * D_MODEL
N_GROUPS = 4
EXPERTS_PER_GROUP = 8
N_EXPERTS = 32
D_EXPERT = 256
LAM_INIT = 0.8 - 0.6 * math.exp(-0.3 * 0)

LANES = 128
MXU_DIM = 256
VMEM_LIMIT_BYTES = 56 * 1024 * 1024

NEG_BIG = -1e30
LOG2_E = math.log2(math.e)
ALIBI_SLOPES = tuple(2.0 ** (-8.0 * (h + 1) / B_HEADS) for h in range(B_HEADS))
SLOPE_PARTS = 4
VT_ROWS = B_V_DIM + 16

PROJ_TM = 512
ATT_TQ = 1024
ATT_TK = 512
MERGE_TM = 512
MOE_TM = 1024
MOE_NE = 4
MOE_ROW_BLOCK = 128
ROUTER_ROWS = 48


def _bf16_parts(value, n_parts):
    parts, rest = [], np.float64(value)
    for _ in range(n_parts):
        part = np.float64(np.asarray(rest, np.float32).astype(BF16).astype(np.float32))
        parts.append(float(part))
        rest = rest - part
    return parts


def _segment_rms_scale(acc, seg_ones_ref, seg):
    sq = (acc * acc).astype(BF16)
    parts = []
    for p in range(acc.shape[1] // MXU_DIM):
        parts.append(jnp.dot(sq[:, p * MXU_DIM:(p + 1) * MXU_DIM], seg_ones_ref[...],
                             preferred_element_type=F32))
    ss = jnp.concatenate(parts, axis=1)
    return lax.rsqrt(ss * (1.0 / seg) + EPS)


def _inproj_kernel(x_ref, g1_ref, w_ref, bias_ref, gv_ref, gq_ref, gk_ref, ones64_ref, ones128_ref,
                   u_ref, va_ref, q_ref, k_ref, vt_ref, ga_ref, gb_ref, *, seq_blocks):
    i = pl.program_id(0)
    tm = x_ref.shape[0]
    x = x_ref[...]
    r = lax.rsqrt(jnp.mean(x * x, axis=-1, keepdims=True) + EPS)
    h = ((x * r) * g1_ref[...]).astype(BF16)

    def slab(j):
        return jnp.dot(h, w_ref[:, j * D_MODEL:(j + 1) * D_MODEL], preferred_element_type=F32)

    lane = lax.broadcasted_iota(jnp.int32, (1, LANES), 1)
    lo_half = lane < B_HEAD_DIM

    u_ref[...] = slab(0).astype(BF16)

    acc = slab(1)
    r = _segment_rms_scale(acc, ones128_ref, LANES)
    va_ref[...] = ((acc * r) * gv_ref[...]).astype(BF16)

    spare_col = lane & (B_HEAD_DIM - 1)

    acc = slab(2)
    r = _segment_rms_scale(acc, ones64_ref, B_HEAD_DIM)
    qn = (acc * r) * gq_ref[...]
    for hd in range(B_HEADS):
        parts = _bf16_parts(ALIBI_SLOPES[hd] * LOG2_E, SLOPE_PARTS)
        cvec = jnp.zeros((1, LANES), F32)
        for n, part in enumerate(parts):
            cvec = jnp.where((spare_col == n) | (spare_col == n + SLOPE_PARTS), part, cvec)
        blk = qn[:, hd * LANES:(hd + 1) * LANES]
        q_ref[0, 2 * hd] = jnp.where(lo_half, blk, cvec).astype(BF16)
        q_ref[0, 2 * hd + 1] = jnp.where(lo_half, cvec, blk).astype(BF16)

    acc = slab(3)
    r = _segment_rms_scale(acc, ones64_ref, B_HEAD_DIM)
    kn = (acc * r) * gk_ref[...]
    pos = (i % seq_blocks) * tm + lax.broadcasted_iota(jnp.int32, (tm, LANES), 0)
    pos_lo = (pos & (MXU_DIM - 1)).astype(F32)
    pos_hi = (pos - (pos & (MXU_DIM - 1))).astype(F32)
    pvec = jnp.where(spare_col < SLOPE_PARTS, pos_lo,
                     jnp.where(spare_col < 2 * SLOPE_PARTS, pos_hi, 0.0))
    for hd in range(B_HEADS):
        blk = kn[:, hd * LANES:(hd + 1) * LANES]
        k_ref[0, 2 * hd] = jnp.where(lo_half, blk, pvec).astype(BF16)
        k_ref[0, 2 * hd + 1] = jnp.where(lo_half, pvec, blk).astype(BF16)

    acc = slab(4)
    for hd in range(B_HEADS):
        vt_ref[0, hd, 0:B_V_DIM, :] = acc[:, hd * LANES:(hd + 1) * LANES].T.astype(BF16)
        vt_ref[0, hd, B_V_DIM:VT_ROWS, :] = jnp.ones((VT_ROWS - B_V_DIM, tm), BF16)

    ga_ref[...] = (1.0 / (1.0 + jnp.exp(-(slab(5) + bias_ref[:, 0:D_MODEL])))).astype(BF16)
    gb_ref[...] = (1.0 / (1.0 + jnp.exp(-(slab(6) + bias_ref[:, D_MODEL:])))).astype(BF16)


def _input_projection(x2, g1, w_cat, b_gate, gv, gq, gk, ones64, ones128, bsz, seq):
    n_tok = x2.shape[0]
    tm = PROJ_TM
    seq_blocks = seq // tm
    row_spec = pl.BlockSpec((tm, D_MODEL), lambda i: (i, 0))
    resident = lambda shape: pl.BlockSpec(shape, lambda i: (0,) * len(shape),
                                          pipeline_mode=pl.Buffered(1))
    head_map = lambda i: (i // seq_blocks, 0, i % seq_blocks, 0)
    tok_bf16 = jax.ShapeDtypeStruct((n_tok, D_MODEL), BF16)
    return pl.pallas_call(
        functools.partial(_inproj_kernel, seq_blocks=seq_blocks),
        grid=(n_tok // tm,),
        in_specs=[
            row_spec,
            resident((1, D_MODEL)),
            resident(w_cat.shape),
            resident(b_gate.shape),
            resident((1, D_MODEL)), resident((1, D_MODEL)), resident((1, D_MODEL)),
            resident((MXU_DIM, MXU_DIM)), resident((MXU_DIM, MXU_DIM)),
        ],
        out_specs=[
            row_spec, row_spec,
            pl.BlockSpec((1, 2 * B_HEADS, tm, LANES), head_map),
            pl.BlockSpec((1, 2 * B_HEADS, tm, LANES), head_map),
            pl.BlockSpec((1, B_HEADS, VT_ROWS, tm), lambda i: (i // seq_blocks, 0, 0, i % seq_blocks)),
            row_spec, row_spec,
        ],
        out_shape=[
            tok_bf16, tok_bf16,
            jax.ShapeDtypeStruct((bsz, 2 * B_HEADS, seq, LANES), BF16),
            jax.ShapeDtypeStruct((bsz, 2 * B_HEADS, seq, LANES), BF16),
            jax.ShapeDtypeStruct((bsz, B_HEADS, VT_ROWS, seq), BF16),
            tok_bf16, tok_bf16,
        ],
        compiler_params=pltpu.CompilerParams(
            dimension_semantics=("arbitrary",),
            vmem_limit_bytes=VMEM_LIMIT_BYTES),
        name="input_projection",
    )(x2, g1, w_cat, b_gate, gv, gq, gk, ones64, ones128)


def _attn_kernel(lam_ref, sg_ref, q_ref, k_ref, vt_ref, o_ref, acc_scr, sa_scr, sb_scr):
    qi = pl.program_id(2)
    tq = q_ref.shape[2]
    tk = ATT_TK
    nt_dims = (((1,), (1,)), ((), ()))

    assert tq == 2 * tk
    tri = (lax.broadcasted_iota(jnp.int32, (tk, tk), 0)
           <= lax.broadcasted_iota(jnp.int32, (tk, tk), 1))

    def scores(kb, s_scr, diag=None):
        ks = pl.multiple_of(kb * tk, tk)
        maxima = []
        for c in range(2):
            k_blk = k_ref[0, c, pl.ds(ks, tk), :]
            if diag is None:
                s = lax.dot_general(k_blk, q_ref[0, c], nt_dims, preferred_element_type=F32)
                s_scr[c] = s
                maxima.append(jnp.max(s, axis=0, keepdims=True))
                continue
            parts = []
            for half in range(diag, 2):
                cols = slice(half * tk, (half + 1) * tk)
                s = lax.dot_general(k_blk, q_ref[0, c, cols, :], nt_dims,
                                    preferred_element_type=F32)
                if half == diag:
                    s = jnp.where(tri, s, NEG_BIG)
                s_scr[c, :, cols] = s
                parts.append(jnp.max(s, axis=0, keepdims=True))
            maxima.append(parts[0] if len(parts) == 1 else jnp.concatenate(parts, axis=1))
        return tuple(maxima)

    def consume(kb, s_scr, m_old, blk_max, col0=0, first=False):
        ks = pl.multiple_of(kb * tk, tk)
        vt = vt_ref[0, 0, :, pl.ds(ks, tk)]
        m_new = []
        for c in range(2):
            m_c = blk_max[c] if first else jnp.maximum(m_old[c][:, col0:], blk_max[c])
            p = jnp.exp2(s_scr[c, :, col0:] - m_c).astype(BF16)
            pv = jnp.dot(vt, p, preferred_element_type=F32)
            cols = slice(c * tq + col0, (c + 1) * tq)
            if first:
                acc_scr[:, cols] = pv
            else:
                acc_scr[:, cols] = acc_scr[:, cols] * jnp.exp2(m_old[c][:, col0:] - m_c) + pv
            m_new.append(m_c)
        return tuple(m_new)

    max_a = scores(2 * qi, sa_scr, diag=0)
    max_b = scores(2 * qi + 1, sb_scr, diag=1)
    m = consume(2 * qi, sa_scr, None, max_a, first=True)
    max_a = scores(0, sa_scr)
    m_right = consume(2 * qi + 1, sb_scr, m, max_b, col0=tk)
    m = tuple(jnp.concatenate([m[c][:, :tk], m_right[c]], axis=1) for c in range(2))

    def trip(t, carry):
        m, max_a = carry
        max_b = scores(2 * t + 1, sb_scr)
        m = consume(2 * t, sa_scr, m, max_a)
        max_a = scores(2 * t + 2, sa_scr)
        m = consume(2 * t + 1, sb_scr, m, max_b)
        return m, max_a

    m, max_a = lax.fori_loop(0, qi - 1, trip, (m, max_a))

    @pl.when(qi > 0)
    def _():
        max_b = scores(2 * qi - 1, sb_scr)
        m_mid = consume(2 * qi - 2, sa_scr, m, max_a)
        consume(2 * qi - 1, sb_scr, m_mid, max_b)

    lam_v = lam_ref[...]
    lam = (jnp.exp(jnp.sum(lam_v[0:1] * lam_v[1:2], axis=1, keepdims=True))
           - jnp.exp(jnp.sum(lam_v[2:3] * lam_v[3:4], axis=1, keepdims=True)) + LAM_INIT)
    o0 = acc_scr[0:B_V_DIM, 0:tq] / acc_scr[B_V_DIM:B_V_DIM + 1, 0:tq]
    o1 = acc_scr[0:B_V_DIM, tq:2 * tq] / acc_scr[B_V_DIM:B_V_DIM + 1, tq:2 * tq]
    o = o0 - lam * o1
    r = lax.rsqrt(jnp.mean(o * o, axis=0, keepdims=True) + EPS)
    o = ((o * r) * sg_ref[...]) * (1.0 - LAM_INIT)
    o_ref[0] = o.T.astype(BF16)


def _diff_attention(lam_vecs, sub_g, q_arr, k_arr, vt_arr):
    bsz, _, seq, _ = q_arr.shape
    tq = ATT_TQ
    return pl.pallas_call(
        _attn_kernel,
        grid=(bsz, B_HEADS, seq // tq),
        in_specs=[
            pl.BlockSpec((4, B_HEAD_DIM), lambda b, h, qi: (0, 0)),
            pl.BlockSpec((B_V_DIM, 1), lambda b, h, qi: (0, 0)),
            pl.BlockSpec((1, 2, tq, LANES), lambda b, h, qi: (b, h, qi, 0)),
            pl.BlockSpec((1, 2, seq, LANES), lambda b, h, qi: (b, h, 0, 0)),
            pl.BlockSpec((1, 1, VT_ROWS, seq), lambda b, h, qi: (b, h, 0, 0)),
        ],
        out_specs=pl.BlockSpec((1, tq, B_V_DIM), lambda b, h, qi: (b, qi, h)),
        out_shape=jax.ShapeDtypeStruct((bsz, seq, B_HEADS * B_V_DIM), BF16),
        scratch_shapes=[pltpu.VMEM((VT_ROWS, 2 * tq), F32),
                        pltpu.VMEM((2, ATT_TK, tq), F32),
                        pltpu.VMEM((2, ATT_TK, tq), F32)],
        compiler_params=pltpu.CompilerParams(
            dimension_semantics=("arbitrary", "arbitrary", "arbitrary"),
            vmem_limit_bytes=VMEM_LIMIT_BYTES),
        name="diff_attention",
    )(lam_vecs, sub_g, q_arr, k_arr, vt_arr)


def _first_argmax(vals, iota, top):
    return jnp.min(jnp.where(vals == top, iota, vals.shape[0]), axis=0, keepdims=True)


def _merge_kernel(u_ref, va_ref, yb_ref, ga_ref, gb_ref, x_ref, ws_ref, bs_ref,
                  wua_ref, wub_ref, wo_ref, g2_ref, wr_hi_ref, wr_lo_ref, br_ref,
                  x1_ref, h2_ref, comb_ref, gidx_ref):
    tm = x_ref.shape[0]
    row = lax.broadcasted_iota(jnp.int32, (CHUNK, CHUNK), 0)
    col = lax.broadcasted_iota(jnp.int32, (CHUNK, CHUNK), 1)
    causal = row >= col

    n_chunks = tm // CHUNK
    group_cols = []
    for g in range(A_GROUPS):
        w_g = jnp.where(causal, ws_ref[g], 0.0).astype(BF16)
        gl = slice(g * LANES, (g + 1) * LANES)
        v_g = jnp.concatenate([va_ref[c * CHUNK:(c + 1) * CHUNK, gl] for c in range(n_chunks)],
                              axis=1)
        z = jnp.dot(w_g, v_g, preferred_element_type=F32) + bs_ref[g]
        z = jnp.concatenate([z[:, c * LANES:(c + 1) * LANES] for c in range(n_chunks)], axis=0)
        group_cols.append((u_ref[:, gl].astype(F32) * z).astype(BF16))
    y_a = jnp.concatenate(group_cols, axis=1)

    up_a = jnp.dot(y_a, wua_ref[...], preferred_element_type=F32)
    up_b = jnp.dot(yb_ref[...], wub_ref[...], preferred_element_type=F32)
    merged = ga_ref[...].astype(F32) * up_a + gb_ref[...].astype(F32) * up_b
    x1 = x_ref[...] + jnp.dot(merged.astype(BF16), wo_ref[...], preferred_element_type=F32)
    x1_ref[...] = x1

    r = lax.rsqrt(jnp.mean(x1 * x1, axis=-1, keepdims=True) + EPS)
    h2 = (x1 * r) * g2_ref[...]
    h2_hi = h2.astype(BF16)
    h2_ref[...] = h2_hi
    h2_lo = (h2 - h2_hi.astype(F32)).astype(BF16)

    nt_dims = (((1,), (1,)), ((), ()))
    lt = (lax.dot_general(wr_hi_ref[...], h2_hi, nt_dims, preferred_element_type=F32)
          + lax.dot_general(wr_hi_ref[...], h2_lo, nt_dims, preferred_element_type=F32)
          + lax.dot_general(wr_lo_ref[...], h2_hi, nt_dims, preferred_element_type=F32)
          + br_ref[...])

    iota8 = lax.broadcasted_iota(jnp.int32, (8, tm), 0)
    gl = lt[0:8]
    ge = jnp.exp(gl - jnp.max(gl, axis=0, keepdims=True))
    gp = ge / jnp.sum(ge, axis=0, keepdims=True)
    g_gate = jnp.max(gp, axis=0, keepdims=True)
    g_idx = _first_argmax(gp, iota8, g_gate)

    el = jnp.zeros((EXPERTS_PER_GROUP, tm), F32)
    for gi in range(N_GROUPS):
        el = jnp.where(g_idx == gi, lt[8 + 8 * gi:16 + 8 * gi], el)
    ee = jnp.exp(el - jnp.max(el, axis=0, keepdims=True))
    ep = ee / jnp.sum(ee, axis=0, keepdims=True)
    p_top1 = jnp.max(ep, axis=0, keepdims=True)
    i_top1 = _first_argmax(ep, iota8, p_top1)
    ep_rest = jnp.where(iota8 == i_top1, -1.0, ep)
    p_top2 = jnp.max(ep_rest, axis=0, keepdims=True)
    i_top2 = _first_argmax(ep_rest, iota8, p_top2)
    den = p_top1 + p_top2
    w_top1 = g_gate * (p_top1 / den)
    w_top2 = g_gate * (p_top2 / den)
    in_group = jnp.where(iota8 == i_top1, w_top1, 0.0) + jnp.where(iota8 == i_top2, w_top2, 0.0)
    blocks = [jnp.where(g_idx == gi, in_group, 0.0) for gi in range(N_GROUPS)]
    blocks.append(jnp.zeros((LANES - N_EXPERTS, tm), F32))
    comb_t = jnp.concatenate(blocks, axis=0)
    comb_ref[...] = comb_t.T
    gidx_ref[...] = jnp.broadcast_to(g_idx, gidx_ref.shape)


def _merge_and_route(u, va, yb, ga, gb, x2, ws, bs, wua, wub, wo, g2, wr_hi, wr_lo, br):
    n_tok = x2.shape[0]
    tm = MERGE_TM
    row_spec = pl.BlockSpec((tm, D_MODEL), lambda i: (i, 0))
    full = lambda shape: pl.BlockSpec(shape, lambda i: (0,) * len(shape))
    return pl.pallas_call(
        _merge_kernel,
        grid=(n_tok // tm,),
        in_specs=[
            row_spec, row_spec, row_spec, row_spec, row_spec, row_spec,
            full((A_GROUPS, CHUNK, CHUNK)), full((A_GROUPS, CHUNK, 1)),
            full((D_MODEL, D_MODEL)), full((D_MODEL, D_MODEL)), full((D_MODEL, D_MODEL)),
            full((1, D_MODEL)),
            full((ROUTER_ROWS, D_MODEL)), full((ROUTER_ROWS, D_MODEL)), full((ROUTER_ROWS, 1)),
        ],
        out_specs=[row_spec, row_spec, pl.BlockSpec((tm, LANES), lambda i: (i, 0)),
                   pl.BlockSpec((8, tm), lambda i: (0, i))],
        out_shape=[
            jax.ShapeDtypeStruct((n_tok, D_MODEL), F32),
            jax.ShapeDtypeStruct((n_tok, D_MODEL), BF16),
            jax.ShapeDtypeStruct((n_tok, LANES), F32),
            jax.ShapeDtypeStruct((8, n_tok), jnp.int32),
        ],
        compiler_params=pltpu.CompilerParams(
            dimension_semantics=("arbitrary",),
            vmem_limit_bytes=VMEM_LIMIT_BYTES),
        name="merge_and_route",
    )(u, va, yb, ga, gb, x2, ws, bs, wua, wub, wo, g2, wr_hi, wr_lo, br)


def _moe_kernel(g_ref, h_ref, c_ref, x1_ref, w13_ref, w2_ref, o_ref,
                p_scr, xg_scr, cg_scr, csel_scr, y_scr, bounds_ref):
    e = pl.program_id(1)
    n_steps = pl.num_programs(1)
    tt = h_ref.shape[0]
    rows = p_scr.shape[0]
    rb = MOE_ROW_BLOCK

    @pl.when(e == 0)
    def _():
        g_idx = g_ref[0:1, :]
        slot = lax.broadcasted_iota(jnp.int32, (16, tt), 0)
        onehot = jnp.where(slot == g_idx, 1.0, 0.0)
        upper = (lax.broadcasted_iota(jnp.int32, (LANES, LANES), 0)
                 <= lax.broadcasted_iota(jnp.int32, (LANES, LANES), 1)).astype(BF16)
        carry = jnp.zeros((16, 1), F32)
        counts = []
        for j in range(tt // LANES):
            seg = onehot[:, j * LANES:(j + 1) * LANES].astype(BF16)
            cs = jnp.dot(seg, upper, preferred_element_type=F32) + carry
            counts.append(cs)
            carry = cs[:, LANES - 1:LANES]
        running = jnp.concatenate(counts, axis=1)
        n_blocks = jnp.floor((carry + (rb - 1)) * (1.0 / rb))
        slot_col = lax.broadcasted_iota(jnp.int32, (16, 1), 0)
        first_block = jnp.zeros((16, 1), F32)
        for g in range(N_GROUPS - 1):
            first_block = first_block + jnp.where(slot_col > g, n_blocks[g:g + 1, :], 0.0)
        end_block = first_block + n_blocks
        for g in range(N_GROUPS):
            bounds_ref[g] = first_block[g:g + 1, :][0, 0].astype(jnp.int32)
            bounds_ref[N_GROUPS + g] = end_block[g:g + 1, :][0, 0].astype(jnp.int32)
        pos = jnp.sum(onehot * (first_block * rb + running - 1.0), axis=0, keepdims=True)
        pos = pos.astype(jnp.int32)

        c = c_ref[...]
        c_hi = c.astype(BF16)
        c_rest = c - c_hi.astype(F32)
        c_mid = c_rest.astype(BF16)
        c_lo = (c_rest - c_mid.astype(F32)).astype(BF16)
        h_ext = jnp.concatenate([h_ref[...], c_hi, c_mid, c_lo], axis=1)
        for ch in range(rows // MXU_DIM):
            sl = slice(ch * MXU_DIM, (ch + 1) * MXU_DIM)
            row_id = lax.broadcasted_iota(jnp.int32, (MXU_DIM, tt), 0) + ch * MXU_DIM
            perm = jnp.where(row_id == pos, 1.0, 0.0).astype(BF16)
            p_scr[sl, :] = perm
            got = jnp.dot(perm, h_ext, preferred_element_type=F32)
            xg_scr[sl, :] = got[:, :D_MODEL].astype(BF16)
            cg_scr[sl, :] = (got[:, D_MODEL:D_MODEL + LANES]
                             + got[:, D_MODEL + LANES:D_MODEL + 2 * LANES]
                             + got[:, D_MODEL + 2 * LANES:])
        y_scr[...] = jnp.zeros_like(y_scr)

    for eb in range(N_EXPERTS // MOE_NE):
        @pl.when(e == eb)
        def _():
            csel_scr[...] = cg_scr[:, eb * MOE_NE:(eb + 1) * MOE_NE]

    group = e // (EXPERTS_PER_GROUP // MOE_NE)

    def row_block(b, carry):
        r0 = pl.multiple_of(b * rb, rb)
        xb = xg_scr[pl.ds(r0, rb), :]
        hids = []
        for n in range(MOE_NE):
            ab = jnp.dot(xb, w13_ref[n], preferred_element_type=F32)
            a = ab[:, :D_EXPERT]
            b3 = ab[:, D_EXPERT:]
            hid = (a * (1.0 / (1.0 + jnp.exp(-a)))) * b3
            hids.append((hid * csel_scr[pl.ds(r0, rb), n:n + 1]).astype(BF16))
        hid_all = jnp.concatenate(hids, axis=1)
        y_scr[pl.ds(r0, rb), :] += jnp.dot(hid_all, w2_ref[...], preferred_element_type=F32)
        return carry

    lax.fori_loop(bounds_ref[group], bounds_ref[N_GROUPS + group], row_block, 0)

    @pl.when(e == n_steps - 1)
    def _():
        tn_dims = (((0,), (0,)), ((), ()))
        y = lax.dot_general(p_scr[...], y_scr[...].astype(BF16), tn_dims,
                            preferred_element_type=F32)
        o_ref[...] = x1_ref[...] + y


def _moe(gidx, h2, comb, x1, w13, w2f):
    n_tok = h2.shape[0]
    tt = MOE_TM
    rows = tt + N_GROUPS * MOE_ROW_BLOCK
    return pl.pallas_call(
        _moe_kernel,
        grid=(n_tok // tt, N_EXPERTS // MOE_NE),
        in_specs=[
            pl.BlockSpec((8, tt), lambda i, e: (0, i)),
            pl.BlockSpec((tt, D_MODEL), lambda i, e: (i, 0)),
            pl.BlockSpec((tt, LANES), lambda i, e: (i, 0)),
            pl.BlockSpec((tt, D_MODEL), lambda i, e: (i, 0)),
            pl.BlockSpec((MOE_NE, D_MODEL, 2 * D_EXPERT), lambda i, e: (e, 0, 0)),
            pl.BlockSpec((MOE_NE * D_EXPERT, D_MODEL), lambda i, e: (e, 0)),
        ],
        out_specs=pl.BlockSpec((tt, D_MODEL), lambda i, e: (i, 0)),
        out_shape=jax.ShapeDtypeStruct((n_tok, D_MODEL), F32),
        scratch_shapes=[
            pltpu.VMEM((rows, tt), BF16),
            pltpu.VMEM((rows, D_MODEL), BF16),
            pltpu.VMEM((rows, LANES), F32),
            pltpu.VMEM((rows, MOE_NE), F32),
            pltpu.VMEM((rows, D_MODEL), F32),
            pltpu.SMEM((2 * N_GROUPS,), jnp.int32),
        ],
        compiler_params=pltpu.CompilerParams(
            dimension_semantics=("arbitrary", "arbitrary"),
            vmem_limit_bytes=VMEM_LIMIT_BYTES),
        name="moe_experts",
    )(gidx, h2, comb, x1, w13, w2f)


def _segment_ones(seg):
    idx = jnp.arange(MXU_DIM) // seg
    return (idx[:, None] == idx[None, :]).astype(BF16)


def kernel(x, norm1_g, w_in, v_norm_g, w_s, b_s, q_norm_g, k_norm_g, lambda_q1, lambda_k1, lambda_q2, lambda_k2, sub_norm_g, w_up_a, w_up_b, w_gate, b_gate, w_out, norm2_g, w_rg, b_rg, w_re, b_re, w1, w3, w2):
    bsz, seq, d = x.shape
    assert d == D_MODEL and seq % PROJ_TM == 0 and seq % ATT_TQ == 0 and ATT_TQ == 2 * ATT_TK
    assert norm1_g.shape[0] == 1, "single layer"
    n_tok = bsz * seq
    assert n_tok % MOE_TM == 0 and n_tok % MERGE_TM == 0
    x2 = x.reshape(n_tok, d)

    w_cat = jnp.concatenate([w_in[0], w_gate[0]], axis=1).astype(BF16)
    gv = v_norm_g[0].reshape(1, D_MODEL)
    gq = jnp.tile(q_norm_g[0] * (B_HEAD_DIM ** -0.5 * LOG2_E), 2 * B_HEADS)[None, :]
    gk = jnp.tile(k_norm_g[0], 2 * B_HEADS)[None, :]
    u, va, q_arr, k_arr, vt_arr, ga, gb = _input_projection(
        x2, norm1_g, w_cat, b_gate, gv, gq, gk, _segment_ones(B_HEAD_DIM), _segment_ones(LANES),
        bsz, seq)

    lam_vecs = jnp.concatenate([lambda_q1, lambda_k1, lambda_q2, lambda_k2], axis=0)
    yb = _diff_attention(lam_vecs, sub_norm_g[0][:, None], q_arr, k_arr, vt_arr)
    yb = yb.reshape(n_tok, D_MODEL)

    pad_rows = ROUTER_ROWS - 8 - N_EXPERTS
    wr = jnp.concatenate([w_rg[0].T, jnp.zeros((4, d), F32), w_re[0].T,
                          jnp.zeros((pad_rows, d), F32)], axis=0)
    br = jnp.concatenate([b_rg[0], jnp.full((4,), NEG_BIG, F32), b_re[0],
                          jnp.zeros((pad_rows,), F32)])[:, None]
    wr_hi = wr.astype(BF16)
    wr_lo = (wr - wr_hi.astype(F32)).astype(BF16)
    x1, h2, comb, gidx = _merge_and_route(
        u, va, yb, ga, gb, x2, w_s[0], b_s[0][:, :, None],
        w_up_a[0].astype(BF16), w_up_b[0].astype(BF16), w_out[0].astype(BF16),
        norm2_g, wr_hi, wr_lo, br)

    w13 = jnp.concatenate([w1[0], w3[0]], axis=2).astype(BF16)
    w2f = w2[0].astype(BF16).reshape(N_EXPERTS * D_EXPERT, D_MODEL)
    out = _moe(gidx, h2, comb, x1, w13, w2f)
    return out.reshape(bsz, seq, d)
```

```python
import functools
import math

import jax
import jax.numpy as jnp
import numpy as np
from jax import lax
from jax.experimental import pallas as pl
from jax.experimental.pallas import tpu as pltpu

F32 = jnp.float32
BF16 = jnp.bfloat16

D_MODEL = 1024
EPS = 1e-6
A_GROUPS = 8
CHUNK = 128
B_HEADS = 8
B_HEAD_DIM = 64
B_V_DIM = 128
IN_COLS = 5 * D_MODEL
N_GROUPS = 4
EXPERTS_PER_GROUP = 8
N_EXPERTS = 32
D_EXPERT = 256
LAM_INIT = 0.8 - 0.6 * math.exp(-0.3 * 0)

LANES = 128
MXU_DIM = 256
VMEM_LIMIT_BYTES = 56 * 1024 * 1024

NEG_BIG = -1e30
LOG2_E = math.log2(math.e)
ALIBI_SLOPES = tuple(2.0 ** (-8.0 * (h + 1) / B_HEADS) for h in range(B_HEADS))
SLOPE_PARTS = 4
VT_ROWS = B_V_DIM + 16
SCORE_BOUND_MARGIN = 1.02
MAX_BOUNDED_SCORE = 48.0

PROJ_TM = 512
ATT_TQ = 1024
ATT_TK = 512
MERGE_TM = 512
MOE_TM = 1024
MOE_NE = 4
MOE_ROW_BLOCK = 128
ROUTER_ROWS = 48


def _bf16_parts(value, n_parts):
    parts, rest = [], np.float64(value)
    for _ in range(n_parts):
        part = np.float64(np.asarray(rest, np.float32).astype(BF16).astype(np.float32))
        parts.append(float(part))
        rest = rest - part
    return parts


def _segment_rms_scale(acc, seg_ones_ref, seg):
    sq = (acc * acc).astype(BF16)
    parts = []
    for p in range(acc.shape[1] // MXU_DIM):
        parts.append(jnp.dot(sq[:, p * MXU_DIM:(p + 1) * MXU_DIM], seg_ones_ref[...],
                             preferred_element_type=F32))
    ss = jnp.concatenate(parts, axis=1)
    return lax.rsqrt(ss * (1.0 / seg) + EPS)


def _inproj_kernel(x_ref, g1_ref, w_ref, bias_ref, gv_ref, gq_ref, gk_ref, ones64_ref, ones128_ref,
                   bound_ref,
                   u_ref, va_ref, q_ref, k_ref, vt_ref, ga_ref, gb_ref, *, seq_blocks):
    i = pl.program_id(0)
    tm = x_ref.shape[0]
    x = x_ref[...]
    r = lax.rsqrt(jnp.mean(x * x, axis=-1, keepdims=True) + EPS)
    h = ((x * r) * g1_ref[...]).astype(BF16)

    def slab(j):
        return jnp.dot(h, w_ref[:, j * D_MODEL:(j + 1) * D_MODEL], preferred_element_type=F32)

    lane = lax.broadcasted_iota(jnp.int32, (1, LANES), 1)
    lo_half = lane < B_HEAD_DIM

    u_ref[...] = slab(0).astype(BF16)

    acc = slab(1)
    r = _segment_rms_scale(acc, ones128_ref, LANES)
    va_ref[...] = ((acc * r) * gv_ref[...]).astype(BF16)

    spare_col = lane & (B_HEAD_DIM - 1)
    pos = (i % seq_blocks) * tm + lax.broadcasted_iota(jnp.int32, (tm, LANES), 0)
    pos_lo = (pos & (MXU_DIM - 1)).astype(F32)
    pos_hi = (pos - (pos & (MXU_DIM - 1))).astype(F32)
    pos_cols = jnp.where(spare_col & SLOPE_PARTS == 0, pos_lo, pos_hi)
    bound = bound_ref[...]
    bound_hi = bound.astype(BF16).astype(F32)
    bound_mid = (bound - bound_hi).astype(BF16).astype(F32)
    bound_lo = (bound - bound_hi - bound_mid).astype(BF16).astype(F32)
    bound_cols = jnp.where(spare_col == 4 * SLOPE_PARTS, bound_hi,
                           jnp.where(spare_col == 4 * SLOPE_PARTS + 1, bound_mid,
                                     jnp.where(spare_col == 4 * SLOPE_PARTS + 2, bound_lo, 0.0)))
    q_shared = jnp.where((spare_col >= 2 * SLOPE_PARTS) & (spare_col < 4 * SLOPE_PARTS),
                         pos_cols, bound_cols)
    k_shared = jnp.where(spare_col < 2 * SLOPE_PARTS, pos_cols, 0.0)

    def slope_cols(hd, sign):
        parts = _bf16_parts(ALIBI_SLOPES[hd] * LOG2_E, SLOPE_PARTS)
        cvec = jnp.zeros((1, LANES), F32)
        for n, part in enumerate(parts):
            cvec = jnp.where(spare_col & (SLOPE_PARTS - 1) == n, sign * part, cvec)
        return cvec

    acc = slab(2)
    r = _segment_rms_scale(acc, ones64_ref, B_HEAD_DIM)
    qn = (acc * r) * gq_ref[...]
    for hd in range(B_HEADS):
        extra = jnp.where(spare_col < 2 * SLOPE_PARTS, slope_cols(hd, 1.0), q_shared)
        blk = qn[:, hd * LANES:(hd + 1) * LANES]
        q_ref[0, 2 * hd] = jnp.where(lo_half, blk, extra).astype(BF16)
        q_ref[0, 2 * hd + 1] = jnp.where(lo_half, extra, blk).astype(BF16)

    acc = slab(3)
    r = _segment_rms_scale(acc, ones64_ref, B_HEAD_DIM)
    kn = (acc * r) * gk_ref[...]
    for hd in range(B_HEADS):
        consts = jnp.where(spare_col < 4 * SLOPE_PARTS, slope_cols(hd, -1.0),
                           jnp.where(spare_col < 4 * SLOPE_PARTS + 3, -1.0, 0.0))
        extra = jnp.where(spare_col < 2 * SLOPE_PARTS, k_shared, consts)
        blk = kn[:, hd * LANES:(hd + 1) * LANES]
        k_ref[0, 2 * hd] = jnp.where(lo_half, blk, extra).astype(BF16)
        k_ref[0, 2 * hd + 1] = jnp.where(lo_half, extra, blk).astype(BF16)

    acc = slab(4)
    for hd in range(B_HEADS):
        vt_ref[0, hd, 0:B_V_DIM, :] = acc[:, hd * LANES:(hd + 1) * LANES].T.astype(BF16)
        vt_ref[0, hd, B_V_DIM:VT_ROWS, :] = jnp.ones((VT_ROWS - B_V_DIM, tm), BF16)

    ga_ref[...] = (1.0 / (1.0 + jnp.exp(-(slab(5) + bias_ref[:, 0:D_MODEL])))).astype(BF16)
    gb_ref[...] = (1.0 / (1.0 + jnp.exp(-(slab(6) + bias_ref[:, D_MODEL:])))).astype(BF16)


def _input_projection(x2, g1, w_cat, b_gate, gv, gq, gk, ones64, ones128, bound, bsz, seq):
    n_tok = x2.shape[0]
    tm = PROJ_TM
    seq_blocks = seq // tm
    row_spec = pl.BlockSpec((tm, D_MODEL), lambda i: (i, 0))
    resident = lambda shape: pl.BlockSpec(shape, lambda i: (0,) * len(shape),
                                          pipeline_mode=pl.Buffered(1))
    head_map = lambda i: (i // seq_blocks, 0, i % seq_blocks, 0)
    tok_bf16 = jax.ShapeDtypeStruct((n_tok, D_MODEL), BF16)
    return pl.pallas_call(
        functools.partial(_inproj_kernel, seq_blocks=seq_blocks),
        grid=(n_tok // tm,),
        in_specs=[
            row_spec,
            resident((1, D_MODEL)),
            resident(w_cat.shape),
            resident(b_gate.shape),
            resident((1, D_MODEL)), resident((1, D_MODEL)), resident((1, D_MODEL)),
            resident((MXU_DIM, MXU_DIM)), resident((MXU_DIM, MXU_DIM)),
            resident((1, 1)),
        ],
        out_specs=[
            row_spec, row_spec,
            pl.BlockSpec((1, 2 * B_HEADS, tm, LANES), head_map),
            pl.BlockSpec((1, 2 * B_HEADS, tm, LANES), head_map),
            pl.BlockSpec((1, B_HEADS, VT_ROWS, tm), lambda i: (i // seq_blocks, 0, 0, i % seq_blocks)),
            row_spec, row_spec,
        ],
        out_shape=[
            tok_bf16, tok_bf16,
            jax.ShapeDtypeStruct((bsz, 2 * B_HEADS, seq, LANES), BF16),
            jax.ShapeDtypeStruct((bsz, 2 * B_HEADS, seq, LANES), BF16),
            jax.ShapeDtypeStruct((bsz, B_HEADS, VT_ROWS, seq), BF16),
            tok_bf16, tok_bf16,
        ],
        compiler_params=pltpu.CompilerParams(
            dimension_semantics=("arbitrary",),
            vmem_limit_bytes=VMEM_LIMIT_BYTES),
        name="input_projection",
    )(x2, g1, w_cat, b_gate, gv, gq, gk, ones64, ones128, bound)


def _attn_kernel(lam_ref, sg_ref, q_ref, k_ref, vt_ref, o_ref, acc_scr, sa_scr, sb_scr):
    qi = pl.program_id(2)
    tq = q_ref.shape[2]
    tk = ATT_TK
    nt_dims = (((1,), (1,)), ((), ()))

    assert tq == 2 * tk
    tri = (lax.broadcasted_iota(jnp.int32, (tk, tk), 0)
           <= lax.broadcasted_iota(jnp.int32, (tk, tk), 1))

    def scores(kb, s_scr, diag=None):
        ks = pl.multiple_of(kb * tk, tk)
        maxima = []
        for c in range(2):
            k_blk = k_ref[0, c, pl.ds(ks, tk), :]
            if diag is None:
                s = lax.dot_general(k_blk, q_ref[0, c], nt_dims, preferred_element_type=F32)
                s_scr[c] = s
                maxima.append(jnp.max(s, axis=0, keepdims=True))
                continue
            parts = []
            for half in range(diag, 2):
                cols = slice(half * tk, (half + 1) * tk)
                s = lax.dot_general(k_blk, q_ref[0, c, cols, :], nt_dims,
                                    preferred_element_type=F32)
                if half == diag:
                    s = jnp.where(tri, s, NEG_BIG)
                s_scr[c, :, cols] = s
                parts.append(jnp.max(s, axis=0, keepdims=True))
            maxima.append(parts[0] if len(parts) == 1 else jnp.concatenate(parts, axis=1))
        return tuple(maxima)

    def consume(kb, s_scr, m_old, blk_max, col0=0, first=False):
        ks = pl.multiple_of(kb * tk, tk)
        vt = vt_ref[0, 0, :, pl.ds(ks, tk)]
        m_new = []
        for c in range(2):
            m_c = blk_max[c] if first else jnp.maximum(m_old[c][:, col0:], blk_max[c])
            p = jnp.exp2(s_scr[c, :, col0:] - m_c).astype(BF16)
            pv = jnp.dot(vt, p, preferred_element_type=F32)
            cols = slice(c * tq + col0, (c + 1) * tq)
            if first:
                acc_scr[:, cols] = pv
            else:
                acc_scr[:, cols] = acc_scr[:, cols] * jnp.exp2(m_old[c][:, col0:] - m_c) + pv
            m_new.append(m_c)
        return tuple(m_new)

    max_a = scores(2 * qi, sa_scr, diag=0)
    max_b = scores(2 * qi + 1, sb_scr, diag=1)
    m = consume(2 * qi, sa_scr, None, max_a, first=True)
    max_a = scores(0, sa_scr)
    m_right = consume(2 * qi + 1, sb_scr, m, max_b, col0=tk)
    m = tuple(jnp.concatenate([m[c][:, :tk], m_right[c]], axis=1) for c in range(2))

    def trip(t, carry):
        m, max_a = carry
        max_b = scores(2 * t + 1, sb_scr)
        m = consume(2 * t, sa_scr, m, max_a)
        max_a = scores(2 * t + 2, sa_scr)
        m = consume(2 * t + 1, sb_scr, m, max_b)
        return m, max_a

    m, max_a = lax.fori_loop(0, qi - 1, trip, (m, max_a))

    @pl.when(qi > 0)
    def _():
        max_b = scores(2 * qi - 1, sb_scr)
        m_mid = consume(2 * qi - 2, sa_scr, m, max_a)
        consume(2 * qi - 1, sb_scr, m_mid, max_b)

    _attn_finalize(lam_ref, sg_ref, acc_scr, o_ref)


def _attn_finalize(lam_ref, sg_ref, acc_scr, o_ref):
    tq = o_ref.shape[1]
    lam_v = lam_ref[...]
    lam = (jnp.exp(jnp.sum(lam_v[0:1] * lam_v[1:2], axis=1, keepdims=True))
           - jnp.exp(jnp.sum(lam_v[2:3] * lam_v[3:4], axis=1, keepdims=True)) + LAM_INIT)
    o0 = acc_scr[0:B_V_DIM, 0:tq] / acc_scr[B_V_DIM:B_V_DIM + 1, 0:tq]
    o1 = acc_scr[0:B_V_DIM, tq:2 * tq] / acc_scr[B_V_DIM:B_V_DIM + 1, tq:2 * tq]
    o = o0 - lam * o1
    r = lax.rsqrt(jnp.mean(o * o, axis=0, keepdims=True) + EPS)
    o = ((o * r) * sg_ref[...]) * (1.0 - LAM_INIT)
    o_ref[0] = o.T.astype(BF16)


def _attn_bounded_kernel(lam_ref, sg_ref, q_ref, k_ref, vt_ref, o_ref, acc_scr):
    qi = pl.program_id(2)
    tq = q_ref.shape[2]
    tk = ATT_TK
    nt_dims = (((1,), (1,)), ((), ()))
    assert tq == 2 * tk
    tri = (lax.broadcasted_iota(jnp.int32, (tk, tk), 0)
           <= lax.broadcasted_iota(jnp.int32, (tk, tk), 1))

    def fold(key_start, n_keys, q_cols, masked, first):
        vt = vt_ref[0, 0, 0:B_V_DIM, pl.ds(key_start, n_keys)]
        for c in range(2):
            e = lax.dot_general(k_ref[0, c, pl.ds(key_start, n_keys), :], q_ref[0, c, q_cols, :],
                                nt_dims, preferred_element_type=F32)
            if masked:
                e = jnp.where(tri, e, NEG_BIG)
            p = jnp.exp2(e)
            p_sum = jnp.sum(p, axis=0, keepdims=True)
            pv = jnp.dot(vt, p.astype(BF16), preferred_element_type=F32)
            cols = slice(c * tq + q_cols.start, c * tq + q_cols.stop)
            if first:
                acc_scr[0:B_V_DIM, cols] = pv
                acc_scr[B_V_DIM:B_V_DIM + 1, cols] = p_sum
            else:
                acc_scr[0:B_V_DIM, cols] += pv
                acc_scr[B_V_DIM:B_V_DIM + 1, cols] += p_sum

    left, right, full = slice(0, tk), slice(tk, tq), slice(0, tq)
    diag0 = pl.multiple_of(qi * tq, tq)
    fold(diag0, tk, left, True, True)
    fold(diag0, tk, right, False, True)
    fold(diag0 + tk, tk, right, True, False)

    def trip(t, carry):
        fold(pl.multiple_of(t * tq, tq), tq, full, False, False)
        return carry

    lax.fori_loop(0, qi, trip, 0)
    _attn_finalize(lam_ref, sg_ref, acc_scr, o_ref)


def _diff_attention(lam_vecs, sub_g, q_arr, k_arr, vt_arr, bounded):
    bsz, _, seq, _ = q_arr.shape
    tq = ATT_TQ
    scratch = [pltpu.VMEM((VT_ROWS, 2 * tq), F32)]
    if not bounded:
        scratch += [pltpu.VMEM((2, ATT_TK, tq), F32), pltpu.VMEM((2, ATT_TK, tq), F32)]
    return pl.pallas_call(
        _attn_bounded_kernel if bounded else _attn_kernel,
        grid=(bsz, B_HEADS, seq // tq),
        in_specs=[
            pl.BlockSpec((4, B_HEAD_DIM), lambda b, h, qi: (0, 0)),
            pl.BlockSpec((B_V_DIM, 1), lambda b, h, qi: (0, 0)),
            pl.BlockSpec((1, 2, tq, LANES), lambda b, h, qi: (b, h, qi, 0)),
            pl.BlockSpec((1, 2, seq, LANES), lambda b, h, qi: (b, h, 0, 0)),
            pl.BlockSpec((1, 1, VT_ROWS, seq), lambda b, h, qi: (b, h, 0, 0)),
        ],
        out_specs=pl.BlockSpec((1, tq, B_V_DIM), lambda b, h, qi: (b, qi, h)),
        out_shape=jax.ShapeDtypeStruct((bsz, seq, B_HEADS * B_V_DIM), BF16),
        scratch_shapes=scratch,
        compiler_params=pltpu.CompilerParams(
            dimension_semantics=("arbitrary", "arbitrary", "arbitrary"),
            vmem_limit_bytes=VMEM_LIMIT_BYTES),
        name="diff_attention_bounded" if bounded else "diff_attention",
    )(lam_vecs, sub_g, q_arr, k_arr, vt_arr)


def _first_argmax(vals, iota, top):
    return jnp.min(jnp.where(vals == top, iota, vals.shape[0]), axis=0, keepdims=True)


def _merge_kernel(u_ref, va_ref, yb_ref, ga_ref, gb_ref, x_ref, ws_ref, bs_ref,
                  wua_ref, wub_ref, wo_ref, g2_ref, wr_hi_ref, wr_lo_ref, br_ref,
                  x1_ref, h2_ref, comb_ref, gidx_ref):
    tm = x_ref.shape[0]
    row = lax.broadcasted_iota(jnp.int32, (CHUNK, CHUNK), 0)
    col = lax.broadcasted_iota(jnp.int32, (CHUNK, CHUNK), 1)
    causal = row >= col

    n_chunks = tm // CHUNK
    group_cols = []
    for g in range(A_GROUPS):
        w_g = jnp.where(causal, ws_ref[g], 0.0).astype(BF16)
        gl = slice(g * LANES, (g + 1) * LANES)
        v_g = jnp.concatenate([va_ref[c * CHUNK:(c + 1) * CHUNK, gl] for c in range(n_chunks)],
                              axis=1)
        z = jnp.dot(w_g, v_g, preferred_element_type=F32) + bs_ref[g]
        z = jnp.concatenate([z[:, c * LANES:(c + 1) * LANES] for c in range(n_chunks)], axis=0)
        group_cols.append((u_ref[:, gl].astype(F32) * z).astype(BF16))
    y_a = jnp.concatenate(group_cols, axis=1)

    up_a = jnp.dot(y_a, wua_ref[...], preferred_element_type=F32)
    up_b = jnp.dot(yb_ref[...], wub_ref[...], preferred_element_type=F32)
    merged = ga_ref[...].astype(F32) * up_a + gb_ref[...].astype(F32) * up_b
    x1 = x_ref[...] + jnp.dot(merged.astype(BF16), wo_ref[...], preferred_element_type=F32)
    x1_ref[...] = x1

    r = lax.rsqrt(jnp.mean(x1 * x1, axis=-1, keepdims=True) + EPS)
    h2 = (x1 * r) * g2_ref[...]
    h2_hi = h2.astype(BF16)
    h2_ref[...] = h2_hi
    h2_lo = (h2 - h2_hi.astype(F32)).astype(BF16)

    nt_dims = (((1,), (1,)), ((), ()))
    lt = (lax.dot_general(wr_hi_ref[...], h2_hi, nt_dims, preferred_element_type=F32)
          + lax.dot_general(wr_hi_ref[...], h2_lo, nt_dims, preferred_element_type=F32)
          + lax.dot_general(wr_lo_ref[...], h2_hi, nt_dims, preferred_element_type=F32)
          + br_ref[...])

    iota8 = lax.broadcasted_iota(jnp.int32, (8, tm), 0)
    gl = lt[0:8]
    ge = jnp.exp(gl - jnp.max(gl, axis=0, keepdims=True))
    gp = ge / jnp.sum(ge, axis=0, keepdims=True)
    g_gate = jnp.max(gp, axis=0, keepdims=True)
    g_idx = _first_argmax(gp, iota8, g_gate)

    el = jnp.zeros((EXPERTS_PER_GROUP, tm), F32)
    for gi in range(N_GROUPS):
        el = jnp.where(g_idx == gi, lt[8 + 8 * gi:16 + 8 * gi], el)
    ee = jnp.exp(el - jnp.max(el, axis=0, keepdims=True))
    ep = ee / jnp.sum(ee, axis=0, keepdims=True)
    p_top1 = jnp.max(ep, axis=0, keepdims=True)
    i_top1 = _first_argmax(ep, iota8, p_top1)
    ep_rest = jnp.where(iota8 == i_top1, -1.0, ep)
    p_top2 = jnp.max(ep_rest, axis=0, keepdims=True)
    i_top2 = _first_argmax(ep_rest, iota8, p_top2)
    den = p_top1 + p_top2
    w_top1 = g_gate * (p_top1 / den)
    w_top2 = g_gate * (p_top2 / den)
    in_group = jnp.where(iota8 == i_top1, w_top1, 0.0) + jnp.where(iota8 == i_top2, w_top2, 0.0)
    blocks = [jnp.where(g_idx == gi, in_group, 0.0) for gi in range(N_GROUPS)]
    blocks.append(jnp.zeros((LANES - N_EXPERTS, tm), F32))
    comb_t = jnp.concatenate(blocks, axis=0)
    comb_ref[...] = comb_t.T
    gidx_ref[...] = jnp.broadcast_to(g_idx, gidx_ref.shape)


def _merge_and_route(u, va, yb, ga, gb, x2, ws, bs, wua, wub, wo, g2, wr_hi, wr_lo, br):
    n_tok = x2.shape[0]
    tm = MERGE_TM
    row_spec = pl.BlockSpec((tm, D_MODEL), lambda i: (i, 0))
    full = lambda shape: pl.BlockSpec(shape, lambda i: (0,) * len(shape))
    return pl.pallas_call(
        _merge_kernel,
        grid=(n_tok // tm,),
        in_specs=[
            row_spec, row_spec, row_spec, row_spec, row_spec, row_spec,
            full((A_GROUPS, CHUNK, CHUNK)), full((A_GROUPS, CHUNK, 1)),
            full((D_MODEL, D_MODEL)), full((D_MODEL, D_MODEL)), full((D_MODEL, D_MODEL)),
            full((1, D_MODEL)),
            full((ROUTER_ROWS, D_MODEL)), full((ROUTER_ROWS, D_MODEL)), full((ROUTER_ROWS, 1)),
        ],
        out_specs=[row_spec, row_spec, pl.BlockSpec((tm, LANES), lambda i: (i, 0)),
                   pl.BlockSpec((8, tm), lambda i: (0, i))],
        out_shape=[
            jax.ShapeDtypeStruct((n_tok, D_MODEL), F32),
            jax.ShapeDtypeStruct((n_tok, D_MODEL), BF16),
            jax.ShapeDtypeStruct((n_tok, LANES), F32),
            jax.ShapeDtypeStruct((8, n_tok), jnp.int32),
        ],
        compiler_params=pltpu.CompilerParams(
            dimension_semantics=("arbitrary",),
            vmem_limit_bytes=VMEM_LIMIT_BYTES),
        name="merge_and_route",
    )(u, va, yb, ga, gb, x2, ws, bs, wua, wub, wo, g2, wr_hi, wr_lo, br)


def _moe_kernel(g_ref, h_ref, c_ref, x1_ref, w13_ref, w2_ref, o_ref,
                p_scr, xg_scr, cg_scr, csel_scr, y_scr, bounds_ref):
    e = pl.program_id(1)
    n_steps = pl.num_programs(1)
    tt = h_ref.shape[0]
    rows = p_scr.shape[0]
    rb = MOE_ROW_BLOCK

    @pl.when(e == 0)
    def _():
        g_idx = g_ref[0:1, :]
        slot = lax.broadcasted_iota(jnp.int32, (16, tt), 0)
        onehot = jnp.where(slot == g_idx, 1.0, 0.0)
        upper = (lax.broadcasted_iota(jnp.int32, (LANES, LANES), 0)
                 <= lax.broadcasted_iota(jnp.int32, (LANES, LANES), 1)).astype(BF16)
        carry = jnp.zeros((16, 1), F32)
        counts = []
        for j in range(tt // LANES):
            seg = onehot[:, j * LANES:(j + 1) * LANES].astype(BF16)
            cs = jnp.dot(seg, upper, preferred_element_type=F32) + carry
            counts.append(cs)
            carry = cs[:, LANES - 1:LANES]
        running = jnp.concatenate(counts, axis=1)
        n_blocks = jnp.floor((carry + (rb - 1)) * (1.0 / rb))
        slot_col = lax.broadcasted_iota(jnp.int32, (16, 1), 0)
        first_block = jnp.zeros((16, 1), F32)
        for g in range(N_GROUPS - 1):
            first_block = first_block + jnp.where(slot_col > g, n_blocks[g:g + 1, :], 0.0)
        end_block = first_block + n_blocks
        for g in range(N_GROUPS):
            bounds_ref[g] = first_block[g:g + 1, :][0, 0].astype(jnp.int32)
            bounds_ref[N_GROUPS + g] = end_block[g:g + 1, :][0, 0].astype(jnp.int32)
        pos = jnp.sum(onehot * (first_block * rb + running - 1.0), axis=0, keepdims=True)
        pos = pos.astype(jnp.int32)

        c = c_ref[...]
        c_hi = c.astype(BF16)
        c_rest = c - c_hi.astype(F32)
        c_mid = c_rest.astype(BF16)
        c_lo = (c_rest - c_mid.astype(F32)).astype(BF16)
        h_ext = jnp.concatenate([h_ref[...], c_hi, c_mid, c_lo], axis=1)
        for ch in range(rows // MXU_DIM):
            sl = slice(ch * MXU_DIM, (ch + 1) * MXU_DIM)
            row_id = lax.broadcasted_iota(jnp.int32, (MXU_DIM, tt), 0) + ch * MXU_DIM
            perm = jnp.where(row_id == pos, 1.0, 0.0).astype(BF16)
            p_scr[sl, :] = perm
            got = jnp.dot(perm, h_ext, preferred_element_type=F32)
            xg_scr[sl, :] = got[:, :D_MODEL].astype(BF16)
            cg_scr[sl, :] = (got[:, D_MODEL:D_MODEL + LANES]
                             + got[:, D_MODEL + LANES:D_MODEL + 2 * LANES]
                             + got[:, D_MODEL + 2 * LANES:])
        y_scr[...] = jnp.zeros_like(y_scr)

    for eb in range(N_EXPERTS // MOE_NE):
        @pl.when(e == eb)
        def _():
            csel_scr[...] = cg_scr[:, eb * MOE_NE:(eb + 1) * MOE_NE]

    group = e // (EXPERTS_PER_GROUP // MOE_NE)

    def row_block(b, carry):
        r0 = pl.multiple_of(b * rb, rb)
        xb = xg_scr[pl.ds(r0, rb), :]
        hids = []
        for n in range(MOE_NE):
            ab = jnp.dot(xb, w13_ref[n], preferred_element_type=F32)
            a = ab[:, :D_EXPERT]
            b3 = ab[:, D_EXPERT:]
            hid = (a * (1.0 / (1.0 + jnp.exp(-a)))) * b3
            hids.append((hid * csel_scr[pl.ds(r0, rb), n:n + 1]).astype(BF16))
        hid_all = jnp.concatenate(hids, axis=1)
        y_scr[pl.ds(r0, rb), :] += jnp.dot(hid_all, w2_ref[...], preferred_element_type=F32)
        return carry

    lax.fori_loop(bounds_ref[group], bounds_ref[N_GROUPS + group], row_block, 0)

    @pl.when(e == n_steps - 1)
    def _():
        tn_dims = (((0,), (0,)), ((), ()))
        y = lax.dot_general(p_scr[...], y_scr[...].astype(BF16), tn_dims,
                            preferred_element_type=F32)
        o_ref[...] = x1_ref[...] + y


def _moe(gidx, h2, comb, x1, w13, w2f):
    n_tok = h2.shape[0]
    tt = MOE_TM
    rows = tt + N_GROUPS * MOE_ROW_BLOCK
    return pl.pallas_call(
        _moe_kernel,
        grid=(n_tok // tt, N_EXPERTS // MOE_NE),
        in_specs=[
            pl.BlockSpec((8, tt), lambda i, e: (0, i)),
            pl.BlockSpec((tt, D_MODEL), lambda i, e: (i, 0)),
            pl.BlockSpec((tt, LANES), lambda i, e: (i, 0)),
            pl.BlockSpec((tt, D_MODEL), lambda i, e: (i, 0)),
            pl.BlockSpec((MOE_NE, D_MODEL, 2 * D_EXPERT), lambda i, e: (e, 0, 0)),
            pl.BlockSpec((MOE_NE * D_EXPERT, D_MODEL), lambda i, e: (e, 0)),
        ],
        out_specs=pl.BlockSpec((tt, D_MODEL), lambda i, e: (i, 0)),
        out_shape=jax.ShapeDtypeStruct((n_tok, D_MODEL), F32),
        scratch_shapes=[
            pltpu.VMEM((rows, tt), BF16),
            pltpu.VMEM((rows, D_MODEL), BF16),
            pltpu.VMEM((rows, LANES), F32),
            pltpu.VMEM((rows, MOE_NE), F32),
            pltpu.VMEM((rows, D_MODEL), F32),
            pltpu.SMEM((2 * N_GROUPS,), jnp.int32),
        ],
        compiler_params=pltpu.CompilerParams(
            dimension_semantics=("arbitrary", "arbitrary"),
            vmem_limit_bytes=VMEM_LIMIT_BYTES),
        name="moe_experts",
    )(gidx, h2, comb, x1, w13, w2f)


def _segment_ones(seg):
    idx = jnp.arange(MXU_DIM) // seg
    return (idx[:, None] == idx[None, :]).astype(BF16)


def kernel(x, norm1_g, w_in, v_norm_g, w_s, b_s, q_norm_g, k_norm_g, lambda_q1, lambda_k1, lambda_q2, lambda_k2, sub_norm_g, w_up_a, w_up_b, w_gate, b_gate, w_out, norm2_g, w_rg, b_rg, w_re, b_re, w1, w3, w2):
    bsz, seq, d = x.shape
    assert d == D_MODEL and seq % PROJ_TM == 0 and seq % ATT_TQ == 0 and ATT_TQ == 2 * ATT_TK
    assert norm1_g.shape[0] == 1, "single layer"
    n_tok = bsz * seq
    assert n_tok % MOE_TM == 0 and n_tok % MERGE_TM == 0
    x2 = x.reshape(n_tok, d)

    w_cat = jnp.concatenate([w_in[0], w_gate[0]], axis=1).astype(BF16)
    gv = v_norm_g[0].reshape(1, D_MODEL)
    gq = jnp.tile(q_norm_g[0] * (B_HEAD_DIM ** -0.5 * LOG2_E), 2 * B_HEADS)[None, :]
    gk = jnp.tile(k_norm_g[0], 2 * B_HEADS)[None, :]
    score_bound = (SCORE_BOUND_MARGIN * B_HEAD_DIM ** 0.5 * LOG2_E
                   * jnp.max(jnp.abs(q_norm_g[0] * k_norm_g[0]))).reshape(1, 1)
    u, va, q_arr, k_arr, vt_arr, ga, gb = _input_projection(
        x2, norm1_g, w_cat, b_gate, gv, gq, gk, _segment_ones(B_HEAD_DIM), _segment_ones(LANES),
        score_bound, bsz, seq)

    lam_vecs = jnp.concatenate([lambda_q1, lambda_k1, lambda_q2, lambda_k2], axis=0)
    attn_args = (lam_vecs, sub_norm_g[0][:, None], q_arr, k_arr, vt_arr)
    yb = lax.cond(score_bound[0, 0] <= MAX_BOUNDED_SCORE,
                  lambda args: _diff_attention(*args, bounded=True),
                  lambda args: _diff_attention(*args, bounded=False), attn_args)
    yb = yb.reshape(n_tok, D_MODEL)

    pad_rows = ROUTER_ROWS - 8 - N_EXPERTS
    wr = jnp.concatenate([w_rg[0].T, jnp.zeros((4, d), F32), w_re[0].T,
                          jnp.zeros((pad_rows, d), F32)], axis=0)
    br = jnp.concatenate([b_rg[0], jnp.full((4,), NEG_BIG, F32), b_re[0],
                          jnp.zeros((pad_rows,), F32)])[:, None]
    wr_hi = wr.astype(BF16)
    wr_lo = (wr - wr_hi.astype(F32)).astype(BF16)
    x1, h2, comb, gidx = _merge_and_route(
        u, va, yb, ga, gb, x2, w_s[0], b_s[0][:, :, None],
        w_up_a[0].astype(BF16), w_up_b[0].astype(BF16), w_out[0].astype(BF16),
        norm2_g, wr_hi, wr_lo, br)

    w13 = jnp.concatenate([w1[0], w3[0]], axis=2).astype(BF16)
    w2f = w2[0].astype(BF16).reshape(N_EXPERTS * D_EXPERT, D_MODEL)
    out = _moe(gidx, h2, comb, x1, w13, w2f)
    return out.reshape(bsz, seq, d)
```

```python
import functools
import math

import jax
import jax.numpy as jnp
import numpy as np
from jax import lax
from jax.experimental import pallas as pl
from jax.experimental.pallas import tpu as pltpu

F32 = jnp.float32
BF16 = jnp.bfloat16

D_MODEL = 1024
EPS = 1e-6
A_GROUPS = 8
CHUNK = 128
B_HEADS = 8
B_HEAD_DIM = 64
B_V_DIM = 128
IN_COLS = 5 * D_MODEL
N_GROUPS = 4
EXPERTS_PER_GROUP = 8
N_EXPERTS = 32
D_EXPERT = 256
LAM_INIT = 0.8 - 0.6 * math.exp(-0.3 * 0)

LANES = 128
MXU_DIM = 256
VMEM_LIMIT_BYTES = 56 * 1024 * 1024

NEG_BIG = -1e30
LOG2_E = math.log2(math.e)
ALIBI_SLOPES = tuple(2.0 ** (-8.0 * (h + 1) / B_HEADS) for h in range(B_HEADS))
SLOPE_PARTS = 4
VT_ROWS = B_V_DIM + 16
SCORE_BOUND_MARGIN = 1.02
MAX_BOUNDED_SCORE = 48.0

PROJ_TM = 512
ATT_TQ = 1024
ATT_TK = 512
MERGE_TM = 512
MERGE_SUBTILES = 1
MOE_TM = 1024
MOE_NE = 4
MOE_ROW_BLOCK = 128
ROUTER_ROWS = 48


def _bf16_parts(value, n_parts):
    parts, rest = [], np.float64(value)
    for _ in range(n_parts):
        part = np.float64(np.asarray(rest, np.float32).astype(BF16).astype(np.float32))
        parts.append(float(part))
        rest = rest - part
    return parts


def _segment_rms_scale(acc, seg_ones_ref, seg):
    sq = (acc * acc).astype(BF16)
    parts = []
    for p in range(acc.shape[1] // MXU_DIM):
        parts.append(jnp.dot(sq[:, p * MXU_DIM:(p + 1) * MXU_DIM], seg_ones_ref[...],
                             preferred_element_type=F32))
    ss = jnp.concatenate(parts, axis=1)
    return lax.rsqrt(ss * (1.0 / seg) + EPS)


def _inproj_kernel(x_ref, g1_ref, w_ref, bias_ref, gv_ref, gq_ref, gk_ref, ones64_ref, ones128_ref,
                   bound_ref,
                   u_ref, va_ref, q_ref, k_ref, vt_ref, ga_ref, gb_ref, *, seq_blocks):
    i = pl.program_id(0)
    tm = x_ref.shape[0]
    x = x_ref[...]
    r = lax.rsqrt(jnp.mean(x * x, axis=-1, keepdims=True) + EPS)
    h = ((x * r) * g1_ref[...]).astype(BF16)

    def slab(j):
        return jnp.dot(h, w_ref[:, j * D_MODEL:(j + 1) * D_MODEL], preferred_element_type=F32)

    lane = lax.broadcasted_iota(jnp.int32, (1, LANES), 1)
    lo_half = lane < B_HEAD_DIM

    u_ref[...] = slab(0).astype(BF16)

    acc = slab(1)
    r = _segment_rms_scale(acc, ones128_ref, LANES)
    va_ref[...] = ((acc * r) * gv_ref[...]).astype(BF16)

    spare_col = lane & (B_HEAD_DIM - 1)
    pos = (i % seq_blocks) * tm + lax.broadcasted_iota(jnp.int32, (tm, LANES), 0)
    pos_lo = (pos & (MXU_DIM - 1)).astype(F32)
    pos_hi = (pos - (pos & (MXU_DIM - 1))).astype(F32)
    pos_cols = jnp.where(spare_col & SLOPE_PARTS == 0, pos_lo, pos_hi)
    bound = bound_ref[...]
    bound_hi = bound.astype(BF16).astype(F32)
    bound_mid = (bound - bound_hi).astype(BF16).astype(F32)
    bound_lo = (bound - bound_hi - bound_mid).astype(BF16).astype(F32)
    bound_cols = jnp.where(spare_col == 4 * SLOPE_PARTS, bound_hi,
                           jnp.where(spare_col == 4 * SLOPE_PARTS + 1, bound_mid,
                                     jnp.where(spare_col == 4 * SLOPE_PARTS + 2, bound_lo, 0.0)))
    q_shared = jnp.where((spare_col >= 2 * SLOPE_PARTS) & (spare_col < 4 * SLOPE_PARTS),
                         pos_cols, bound_cols)
    k_shared = jnp.where(spare_col < 2 * SLOPE_PARTS, pos_cols, 0.0)

    def slope_cols(hd, sign):
        parts = _bf16_parts(ALIBI_SLOPES[hd] * LOG2_E, SLOPE_PARTS)
        cvec = jnp.zeros((1, LANES), F32)
        for n, part in enumerate(parts):
            cvec = jnp.where(spare_col & (SLOPE_PARTS - 1) == n, sign * part, cvec)
        return cvec

    acc = slab(2)
    r = _segment_rms_scale(acc, ones64_ref, B_HEAD_DIM)
    qn = (acc * r) * gq_ref[...]
    for hd in range(B_HEADS):
        extra = jnp.where(spare_col < 2 * SLOPE_PARTS, slope_cols(hd, 1.0), q_shared)
        blk = qn[:, hd * LANES:(hd + 1) * LANES]
        q_ref[0, 2 * hd] = jnp.where(lo_half, blk, extra).astype(BF16)
        q_ref[0, 2 * hd + 1] = jnp.where(lo_half, extra, blk).astype(BF16)

    acc = slab(3)
    r = _segment_rms_scale(acc, ones64_ref, B_HEAD_DIM)
    kn = (acc * r) * gk_ref[...]
    for hd in range(B_HEADS):
        consts = jnp.where(spare_col < 4 * SLOPE_PARTS, slope_cols(hd, -1.0),
                           jnp.where(spare_col < 4 * SLOPE_PARTS + 3, -1.0, 0.0))
        extra = jnp.where(spare_col < 2 * SLOPE_PARTS, k_shared, consts)
        blk = kn[:, hd * LANES:(hd + 1) * LANES]
        k_ref[0, 2 * hd] = jnp.where(lo_half, blk, extra).astype(BF16)
        k_ref[0, 2 * hd + 1] = jnp.where(lo_half, extra, blk).astype(BF16)

    acc = slab(4)
    for hd in range(B_HEADS):
        vt_ref[0, hd, 0:B_V_DIM, :] = acc[:, hd * LANES:(hd + 1) * LANES].T.astype(BF16)
        vt_ref[0, hd, B_V_DIM:VT_ROWS, :] = jnp.ones((VT_ROWS - B_V_DIM, tm), BF16)

    ga_ref[...] = (1.0 / (1.0 + jnp.exp(-(slab(5) + bias_ref[:, 0:D_MODEL])))).astype(BF16)
    gb_ref[...] = (1.0 / (1.0 + jnp.exp(-(slab(6) + bias_ref[:, D_MODEL:])))).astype(BF16)


def _input_projection(x2, g1, w_cat, b_gate, gv, gq, gk, ones64, ones128, bound, bsz, seq):
    n_tok = x2.shape[0]
    tm = PROJ_TM
    seq_blocks = seq // tm
    row_spec = pl.BlockSpec((tm, D_MODEL), lambda i: (i, 0))
    resident = lambda shape: pl.BlockSpec(shape, lambda i: (0,) * len(shape),
                                          pipeline_mode=pl.Buffered(1))
    head_map = lambda i: (i // seq_blocks, 0, i % seq_blocks, 0)
    tok_bf16 = jax.ShapeDtypeStruct((n_tok, D_MODEL), BF16)
    return pl.pallas_call(
        functools.partial(_inproj_kernel, seq_blocks=seq_blocks),
        grid=(n_tok // tm,),
        in_specs=[
            row_spec,
            resident((1, D_MODEL)),
            resident(w_cat.shape),
            resident(b_gate.shape),
            resident((1, D_MODEL)), resident((1, D_MODEL)), resident((1, D_MODEL)),
            resident((MXU_DIM, MXU_DIM)), resident((MXU_DIM, MXU_DIM)),
            resident((1, 1)),
        ],
        out_specs=[
            row_spec, row_spec,
            pl.BlockSpec((1, 2 * B_HEADS, tm, LANES), head_map),
            pl.BlockSpec((1, 2 * B_HEADS, tm, LANES), head_map),
            pl.BlockSpec((1, B_HEADS, VT_ROWS, tm), lambda i: (i // seq_blocks, 0, 0, i % seq_blocks)),
            row_spec, row_spec,
        ],
        out_shape=[
            tok_bf16, tok_bf16,
            jax.ShapeDtypeStruct((bsz, 2 * B_HEADS, seq, LANES), BF16),
            jax.ShapeDtypeStruct((bsz, 2 * B_HEADS, seq, LANES), BF16),
            jax.ShapeDtypeStruct((bsz, B_HEADS, VT_ROWS, seq), BF16),
            tok_bf16, tok_bf16,
        ],
        compiler_params=pltpu.CompilerParams(
            dimension_semantics=("arbitrary",),
            vmem_limit_bytes=VMEM_LIMIT_BYTES),
        name="input_projection",
    )(x2, g1, w_cat, b_gate, gv, gq, gk, ones64, ones128, bound)


def _attn_kernel(lam_ref, sg_ref, q_ref, k_ref, vt_ref, o_ref, acc_scr, sa_scr, sb_scr):
    qi = pl.program_id(2)
    tq = q_ref.shape[2]
    tk = ATT_TK
    nt_dims = (((1,), (1,)), ((), ()))

    assert tq == 2 * tk
    tri = (lax.broadcasted_iota(jnp.int32, (tk, tk), 0)
           <= lax.broadcasted_iota(jnp.int32, (tk, tk), 1))

    def scores(kb, s_scr, diag=None):
        ks = pl.multiple_of(kb * tk, tk)
        maxima = []
        for c in range(2):
            k_blk = k_ref[0, c, pl.ds(ks, tk), :]
            if diag is None:
                s = lax.dot_general(k_blk, q_ref[0, c], nt_dims, preferred_element_type=F32)
                s_scr[c] = s
                maxima.append(jnp.max(s, axis=0, keepdims=True))
                continue
            parts = []
            for half in range(diag, 2):
                cols = slice(half * tk, (half + 1) * tk)
                s = lax.dot_general(k_blk, q_ref[0, c, cols, :], nt_dims,
                                    preferred_element_type=F32)
                if half == diag:
                    s = jnp.where(tri, s, NEG_BIG)
                s_scr[c, :, cols] = s
                parts.append(jnp.max(s, axis=0, keepdims=True))
            maxima.append(parts[0] if len(parts) == 1 else jnp.concatenate(parts, axis=1))
        return tuple(maxima)

    def consume(kb, s_scr, m_old, blk_max, col0=0, first=False):
        ks = pl.multiple_of(kb * tk, tk)
        vt = vt_ref[0, 0, :, pl.ds(ks, tk)]
        m_new = []
        for c in range(2):
            m_c = blk_max[c] if first else jnp.maximum(m_old[c][:, col0:], blk_max[c])
            p = jnp.exp2(s_scr[c, :, col0:] - m_c).astype(BF16)
            pv = jnp.dot(vt, p, preferred_element_type=F32)
            cols = slice(c * tq + col0, (c + 1) * tq)
            if first:
                acc_scr[:, cols] = pv
            else:
                acc_scr[:, cols] = acc_scr[:, cols] * jnp.exp2(m_old[c][:, col0:] - m_c) + pv
            m_new.append(m_c)
        return tuple(m_new)

    max_a = scores(2 * qi, sa_scr, diag=0)
    max_b = scores(2 * qi + 1, sb_scr, diag=1)
    m = consume(2 * qi, sa_scr, None, max_a, first=True)
    max_a = scores(0, sa_scr)
    m_right = consume(2 * qi + 1, sb_scr, m, max_b, col0=tk)
    m = tuple(jnp.concatenate([m[c][:, :tk], m_right[c]], axis=1) for c in range(2))

    def trip(t, carry):
        m, max_a = carry
        max_b = scores(2 * t + 1, sb_scr)
        m = consume(2 * t, sa_scr, m, max_a)
        max_a = scores(2 * t + 2, sa_scr)
        m = consume(2 * t + 1, sb_scr, m, max_b)
        return m, max_a

    m, max_a = lax.fori_loop(0, qi - 1, trip, (m, max_a))

    @pl.when(qi > 0)
    def _():
        max_b = scores(2 * qi - 1, sb_scr)
        m_mid = consume(2 * qi - 2, sa_scr, m, max_a)
        consume(2 * qi - 1, sb_scr, m_mid, max_b)

    _attn_finalize(lam_ref, sg_ref, acc_scr, o_ref)


def _attn_finalize(lam_ref, sg_ref, acc_scr, o_ref):
    tq = o_ref.shape[1]
    lam_v = lam_ref[...]
    lam = (jnp.exp(jnp.sum(lam_v[0:1] * lam_v[1:2], axis=1, keepdims=True))
           - jnp.exp(jnp.sum(lam_v[2:3] * lam_v[3:4], axis=1, keepdims=True)) + LAM_INIT)
    o0 = acc_scr[0:B_V_DIM, 0:tq] / acc_scr[B_V_DIM:B_V_DIM + 1, 0:tq]
    o1 = acc_scr[0:B_V_DIM, tq:2 * tq] / acc_scr[B_V_DIM:B_V_DIM + 1, tq:2 * tq]
    o = o0 - lam * o1
    r = lax.rsqrt(jnp.mean(o * o, axis=0, keepdims=True) + EPS)
    o = ((o * r) * sg_ref[...]) * (1.0 - LAM_INIT)
    o_ref[0] = o.T.astype(BF16)


def _attn_bounded_kernel(lam_ref, sg_ref, q_ref, k_ref, vt_ref, o_ref, acc_scr):
    qi = pl.program_id(2)
    tq = q_ref.shape[2]
    tk = ATT_TK
    nt_dims = (((1,), (1,)), ((), ()))
    assert tq == 2 * tk
    tri = (lax.broadcasted_iota(jnp.int32, (tk, tk), 0)
           <= lax.broadcasted_iota(jnp.int32, (tk, tk), 1))

    def fold(key_start, n_keys, q_cols, masked, first):
        vt = vt_ref[0, 0, 0:B_V_DIM, pl.ds(key_start, n_keys)]
        for c in range(2):
            e = lax.dot_general(k_ref[0, c, pl.ds(key_start, n_keys), :], q_ref[0, c, q_cols, :],
                                nt_dims, preferred_element_type=F32)
            if masked:
                e_diag = jnp.where(tri, e[n_keys - tk:, :], NEG_BIG)
                e = e_diag if n_keys == tk else jnp.concatenate([e[:n_keys - tk, :], e_diag], axis=0)
            p = jnp.exp2(e)
            p_sum = jnp.sum(p, axis=0, keepdims=True)
            pv = jnp.dot(vt, p.astype(BF16), preferred_element_type=F32)
            cols = slice(c * tq + q_cols.start, c * tq + q_cols.stop)
            if first:
                acc_scr[0:B_V_DIM, cols] = pv
                acc_scr[B_V_DIM:B_V_DIM + 1, cols] = p_sum
            else:
                acc_scr[0:B_V_DIM, cols] += pv
                acc_scr[B_V_DIM:B_V_DIM + 1, cols] += p_sum

    left, right, full = slice(0, tk), slice(tk, tq), slice(0, tq)
    diag0 = pl.multiple_of(qi * tq, tq)
    fold(diag0, tk, left, True, True)
    fold(diag0, tq, right, True, True)

    def trip(t, carry):
        fold(pl.multiple_of(t * tq, tq), tq, full, False, False)
        return carry

    lax.fori_loop(0, qi, trip, 0)
    _attn_finalize(lam_ref, sg_ref, acc_scr, o_ref)


def _diff_attention(lam_vecs, sub_g, q_arr, k_arr, vt_arr, bounded):
    bsz, _, seq, _ = q_arr.shape
    tq = ATT_TQ
    scratch = [pltpu.VMEM((VT_ROWS, 2 * tq), F32)]
    if not bounded:
        scratch += [pltpu.VMEM((2, ATT_TK, tq), F32), pltpu.VMEM((2, ATT_TK, tq), F32)]
    return pl.pallas_call(
        _attn_bounded_kernel if bounded else _attn_kernel,
        grid=(bsz, B_HEADS, seq // tq),
        in_specs=[
            pl.BlockSpec((4, B_HEAD_DIM), lambda b, h, qi: (0, 0)),
            pl.BlockSpec((B_V_DIM, 1), lambda b, h, qi: (0, 0)),
            pl.BlockSpec((1, 2, tq, LANES), lambda b, h, qi: (b, h, qi, 0)),
            pl.BlockSpec((1, 2, seq, LANES), lambda b, h, qi: (b, h, 0, 0)),
            pl.BlockSpec((1, 1, VT_ROWS, seq), lambda b, h, qi: (b, h, 0, 0)),
        ],
        out_specs=pl.BlockSpec((1, tq, B_V_DIM), lambda b, h, qi: (b, qi, h)),
        out_shape=jax.ShapeDtypeStruct((bsz, seq, B_HEADS * B_V_DIM), BF16),
        scratch_shapes=scratch,
        compiler_params=pltpu.CompilerParams(
            dimension_semantics=("arbitrary", "arbitrary", "arbitrary"),
            vmem_limit_bytes=VMEM_LIMIT_BYTES),
        name="diff_attention_bounded" if bounded else "diff_attention",
    )(lam_vecs, sub_g, q_arr, k_arr, vt_arr)


def _first_argmax(vals, iota, top):
    return jnp.min(jnp.where(vals == top, iota, vals.shape[0]), axis=0, keepdims=True)


def _merge_kernel(u_ref, va_ref, yb_ref, ga_ref, gb_ref, x_ref, ws_ref, bs_ref,
                  wua_ref, wub_ref, wo_ref, g2_ref, wr_hi_ref, wr_lo_ref, br_ref,
                  x1_ref, h2_ref, comb_ref, gidx_ref):
    tm = x_ref.shape[0]
    row = lax.broadcasted_iota(jnp.int32, (CHUNK, CHUNK), 0)
    col = lax.broadcasted_iota(jnp.int32, (CHUNK, CHUNK), 1)
    causal = row >= col

    w_tril = [jnp.where(causal, ws_ref[g], 0.0).astype(BF16) for g in range(A_GROUPS)]
    for sub in range(MERGE_SUBTILES):
        rows = slice(sub * (tm // MERGE_SUBTILES), (sub + 1) * (tm // MERGE_SUBTILES))
        _merge_rows(rows, w_tril, u_ref, va_ref, yb_ref, ga_ref, gb_ref, x_ref, bs_ref,
                    wua_ref, wub_ref, wo_ref, g2_ref, wr_hi_ref, wr_lo_ref, br_ref,
                    x1_ref, h2_ref, comb_ref, gidx_ref)


def _merge_rows(rows, w_tril, u_ref, va_ref, yb_ref, ga_ref, gb_ref, x_ref, bs_ref,
                wua_ref, wub_ref, wo_ref, g2_ref, wr_hi_ref, wr_lo_ref, br_ref,
                x1_ref, h2_ref, comb_ref, gidx_ref):
    tm = rows.stop - rows.start
    n_chunks = tm // CHUNK
    group_cols = []
    for g in range(A_GROUPS):
        gl = slice(g * LANES, (g + 1) * LANES)
        v_g = jnp.concatenate(
            [va_ref[rows.start + c * CHUNK:rows.start + (c + 1) * CHUNK, gl] for c in range(n_chunks)],
            axis=1)
        z = jnp.dot(w_tril[g], v_g, preferred_element_type=F32) + bs_ref[g]
        z = jnp.concatenate([z[:, c * LANES:(c + 1) * LANES] for c in range(n_chunks)], axis=0)
        group_cols.append((u_ref[rows, gl].astype(F32) * z).astype(BF16))
    y_a = jnp.concatenate(group_cols, axis=1)

    up_a = jnp.dot(y_a, wua_ref[...], preferred_element_type=F32)
    up_b = jnp.dot(yb_ref[rows, :], wub_ref[...], preferred_element_type=F32)
    merged = ga_ref[rows, :].astype(F32) * up_a + gb_ref[rows, :].astype(F32) * up_b
    x1 = x_ref[rows, :] + jnp.dot(merged.astype(BF16), wo_ref[...], preferred_element_type=F32)
    x1_ref[rows, :] = x1

    r = lax.rsqrt(jnp.mean(x1 * x1, axis=-1, keepdims=True) + EPS)
    h2 = (x1 * r) * g2_ref[...]
    h2_hi = h2.astype(BF16)
    h2_ref[rows, :] = h2_hi
    h2_lo = (h2 - h2_hi.astype(F32)).astype(BF16)

    nt_dims = (((1,), (1,)), ((), ()))
    lt = (lax.dot_general(wr_hi_ref[...], h2_hi, nt_dims, preferred_element_type=F32)
          + lax.dot_general(wr_hi_ref[...], h2_lo, nt_dims, preferred_element_type=F32)
          + lax.dot_general(wr_lo_ref[...], h2_hi, nt_dims, preferred_element_type=F32)
          + br_ref[...])

    iota8 = lax.broadcasted_iota(jnp.int32, (8, tm), 0)
    gl = lt[0:8]
    ge = jnp.exp(gl - jnp.max(gl, axis=0, keepdims=True))
    gp = ge / jnp.sum(ge, axis=0, keepdims=True)
    g_gate = jnp.max(gp, axis=0, keepdims=True)
    g_idx = _first_argmax(gp, iota8, g_gate)

    el = jnp.zeros((EXPERTS_PER_GROUP, tm), F32)
    for gi in range(N_GROUPS):
        el = jnp.where(g_idx == gi, lt[8 + 8 * gi:16 + 8 * gi], el)
    ee = jnp.exp(el - jnp.max(el, axis=0, keepdims=True))
    ep = ee / jnp.sum(ee, axis=0, keepdims=True)
    p_top1 = jnp.max(ep, axis=0, keepdims=True)
    i_top1 = _first_argmax(ep, iota8, p_top1)
    ep_rest = jnp.where(iota8 == i_top1, -1.0, ep)
    p_top2 = jnp.max(ep_rest, axis=0, keepdims=True)
    i_top2 = _first_argmax(ep_rest, iota8, p_top2)
    den = p_top1 + p_top2
    w_top1 = g_gate * (p_top1 / den)
    w_top2 = g_gate * (p_top2 / den)
    in_group = jnp.where(iota8 == i_top1, w_top1, 0.0) + jnp.where(iota8 == i_top2, w_top2, 0.0)
    blocks = [jnp.where(g_idx == gi, in_group, 0.0) for gi in range(N_GROUPS)]
    blocks.append(jnp.zeros((LANES - N_EXPERTS, tm), F32))
    comb_t = jnp.concatenate(blocks, axis=0)
    comb_ref[rows, :] = comb_t.T
    gidx_ref[:, rows] = jnp.broadcast_to(g_idx, (gidx_ref.shape[0], tm))


def _merge_and_route(u, va, yb, ga, gb, x2, ws, bs, wua, wub, wo, g2, wr_hi, wr_lo, br):
    n_tok = x2.shape[0]
    tm = MERGE_TM
    row_spec = pl.BlockSpec((tm, D_MODEL), lambda i: (i, 0))
    full = lambda shape: pl.BlockSpec(shape, lambda i: (0,) * len(shape))
    return pl.pallas_call(
        _merge_kernel,
        grid=(n_tok // tm,),
        in_specs=[
            row_spec, row_spec, row_spec, row_spec, row_spec, row_spec,
            full((A_GROUPS, CHUNK, CHUNK)), full((A_GROUPS, CHUNK, 1)),
            full((D_MODEL, D_MODEL)), full((D_MODEL, D_MODEL)), full((D_MODEL, D_MODEL)),
            full((1, D_MODEL)),
            full((ROUTER_ROWS, D_MODEL)), full((ROUTER_ROWS, D_MODEL)), full((ROUTER_ROWS, 1)),
        ],
        out_specs=[row_spec, row_spec, pl.BlockSpec((tm, LANES), lambda i: (i, 0)),
                   pl.BlockSpec((8, tm), lambda i: (0, i))],
        out_shape=[
            jax.ShapeDtypeStruct((n_tok, D_MODEL), F32),
            jax.ShapeDtypeStruct((n_tok, D_MODEL), BF16),
            jax.ShapeDtypeStruct((n_tok, LANES), F32),
            jax.ShapeDtypeStruct((8, n_tok), jnp.int32),
        ],
        compiler_params=pltpu.CompilerParams(
            dimension_semantics=("arbitrary",),
            vmem_limit_bytes=VMEM_LIMIT_BYTES),
        name="merge_and_route",
    )(u, va, yb, ga, gb, x2, ws, bs, wua, wub, wo, g2, wr_hi, wr_lo, br)


def _moe_kernel(g_ref, h_ref, c_ref, x1_ref, w13_ref, w2_ref, o_ref,
                p_scr, xg_scr, cg_scr, y_scr, bounds_ref):
    e = pl.program_id(1)
    n_steps = pl.num_programs(1)
    tt = h_ref.shape[0]
    rows = p_scr.shape[0]
    rb = MOE_ROW_BLOCK

    @pl.when(e == 0)
    def _():
        g_idx = g_ref[0:1, :]
        slot = lax.broadcasted_iota(jnp.int32, (16, tt), 0)
        onehot = jnp.where(slot == g_idx, 1.0, 0.0)
        upper = (lax.broadcasted_iota(jnp.int32, (LANES, LANES), 0)
                 <= lax.broadcasted_iota(jnp.int32, (LANES, LANES), 1)).astype(BF16)
        carry = jnp.zeros((16, 1), F32)
        counts = []
        for j in range(tt // LANES):
            seg = onehot[:, j * LANES:(j + 1) * LANES].astype(BF16)
            cs = jnp.dot(seg, upper, preferred_element_type=F32) + carry
            counts.append(cs)
            carry = cs[:, LANES - 1:LANES]
        running = jnp.concatenate(counts, axis=1)
        n_blocks = jnp.floor((carry + (rb - 1)) * (1.0 / rb))
        slot_col = lax.broadcasted_iota(jnp.int32, (16, 1), 0)
        first_block = jnp.zeros((16, 1), F32)
        for g in range(N_GROUPS - 1):
            first_block = first_block + jnp.where(slot_col > g, n_blocks[g:g + 1, :], 0.0)
        end_block = first_block + n_blocks
        for g in range(N_GROUPS):
            bounds_ref[g] = first_block[g:g + 1, :][0, 0].astype(jnp.int32)
            bounds_ref[N_GROUPS + g] = end_block[g:g + 1, :][0, 0].astype(jnp.int32)
        pos = jnp.sum(onehot * (first_block * rb + running - 1.0), axis=0, keepdims=True)
        pos = pos.astype(jnp.int32)

        c = c_ref[...]
        c_hi = c.astype(BF16)
        c_rest = c - c_hi.astype(F32)
        c_mid = c_rest.astype(BF16)
        c_lo = (c_rest - c_mid.astype(F32)).astype(BF16)
        h_ext = jnp.concatenate([h_ref[...], c_hi, c_mid, c_lo], axis=1)
        for ch in range(rows // MXU_DIM):
            sl = slice(ch * MXU_DIM, (ch + 1) * MXU_DIM)
            row_id = lax.broadcasted_iota(jnp.int32, (MXU_DIM, tt), 0) + ch * MXU_DIM
            perm = jnp.where(row_id == pos, 1.0, 0.0).astype(BF16)
            p_scr[sl, :] = perm
            got = jnp.dot(perm, h_ext, preferred_element_type=F32)
            xg_scr[sl, :] = got[:, :D_MODEL].astype(BF16)
            cg_scr[sl, :] = (got[:, D_MODEL:D_MODEL + LANES]
                             + got[:, D_MODEL + LANES:D_MODEL + 2 * LANES]
                             + got[:, D_MODEL + 2 * LANES:])
        y_scr[...] = jnp.zeros_like(y_scr)

    group = e // (EXPERTS_PER_GROUP // MOE_NE)
    to_front = (LANES - e * MOE_NE) & (LANES - 1)

    def row_block(b):
        r0 = pl.multiple_of(b * rb, rb)
        xb = xg_scr[pl.ds(r0, rb), :]
        cb = pltpu.roll(cg_scr[pl.ds(r0, rb), :], to_front, axis=1)
        hids = []
        for n in range(MOE_NE):
            ab = jnp.dot(xb, w13_ref[n], preferred_element_type=F32)
            a = ab[:, :D_EXPERT]
            b3 = ab[:, D_EXPERT:]
            hid = (a * (1.0 / (1.0 + jnp.exp(-a)))) * b3
            hids.append((hid * cb[:, n:n + 1]).astype(BF16))
        hid_all = jnp.concatenate(hids, axis=1)
        y_scr[pl.ds(r0, rb), :] += jnp.dot(hid_all, w2_ref[...], preferred_element_type=F32)

    first = bounds_ref[group]
    n_blocks = bounds_ref[N_GROUPS + group] - first

    def pair(i, carry):
        row_block(first + 2 * i)
        row_block(first + 2 * i + 1)
        return carry

    lax.fori_loop(0, n_blocks // 2, pair, 0)

    @pl.when(n_blocks % 2 == 1)
    def _():
        row_block(first + n_blocks - 1)

    @pl.when(e == n_steps - 1)
    def _():
        tn_dims = (((0,), (0,)), ((), ()))
        y = lax.dot_general(p_scr[...], y_scr[...].astype(BF16), tn_dims,
                            preferred_element_type=F32)
        o_ref[...] = x1_ref[...] + y


def _moe(gidx, h2, comb, x1, w13, w2f):
    n_tok = h2.shape[0]
    tt = MOE_TM
    rows = tt + N_GROUPS * MOE_ROW_BLOCK
    return pl.pallas_call(
        _moe_kernel,
        grid=(n_tok // tt, N_EXPERTS // MOE_NE),
        in_specs=[
            pl.BlockSpec((8, tt), lambda i, e: (0, i)),
            pl.BlockSpec((tt, D_MODEL), lambda i, e: (i, 0)),
            pl.BlockSpec((tt, LANES), lambda i, e: (i, 0)),
            pl.BlockSpec((tt, D_MODEL), lambda i, e: (i, 0)),
            pl.BlockSpec((MOE_NE, D_MODEL, 2 * D_EXPERT), lambda i, e: (e, 0, 0)),
            pl.BlockSpec((MOE_NE * D_EXPERT, D_MODEL), lambda i, e: (e, 0)),
        ],
        out_specs=pl.BlockSpec((tt, D_MODEL), lambda i, e: (i, 0)),
        out_shape=jax.ShapeDtypeStruct((n_tok, D_MODEL), F32),
        scratch_shapes=[
            pltpu.VMEM((rows, tt), BF16),
            pltpu.VMEM((rows, D_MODEL), BF16),
            pltpu.VMEM((rows, LANES), F32),
            pltpu.VMEM((rows, D_MODEL), F32),
            pltpu.SMEM((2 * N_GROUPS,), jnp.int32),
        ],
        compiler_params=pltpu.CompilerParams(
            dimension_semantics=("arbitrary", "arbitrary"),
            vmem_limit_bytes=VMEM_LIMIT_BYTES),
        name="moe_experts",
    )(gidx, h2, comb, x1, w13, w2f)


def _segment_ones(seg):
    idx = jnp.arange(MXU_DIM) // seg
    return (idx[:, None] == idx[None, :]).astype(BF16)


def kernel(x, norm1_g, w_in, v_norm_g, w_s, b_s, q_norm_g, k_norm_g, lambda_q1, lambda_k1, lambda_q2, lambda_k2, sub_norm_g, w_up_a, w_up_b, w_gate, b_gate, w_out, norm2_g, w_rg, b_rg, w_re, b_re, w1, w3, w2):
    bsz, seq, d = x.shape
    assert d == D_MODEL and seq % PROJ_TM == 0 and seq % ATT_TQ == 0 and ATT_TQ == 2 * ATT_TK
    assert norm1_g.shape[0] == 1, "single layer"
    n_tok = bsz * seq
    assert n_tok % MOE_TM == 0 and n_tok % MERGE_TM == 0
    x2 = x.reshape(n_tok, d)

    w_cat = jnp.concatenate([w_in[0], w_gate[0]], axis=1).astype(BF16)
    gv = v_norm_g[0].reshape(1, D_MODEL)
    gq = jnp.tile(q_norm_g[0] * (B_HEAD_DIM ** -0.5 * LOG2_E), 2 * B_HEADS)[None, :]
    gk = jnp.tile(k_norm_g[0], 2 * B_HEADS)[None, :]
    score_bound = (SCORE_BOUND_MARGIN * B_HEAD_DIM ** 0.5 * LOG2_E
                   * jnp.max(jnp.abs(q_norm_g[0] * k_norm_g[0]))).reshape(1, 1)
    u, va, q_arr, k_arr, vt_arr, ga, gb = _input_projection(
        x2, norm1_g, w_cat, b_gate, gv, gq, gk, _segment_ones(B_HEAD_DIM), _segment_ones(LANES),
        score_bound, bsz, seq)

    lam_vecs = jnp.concatenate([lambda_q1, lambda_k1, lambda_q2, lambda_k2], axis=0)
    attn_args = (lam_vecs, sub_norm_g[0][:, None], q_arr, k_arr, vt_arr)
    yb = lax.cond(score_bound[0, 0] <= MAX_BOUNDED_SCORE,
                  lambda args: _diff_attention(*args, bounded=True),
                  lambda args: _diff_attention(*args, bounded=False), attn_args)
    yb = yb.reshape(n_tok, D_MODEL)

    pad_rows = ROUTER_ROWS - 8 - N_EXPERTS
    wr = jnp.concatenate([w_rg[0].T, jnp.zeros((4, d), F32), w_re[0].T,
                          jnp.zeros((pad_rows, d), F32)], axis=0)
    br = jnp.concatenate([b_rg[0], jnp.full((4,), NEG_BIG, F32), b_re[0],
                          jnp.zeros((pad_rows,), F32)])[:, None]
    wr_hi = wr.astype(BF16)
    wr_lo = (wr - wr_hi.astype(F32)).astype(BF16)
    x1, h2, comb, gidx = _merge_and_route(
        u, va, yb, ga, gb, x2, w_s[0], b_s[0][:, :, None],
        w_up_a[0].astype(BF16), w_up_b[0].astype(BF16), w_out[0].astype(BF16),
        norm2_g, wr_hi, wr_lo, br)

    w13 = jnp.concatenate([w1[0], w3[0]], axis=2).astype(BF16)
    w2f = w2[0].astype(BF16).reshape(N_EXPERTS * D_EXPERT, D_MODEL)
    out = _moe(gidx, h2, comb, x1, w13, w2f)
    return out.reshape(bsz, seq, d)
```

```python
import functools
import math

import jax
import jax.numpy as jnp
import numpy as np
from jax import lax
from jax.experimental import pallas as pl
from jax.experimental.pallas import tpu as pltpu

F32 = jnp.float32
BF16 = jnp.bfloat16

D_MODEL = 1024
EPS = 1e-6
A_GROUPS = 8
CHUNK = 128
B_HEADS = 8
B_HEAD_DIM = 64
B_V_DIM = 128
IN_COLS = 5 * D_MODEL
N_GROUPS = 4
EXPERTS_PER_GROUP = 8
N_EXPERTS = 32
D_EXPERT = 256
LAM_INIT = 0.8 - 0.6 * math.exp(-0.3 * 0)

LANES = 128
MXU_DIM = 256
VMEM_LIMIT_BYTES = 56 * 1024 * 1024

NEG_BIG = -1e30
LOG2_E = math.log2(math.e)
ALIBI_SLOPES = tuple(2.0 ** (-8.0 * (h + 1) / B_HEADS) for h in range(B_HEADS))
SLOPE_PARTS = 4
VT_ROWS = B_V_DIM + 16
SCORE_BOUND_MARGIN = 1.02
MAX_BOUNDED_SCORE = 48.0

PROJ_TM = 512
ATT_TQ_BOUNDED = 2048
ATT_TQ = 1024
ATT_TK = 512
MERGE_TM = 512
MERGE_SUBTILES = 1
MOE_TM = 1024
MOE_NE = 4
MOE_ROW_BLOCK = 128
ROUTER_ROWS = 48


def _bf16_parts(value, n_parts):
    parts, rest = [], np.float64(value)
    for _ in range(n_parts):
        part = np.float64(np.asarray(rest, np.float32).astype(BF16).astype(np.float32))
        parts.append(float(part))
        rest = rest - part
    return parts


def _segment_rms_scale(acc, seg_ones_ref, seg):
    sq = (acc * acc).astype(BF16)
    parts = []
    for p in range(acc.shape[1] // MXU_DIM):
        parts.append(jnp.dot(sq[:, p * MXU_DIM:(p + 1) * MXU_DIM], seg_ones_ref[...],
                             preferred_element_type=F32))
    ss = jnp.concatenate(parts, axis=1)
    return lax.rsqrt(ss * (1.0 / seg) + EPS)


def _inproj_kernel(x_ref, g1_ref, w_ref, bias_ref, gv_ref, gq_ref, gk_ref, ones64_ref, ones128_ref,
                   bound_ref,
                   u_ref, va_ref, q_ref, k_ref, vt_ref, ga_ref, gb_ref, *, seq_blocks):
    i = pl.program_id(0)
    tm = x_ref.shape[0]
    x = x_ref[...]
    r = lax.rsqrt(jnp.mean(x * x, axis=-1, keepdims=True) + EPS)
    h = ((x * r) * g1_ref[...]).astype(BF16)

    def slab(j):
        return jnp.dot(h, w_ref[:, j * D_MODEL:(j + 1) * D_MODEL], preferred_element_type=F32)

    lane = lax.broadcasted_iota(jnp.int32, (1, LANES), 1)
    lo_half = lane < B_HEAD_DIM

    u_ref[...] = slab(0).astype(BF16)

    acc = slab(1)
    r = _segment_rms_scale(acc, ones128_ref, LANES)
    va_ref[...] = ((acc * r) * gv_ref[...]).astype(BF16)

    spare_col = lane & (B_HEAD_DIM - 1)
    pos = (i % seq_blocks) * tm + lax.broadcasted_iota(jnp.int32, (tm, LANES), 0)
    pos_lo = (pos & (MXU_DIM - 1)).astype(F32)
    pos_hi = (pos - (pos & (MXU_DIM - 1))).astype(F32)
    pos_cols = jnp.where(spare_col & SLOPE_PARTS == 0, pos_lo, pos_hi)
    bound = bound_ref[...]
    bound_hi = bound.astype(BF16).astype(F32)
    bound_mid = (bound - bound_hi).astype(BF16).astype(F32)
    bound_lo = (bound - bound_hi - bound_mid).astype(BF16).astype(F32)
    bound_cols = jnp.where(spare_col == 4 * SLOPE_PARTS, bound_hi,
                           jnp.where(spare_col == 4 * SLOPE_PARTS + 1, bound_mid,
                                     jnp.where(spare_col == 4 * SLOPE_PARTS + 2, bound_lo, 0.0)))
    q_shared = jnp.where((spare_col >= 2 * SLOPE_PARTS) & (spare_col < 4 * SLOPE_PARTS),
                         pos_cols, bound_cols)
    k_shared = jnp.where(spare_col < 2 * SLOPE_PARTS, pos_cols, 0.0)

    def slope_cols(hd, sign):
        parts = _bf16_parts(ALIBI_SLOPES[hd] * LOG2_E, SLOPE_PARTS)
        cvec = jnp.zeros((1, LANES), F32)
        for n, part in enumerate(parts):
            cvec = jnp.where(spare_col & (SLOPE_PARTS - 1) == n, sign * part, cvec)
        return cvec

    acc = slab(2)
    r = _segment_rms_scale(acc, ones64_ref, B_HEAD_DIM)
    qn = (acc * r) * gq_ref[...]
    for hd in range(B_HEADS):
        extra = jnp.where(spare_col < 2 * SLOPE_PARTS, slope_cols(hd, 1.0), q_shared)
        blk = qn[:, hd * LANES:(hd + 1) * LANES]
        q_ref[0, 2 * hd] = jnp.where(lo_half, blk, extra).astype(BF16)
        q_ref[0, 2 * hd + 1] = jnp.where(lo_half, extra, blk).astype(BF16)

    acc = slab(3)
    r = _segment_rms_scale(acc, ones64_ref, B_HEAD_DIM)
    kn = (acc * r) * gk_ref[...]
    for hd in range(B_HEADS):
        consts = jnp.where(spare_col < 4 * SLOPE_PARTS, slope_cols(hd, -1.0),
                           jnp.where(spare_col < 4 * SLOPE_PARTS + 3, -1.0, 0.0))
        extra = jnp.where(spare_col < 2 * SLOPE_PARTS, k_shared, consts)
        blk = kn[:, hd * LANES:(hd + 1) * LANES]
        k_ref[0, 2 * hd] = jnp.where(lo_half, blk, extra).astype(BF16)
        k_ref[0, 2 * hd + 1] = jnp.where(lo_half, extra, blk).astype(BF16)

    acc = slab(4)
    for hd in range(B_HEADS):
        vt_ref[0, hd, 0:B_V_DIM, :] = acc[:, hd * LANES:(hd + 1) * LANES].T.astype(BF16)
        vt_ref[0, hd, B_V_DIM:VT_ROWS, :] = jnp.ones((VT_ROWS - B_V_DIM, tm), BF16)

    ga_ref[...] = (1.0 / (1.0 + jnp.exp(-(slab(5) + bias_ref[:, 0:D_MODEL])))).astype(BF16)
    gb_ref[...] = (1.0 / (1.0 + jnp.exp(-(slab(6) + bias_ref[:, D_MODEL:])))).astype(BF16)


def _input_projection(x2, g1, w_cat, b_gate, gv, gq, gk, ones64, ones128, bound, bsz, seq):
    n_tok = x2.shape[0]
    tm = PROJ_TM
    seq_blocks = seq // tm
    row_spec = pl.BlockSpec((tm, D_MODEL), lambda i: (i, 0))
    resident = lambda shape: pl.BlockSpec(shape, lambda i: (0,) * len(shape),
                                          pipeline_mode=pl.Buffered(1))
    head_map = lambda i: (i // seq_blocks, 0, i % seq_blocks, 0)
    tok_bf16 = jax.ShapeDtypeStruct((n_tok, D_MODEL), BF16)
    return pl.pallas_call(
        functools.partial(_inproj_kernel, seq_blocks=seq_blocks),
        grid=(n_tok // tm,),
        in_specs=[
            row_spec,
            resident((1, D_MODEL)),
            resident(w_cat.shape),
            resident(b_gate.shape),
            resident((1, D_MODEL)), resident((1, D_MODEL)), resident((1, D_MODEL)),
            resident((MXU_DIM, MXU_DIM)), resident((MXU_DIM, MXU_DIM)),
            resident((1, 1)),
        ],
        out_specs=[
            row_spec, row_spec,
            pl.BlockSpec((1, 2 * B_HEADS, tm, LANES), head_map),
            pl.BlockSpec((1, 2 * B_HEADS, tm, LANES), head_map),
            pl.BlockSpec((1, B_HEADS, VT_ROWS, tm), lambda i: (i // seq_blocks, 0, 0, i % seq_blocks)),
            row_spec, row_spec,
        ],
        out_shape=[
            tok_bf16, tok_bf16,
            jax.ShapeDtypeStruct((bsz, 2 * B_HEADS, seq, LANES), BF16),
            jax.ShapeDtypeStruct((bsz, 2 * B_HEADS, seq, LANES), BF16),
            jax.ShapeDtypeStruct((bsz, B_HEADS, VT_ROWS, seq), BF16),
            tok_bf16, tok_bf16,
        ],
        compiler_params=pltpu.CompilerParams(
            dimension_semantics=("arbitrary",),
            vmem_limit_bytes=VMEM_LIMIT_BYTES),
        name="input_projection",
    )(x2, g1, w_cat, b_gate, gv, gq, gk, ones64, ones128, bound)


def _attn_kernel(lam_ref, sg_ref, q_ref, k_ref, vt_ref, o_ref, acc_scr, sa_scr, sb_scr):
    qi = pl.program_id(2)
    tq = q_ref.shape[2]
    tk = ATT_TK
    nt_dims = (((1,), (1,)), ((), ()))

    assert tq == 2 * tk
    tri = (lax.broadcasted_iota(jnp.int32, (tk, tk), 0)
           <= lax.broadcasted_iota(jnp.int32, (tk, tk), 1))

    def scores(kb, s_scr, diag=None):
        ks = pl.multiple_of(kb * tk, tk)
        maxima = []
        for c in range(2):
            k_blk = k_ref[0, c, pl.ds(ks, tk), :]
            if diag is None:
                s = lax.dot_general(k_blk, q_ref[0, c], nt_dims, preferred_element_type=F32)
                s_scr[c] = s
                maxima.append(jnp.max(s, axis=0, keepdims=True))
                continue
            parts = []
            for half in range(diag, 2):
                cols = slice(half * tk, (half + 1) * tk)
                s = lax.dot_general(k_blk, q_ref[0, c, cols, :], nt_dims,
                                    preferred_element_type=F32)
                if half == diag:
                    s = jnp.where(tri, s, NEG_BIG)
                s_scr[c, :, cols] = s
                parts.append(jnp.max(s, axis=0, keepdims=True))
            maxima.append(parts[0] if len(parts) == 1 else jnp.concatenate(parts, axis=1))
        return tuple(maxima)

    def consume(kb, s_scr, m_old, blk_max, col0=0, first=False):
        ks = pl.multiple_of(kb * tk, tk)
        vt = vt_ref[0, 0, :, pl.ds(ks, tk)]
        m_new = []
        for c in range(2):
            m_c = blk_max[c] if first else jnp.maximum(m_old[c][:, col0:], blk_max[c])
            p = jnp.exp2(s_scr[c, :, col0:] - m_c).astype(BF16)
            pv = jnp.dot(vt, p, preferred_element_type=F32)
            cols = slice(c * tq + col0, (c + 1) * tq)
            if first:
                acc_scr[:, cols] = pv
            else:
                acc_scr[:, cols] = acc_scr[:, cols] * jnp.exp2(m_old[c][:, col0:] - m_c) + pv
            m_new.append(m_c)
        return tuple(m_new)

    max_a = scores(2 * qi, sa_scr, diag=0)
    max_b = scores(2 * qi + 1, sb_scr, diag=1)
    m = consume(2 * qi, sa_scr, None, max_a, first=True)
    max_a = scores(0, sa_scr)
    m_right = consume(2 * qi + 1, sb_scr, m, max_b, col0=tk)
    m = tuple(jnp.concatenate([m[c][:, :tk], m_right[c]], axis=1) for c in range(2))

    def trip(t, carry):
        m, max_a = carry
        max_b = scores(2 * t + 1, sb_scr)
        m = consume(2 * t, sa_scr, m, max_a)
        max_a = scores(2 * t + 2, sa_scr)
        m = consume(2 * t + 1, sb_scr, m, max_b)
        return m, max_a

    m, max_a = lax.fori_loop(0, qi - 1, trip, (m, max_a))

    @pl.when(qi > 0)
    def _():
        max_b = scores(2 * qi - 1, sb_scr)
        m_mid = consume(2 * qi - 2, sa_scr, m, max_a)
        consume(2 * qi - 1, sb_scr, m_mid, max_b)

    _attn_finalize(lam_ref, sg_ref, acc_scr, o_ref)


def _attn_finalize(lam_ref, sg_ref, acc_scr, o_ref):
    tq = o_ref.shape[1]
    lam_v = lam_ref[...]
    lam = (jnp.exp(jnp.sum(lam_v[0:1] * lam_v[1:2], axis=1, keepdims=True))
           - jnp.exp(jnp.sum(lam_v[2:3] * lam_v[3:4], axis=1, keepdims=True)) + LAM_INIT)
    o0 = acc_scr[0:B_V_DIM, 0:tq] / acc_scr[B_V_DIM:B_V_DIM + 1, 0:tq]
    o1 = acc_scr[0:B_V_DIM, tq:2 * tq] / acc_scr[B_V_DIM:B_V_DIM + 1, tq:2 * tq]
    o = o0 - lam * o1
    r = lax.rsqrt(jnp.mean(o * o, axis=0, keepdims=True) + EPS)
    o = ((o * r) * sg_ref[...]) * (1.0 - LAM_INIT)
    o_ref[0] = o.T.astype(BF16)


def _attn_bounded_kernel(lam_ref, sg_ref, q_ref, k_ref, vt_ref, o_ref, acc_scr):
    qi = pl.program_id(2)
    tq = q_ref.shape[2]
    tk = ATT_TK
    nt_dims = (((1,), (1,)), ((), ()))
    tri = (lax.broadcasted_iota(jnp.int32, (tk, tk), 0)
           <= lax.broadcasted_iota(jnp.int32, (tk, tk), 1))

    def fold(key_start, n_keys, q_cols, masked, first):
        vt = vt_ref[0, 0, 0:B_V_DIM, pl.ds(key_start, n_keys)]
        for c in range(2):
            e = lax.dot_general(k_ref[0, c, pl.ds(key_start, n_keys), :], q_ref[0, c, q_cols, :],
                                nt_dims, preferred_element_type=F32)
            if masked:
                e_diag = jnp.where(tri, e[n_keys - tk:, :], NEG_BIG)
                e = e_diag if n_keys == tk else jnp.concatenate([e[:n_keys - tk, :], e_diag], axis=0)
            p = jnp.exp2(e)
            p_sum = jnp.sum(p, axis=0, keepdims=True)
            pv = jnp.dot(vt, p.astype(BF16), preferred_element_type=F32)
            cols = slice(c * tq + q_cols.start, c * tq + q_cols.stop)
            if first:
                acc_scr[0:B_V_DIM, cols] = pv
                acc_scr[B_V_DIM:B_V_DIM + 1, cols] = p_sum
            else:
                acc_scr[0:B_V_DIM, cols] += pv
                acc_scr[B_V_DIM:B_V_DIM + 1, cols] += p_sum

    strips = [slice(j * tk, (j + 1) * tk) for j in range(tq // tk)]
    diag0 = pl.multiple_of(qi * tq, tq)
    for j, strip in enumerate(strips):
        fold(diag0, (j + 1) * tk, strip, True, True)

    def trip(t, carry):
        for strip in strips:
            fold(pl.multiple_of(t * tq, tq), tq, strip, False, False)
        return carry

    lax.fori_loop(0, qi, trip, 0)
    _attn_finalize(lam_ref, sg_ref, acc_scr, o_ref)


def _diff_attention(lam_vecs, sub_g, q_arr, k_arr, vt_arr, bounded):
    bsz, _, seq, _ = q_arr.shape
    tq = ATT_TQ_BOUNDED if bounded else ATT_TQ
    scratch = [pltpu.VMEM((VT_ROWS, 2 * tq), F32)]
    if not bounded:
        scratch += [pltpu.VMEM((2, ATT_TK, tq), F32), pltpu.VMEM((2, ATT_TK, tq), F32)]
    return pl.pallas_call(
        _attn_bounded_kernel if bounded else _attn_kernel,
        grid=(bsz, B_HEADS, seq // tq),
        in_specs=[
            pl.BlockSpec((4, B_HEAD_DIM), lambda b, h, qi: (0, 0)),
            pl.BlockSpec((B_V_DIM, 1), lambda b, h, qi: (0, 0)),
            pl.BlockSpec((1, 2, tq, LANES), lambda b, h, qi: (b, h, qi, 0)),
            pl.BlockSpec((1, 2, seq, LANES), lambda b, h, qi: (b, h, 0, 0)),
            pl.BlockSpec((1, 1, VT_ROWS, seq), lambda b, h, qi: (b, h, 0, 0)),
        ],
        out_specs=pl.BlockSpec((1, tq, B_V_DIM), lambda b, h, qi: (b, qi, h)),
        out_shape=jax.ShapeDtypeStruct((bsz, seq, B_HEADS * B_V_DIM), BF16),
        scratch_shapes=scratch,
        compiler_params=pltpu.CompilerParams(
            dimension_semantics=("arbitrary", "arbitrary", "arbitrary"),
            vmem_limit_bytes=VMEM_LIMIT_BYTES),
        name="diff_attention_bounded" if bounded else "diff_attention",
    )(lam_vecs, sub_g, q_arr, k_arr, vt_arr)


def _first_argmax(vals, iota, top):
    return jnp.min(jnp.where(vals == top, iota, vals.shape[0]), axis=0, keepdims=True)


def _merge_kernel(u_ref, va_ref, yb_ref, ga_ref, gb_ref, x_ref, ws_ref, bs_ref,
                  wua_ref, wub_ref, wo_ref, g2_ref, wr_hi_ref, wr_lo_ref, br_ref,
                  x1_ref, h2_ref, comb_ref, gidx_ref):
    tm = x_ref.shape[0]
    row = lax.broadcasted_iota(jnp.int32, (CHUNK, CHUNK), 0)
    col = lax.broadcasted_iota(jnp.int32, (CHUNK, CHUNK), 1)
    causal = row >= col

    w_tril = [jnp.where(causal, ws_ref[g], 0.0).astype(BF16) for g in range(A_GROUPS)]
    for sub in range(MERGE_SUBTILES):
        rows = slice(sub * (tm // MERGE_SUBTILES), (sub + 1) * (tm // MERGE_SUBTILES))
        _merge_rows(rows, w_tril, u_ref, va_ref, yb_ref, ga_ref, gb_ref, x_ref, bs_ref,
                    wua_ref, wub_ref, wo_ref, g2_ref, wr_hi_ref, wr_lo_ref, br_ref,
                    x1_ref, h2_ref, comb_ref, gidx_ref)


def _merge_rows(rows, w_tril, u_ref, va_ref, yb_ref, ga_ref, gb_ref, x_ref, bs_ref,
                wua_ref, wub_ref, wo_ref, g2_ref, wr_hi_ref, wr_lo_ref, br_ref,
                x1_ref, h2_ref, comb_ref, gidx_ref):
    tm = rows.stop - rows.start
    n_chunks = tm // CHUNK
    group_cols = []
    for g in range(A_GROUPS):
        gl = slice(g * LANES, (g + 1) * LANES)
        v_g = jnp.concatenate(
            [va_ref[rows.start + c * CHUNK:rows.start + (c + 1) * CHUNK, gl] for c in range(n_chunks)],
            axis=1)
        z = jnp.dot(w_tril[g], v_g, preferred_element_type=F32) + bs_ref[g]
        z = jnp.concatenate([z[:, c * LANES:(c + 1) * LANES] for c in range(n_chunks)], axis=0)
        group_cols.append((u_ref[rows, gl].astype(F32) * z).astype(BF16))
    y_a = jnp.concatenate(group_cols, axis=1)

    up_a = jnp.dot(y_a, wua_ref[...], preferred_element_type=F32)
    up_b = jnp.dot(yb_ref[rows, :], wub_ref[...], preferred_element_type=F32)
    merged = ga_ref[rows, :].astype(F32) * up_a + gb_ref[rows, :].astype(F32) * up_b
    x1 = x_ref[rows, :] + jnp.dot(merged.astype(BF16), wo_ref[...], preferred_element_type=F32)
    x1_ref[rows, :] = x1

    r = lax.rsqrt(jnp.mean(x1 * x1, axis=-1, keepdims=True) + EPS)
    h2 = (x1 * r) * g2_ref[...]
    h2_hi = h2.astype(BF16)
    h2_ref[rows, :] = h2_hi
    h2_lo = (h2 - h2_hi.astype(F32)).astype(BF16)

    nt_dims = (((1,), (1,)), ((), ()))
    lt = (lax.dot_general(wr_hi_ref[...], h2_hi, nt_dims, preferred_element_type=F32)
          + lax.dot_general(wr_hi_ref[...], h2_lo, nt_dims, preferred_element_type=F32)
          + lax.dot_general(wr_lo_ref[...], h2_hi, nt_dims, preferred_element_type=F32)
          + br_ref[...])

    iota8 = lax.broadcasted_iota(jnp.int32, (8, tm), 0)
    gl = lt[0:8]
    ge = jnp.exp(gl - jnp.max(gl, axis=0, keepdims=True))
    gp = ge / jnp.sum(ge, axis=0, keepdims=True)
    g_gate = jnp.max(gp, axis=0, keepdims=True)
    g_idx = _first_argmax(gp, iota8, g_gate)

    el = jnp.zeros((EXPERTS_PER_GROUP, tm), F32)
    for gi in range(N_GROUPS):
        el = jnp.where(g_idx == gi, lt[8 + 8 * gi:16 + 8 * gi], el)
    ee = jnp.exp(el - jnp.max(el, axis=0, keepdims=True))
    ep = ee / jnp.sum(ee, axis=0, keepdims=True)
    p_top1 = jnp.max(ep, axis=0, keepdims=True)
    i_top1 = _first_argmax(ep, iota8, p_top1)
    ep_rest = jnp.where(iota8 == i_top1, -1.0, ep)
    p_top2 = jnp.max(ep_rest, axis=0, keepdims=True)
    i_top2 = _first_argmax(ep_rest, iota8, p_top2)
    den = p_top1 + p_top2
    w_top1 = g_gate * (p_top1 / den)
    w_top2 = g_gate * (p_top2 / den)
    in_group = jnp.where(iota8 == i_top1, w_top1, 0.0) + jnp.where(iota8 == i_top2, w_top2, 0.0)
    blocks = [jnp.where(g_idx == gi, in_group, 0.0) for gi in range(N_GROUPS)]
    blocks.append(jnp.zeros((LANES - N_EXPERTS, tm), F32))
    comb_t = jnp.concatenate(blocks, axis=0)
    comb_ref[rows, :] = comb_t.T
    gidx_ref[:, rows] = jnp.broadcast_to(g_idx, (gidx_ref.shape[0], tm))


def _merge_and_route(u, va, yb, ga, gb, x2, ws, bs, wua, wub, wo, g2, wr_hi, wr_lo, br):
    n_tok = x2.shape[0]
    tm = MERGE_TM
    row_spec = pl.BlockSpec((tm, D_MODEL), lambda i: (i, 0))
    full = lambda shape: pl.BlockSpec(shape, lambda i: (0,) * len(shape))
    return pl.pallas_call(
        _merge_kernel,
        grid=(n_tok // tm,),
        in_specs=[
            row_spec, row_spec, row_spec, row_spec, row_spec, row_spec,
            full((A_GROUPS, CHUNK, CHUNK)), full((A_GROUPS, CHUNK, 1)),
            full((D_MODEL, D_MODEL)), full((D_MODEL, D_MODEL)), full((D_MODEL, D_MODEL)),
            full((1, D_MODEL)),
            full((ROUTER_ROWS, D_MODEL)), full((ROUTER_ROWS, D_MODEL)), full((ROUTER_ROWS, 1)),
        ],
        out_specs=[row_spec, row_spec, pl.BlockSpec((tm, LANES), lambda i: (i, 0)),
                   pl.BlockSpec((8, tm), lambda i: (0, i))],
        out_shape=[
            jax.ShapeDtypeStruct((n_tok, D_MODEL), F32),
            jax.ShapeDtypeStruct((n_tok, D_MODEL), BF16),
            jax.ShapeDtypeStruct((n_tok, LANES), F32),
            jax.ShapeDtypeStruct((8, n_tok), jnp.int32),
        ],
        compiler_params=pltpu.CompilerParams(
            dimension_semantics=("arbitrary",),
            vmem_limit_bytes=VMEM_LIMIT_BYTES),
        name="merge_and_route",
    )(u, va, yb, ga, gb, x2, ws, bs, wua, wub, wo, g2, wr_hi, wr_lo, br)


def _moe_kernel(g_ref, h_ref, c_ref, x1_ref, w13_ref, w2_ref, o_ref,
                p_scr, xg_scr, cg_scr, y_scr, bounds_ref):
    e = pl.program_id(1)
    n_steps = pl.num_programs(1)
    tt = h_ref.shape[0]
    rows = p_scr.shape[0]
    rb = MOE_ROW_BLOCK

    @pl.when(e == 0)
    def _():
        g_idx = g_ref[0:1, :]
        slot = lax.broadcasted_iota(jnp.int32, (16, tt), 0)
        onehot = jnp.where(slot == g_idx, 1.0, 0.0)
        upper = (lax.broadcasted_iota(jnp.int32, (LANES, LANES), 0)
                 <= lax.broadcasted_iota(jnp.int32, (LANES, LANES), 1)).astype(BF16)
        carry = jnp.zeros((16, 1), F32)
        counts = []
        for j in range(tt // LANES):
            seg = onehot[:, j * LANES:(j + 1) * LANES].astype(BF16)
            cs = jnp.dot(seg, upper, preferred_element_type=F32) + carry
            counts.append(cs)
            carry = cs[:, LANES - 1:LANES]
        running = jnp.concatenate(counts, axis=1)
        n_blocks = jnp.floor((carry + (rb - 1)) * (1.0 / rb))
        slot_col = lax.broadcasted_iota(jnp.int32, (16, 1), 0)
        first_block = jnp.zeros((16, 1), F32)
        for g in range(N_GROUPS - 1):
            first_block = first_block + jnp.where(slot_col > g, n_blocks[g:g + 1, :], 0.0)
        end_block = first_block + n_blocks
        for g in range(N_GROUPS):
            bounds_ref[g] = first_block[g:g + 1, :][0, 0].astype(jnp.int32)
            bounds_ref[N_GROUPS + g] = end_block[g:g + 1, :][0, 0].astype(jnp.int32)
        pos = jnp.sum(onehot * (first_block * rb + running - 1.0), axis=0, keepdims=True)
        pos = pos.astype(jnp.int32)

        c = c_ref[...]
        c_hi = c.astype(BF16)
        c_rest = c - c_hi.astype(F32)
        c_mid = c_rest.astype(BF16)
        c_lo = (c_rest - c_mid.astype(F32)).astype(BF16)
        h_ext = jnp.concatenate([h_ref[...], c_hi, c_mid, c_lo], axis=1)
        for ch in range(rows // MXU_DIM):
            sl = slice(ch * MXU_DIM, (ch + 1) * MXU_DIM)
            row_id = lax.broadcasted_iota(jnp.int32, (MXU_DIM, tt), 0) + ch * MXU_DIM
            perm = jnp.where(row_id == pos, 1.0, 0.0).astype(BF16)
            p_scr[sl, :] = perm
            got = jnp.dot(perm, h_ext, preferred_element_type=F32)
            xg_scr[sl, :] = got[:, :D_MODEL].astype(BF16)
            cg_scr[sl, :] = (got[:, D_MODEL:D_MODEL + LANES]
                             + got[:, D_MODEL + LANES:D_MODEL + 2 * LANES]
                             + got[:, D_MODEL + 2 * LANES:])
        y_scr[...] = jnp.zeros_like(y_scr)

    group = e // (EXPERTS_PER_GROUP // MOE_NE)
    to_front = (LANES - e * MOE_NE) & (LANES - 1)

    def row_block(b):
        r0 = pl.multiple_of(b * rb, rb)
        xb = xg_scr[pl.ds(r0, rb), :]
        cb = pltpu.roll(cg_scr[pl.ds(r0, rb), :], to_front, axis=1)
        hids = []
        for n in range(MOE_NE):
            ab = jnp.dot(xb, w13_ref[n], preferred_element_type=F32)
            a = ab[:, :D_EXPERT]
            b3 = ab[:, D_EXPERT:]
            hid = (a * (1.0 / (1.0 + jnp.exp(-a)))) * b3
            hids.append((hid * cb[:, n:n + 1]).astype(BF16))
        hid_all = jnp.concatenate(hids, axis=1)
        y_scr[pl.ds(r0, rb), :] += jnp.dot(hid_all, w2_ref[...], preferred_element_type=F32)

    first = bounds_ref[group]
    n_blocks = bounds_ref[N_GROUPS + group] - first

    def pair(i, carry):
        row_block(first + 2 * i)
        row_block(first + 2 * i + 1)
        return carry

    lax.fori_loop(0, n_blocks // 2, pair, 0)

    @pl.when(n_blocks % 2 == 1)
    def _():
        row_block(first + n_blocks - 1)

    @pl.when(e == n_steps - 1)
    def _():
        tn_dims = (((0,), (0,)), ((), ()))
        y = lax.dot_general(p_scr[...], y_scr[...].astype(BF16), tn_dims,
                            preferred_element_type=F32)
        o_ref[...] = x1_ref[...] + y


def _moe(gidx, h2, comb, x1, w13, w2f):
    n_tok = h2.shape[0]
    tt = MOE_TM
    rows = tt + N_GROUPS * MOE_ROW_BLOCK
    return pl.pallas_call(
        _moe_kernel,
        grid=(n_tok // tt, N_EXPERTS // MOE_NE),
        in_specs=[
            pl.BlockSpec((8, tt), lambda i, e: (0, i)),
            pl.BlockSpec((tt, D_MODEL), lambda i, e: (i, 0)),
            pl.BlockSpec((tt, LANES), lambda i, e: (i, 0)),
            pl.BlockSpec((tt, D_MODEL), lambda i, e: (i, 0)),
            pl.BlockSpec((MOE_NE, D_MODEL, 2 * D_EXPERT), lambda i, e: (e, 0, 0)),
            pl.BlockSpec((MOE_NE * D_EXPERT, D_MODEL), lambda i, e: (e, 0)),
        ],
        out_specs=pl.BlockSpec((tt, D_MODEL), lambda i, e: (i, 0)),
        out_shape=jax.ShapeDtypeStruct((n_tok, D_MODEL), F32),
        scratch_shapes=[
            pltpu.VMEM((rows, tt), BF16),
            pltpu.VMEM((rows, D_MODEL), BF16),
            pltpu.VMEM((rows, LANES), F32),
            pltpu.VMEM((rows, D_MODEL), F32),
            pltpu.SMEM((2 * N_GROUPS,), jnp.int32),
        ],
        compiler_params=pltpu.CompilerParams(
            dimension_semantics=("arbitrary", "arbitrary"),
            vmem_limit_bytes=VMEM_LIMIT_BYTES),
        name="moe_experts",
    )(gidx, h2, comb, x1, w13, w2f)


def _segment_ones(seg):
    idx = jnp.arange(MXU_DIM) // seg
    return (idx[:, None] == idx[None, :]).astype(BF16)


def kernel(x, norm1_g, w_in, v_norm_g, w_s, b_s, q_norm_g, k_norm_g, lambda_q1, lambda_k1, lambda_q2, lambda_k2, sub_norm_g, w_up_a, w_up_b, w_gate, b_gate, w_out, norm2_g, w_rg, b_rg, w_re, b_re, w1, w3, w2):
    bsz, seq, d = x.shape
    assert d == D_MODEL and seq % PROJ_TM == 0 and seq % ATT_TQ == 0 and ATT_TQ == 2 * ATT_TK
    assert seq % ATT_TQ_BOUNDED == 0 and ATT_TQ_BOUNDED % ATT_TK == 0
    assert norm1_g.shape[0] == 1, "single layer"
    n_tok = bsz * seq
    assert n_tok % MOE_TM == 0 and n_tok % MERGE_TM == 0
    x2 = x.reshape(n_tok, d)

    w_cat = jnp.concatenate([w_in[0], w_gate[0]], axis=1).astype(BF16)
    gv = v_norm_g[0].reshape(1, D_MODEL)
    gq = jnp.tile(q_norm_g[0] * (B_HEAD_DIM ** -0.5 * LOG2_E), 2 * B_HEADS)[None, :]
    gk = jnp.tile(k_norm_g[0], 2 * B_HEADS)[None, :]
    score_bound = (SCORE_BOUND_MARGIN * B_HEAD_DIM ** 0.5 * LOG2_E
                   * jnp.max(jnp.abs(q_norm_g[0] * k_norm_g[0]))).reshape(1, 1)
    u, va, q_arr, k_arr, vt_arr, ga, gb = _input_projection(
        x2, norm1_g, w_cat, b_gate, gv, gq, gk, _segment_ones(B_HEAD_DIM), _segment_ones(LANES),
        score_bound, bsz, seq)

    lam_vecs = jnp.concatenate([lambda_q1, lambda_k1, lambda_q2, lambda_k2], axis=0)
    attn_args = (lam_vecs, sub_norm_g[0][:, None], q_arr, k_arr, vt_arr)
    yb = lax.cond(score_bound[0, 0] <= MAX_BOUNDED_SCORE,
                  lambda args: _diff_attention(*args, bounded=True),
                  lambda args: _diff_attention(*args, bounded=False), attn_args)
    yb = yb.reshape(n_tok, D_MODEL)

    pad_rows = ROUTER_ROWS - 8 - N_EXPERTS
    wr = jnp.concatenate([w_rg[0].T, jnp.zeros((4, d), F32), w_re[0].T,
                          jnp.zeros((pad_rows, d), F32)], axis=0)
    br = jnp.concatenate([b_rg[0], jnp.full((4,), NEG_BIG, F32), b_re[0],
                          jnp.zeros((pad_rows,), F32)])[:, None]
    wr_hi = wr.astype(BF16)
    wr_lo = (wr - wr_hi.astype(F32)).astype(BF16)
    x1, h2, comb, gidx = _merge_and_route(
        u, va, yb, ga, gb, x2, w_s[0], b_s[0][:, :, None],
        w_up_a[0].astype(BF16), w_up_b[0].astype(BF16), w_out[0].astype(BF16),
        norm2_g, wr_hi, wr_lo, br)

    w13 = jnp.concatenate([w1[0], w3[0]], axis=2).astype(BF16)
    w2f = w2[0].astype(BF16).reshape(N_EXPERTS * D_EXPERT, D_MODEL)
    out = _moe(gidx, h2, comb, x1, w13, w2f)
    return out.reshape(bsz, seq, d)
```

```python
import functools
import math

import jax
import jax.numpy as jnp
import numpy as np
from jax import lax
from jax.experimental import pallas as pl
from jax.experimental.pallas import tpu as pltpu

F32 = jnp.float32
BF16 = jnp.bfloat16

D_MODEL = 1024
EPS = 1e-6
A_GROUPS = 8
CHUNK = 128
B_HEADS = 8
B_HEAD_DIM = 64
B_V_DIM = 128
IN_COLS = 5 * D_MODEL
N_GROUPS = 4
EXPERTS_PER_GROUP = 8
N_EXPERTS = 32
D_EXPERT = 256
LAM_INIT = 0.8 - 0.6 * math.exp(-0.3 * 0)

LANES = 128
MXU_DIM = 256
VMEM_LIMIT_BYTES = 56 * 1024 * 1024
MOE_VMEM_LIMIT_BYTES = 63 * 1024 * 1024

NEG_BIG = -1e30
LOG2_E = math.log2(math.e)
ALIBI_SLOPES = tuple(2.0 ** (-8.0 * (h + 1) / B_HEADS) for h in range(B_HEADS))
SLOPE_PARTS = 4
VT_ROWS = B_V_DIM + 16
SCORE_BOUND_MARGIN = 1.02
MAX_BOUNDED_SCORE = 48.0

PROJ_TM = 512
ATT_TQ_BOUNDED = 2048
ATT_TQ = 1024
ATT_TK = 512
MERGE_TM = 512
MOE_TM = 1024
MOE_NE = 8
MOE_ROW_BLOCK = 128
ROUTER_ROWS = 48


def _bf16_parts(value, n_parts):
    parts, rest = [], np.float64(value)
    for _ in range(n_parts):
        part = np.float64(np.asarray(rest, np.float32).astype(BF16).astype(np.float32))
        parts.append(float(part))
        rest = rest - part
    return parts


def _segment_rms_scale(acc, seg_ones_ref, seg):
    sq = (acc * acc).astype(BF16)
    parts = []
    for p in range(acc.shape[1] // MXU_DIM):
        parts.append(jnp.dot(sq[:, p * MXU_DIM:(p + 1) * MXU_DIM], seg_ones_ref[...],
                             preferred_element_type=F32))
    ss = jnp.concatenate(parts, axis=1)
    return lax.rsqrt(ss * (1.0 / seg) + EPS)


def _inproj_kernel(x_ref, g1_ref, w_ref, bias_ref, gv_ref, gq_ref, gk_ref, ones64_ref, ones128_ref,
                   bound_ref,
                   u_ref, va_ref, q_ref, k_ref, vt_ref, ga_ref, gb_ref, *, seq_blocks):
    i = pl.program_id(0)
    tm = x_ref.shape[0]
    x = x_ref[...]
    r = lax.rsqrt(jnp.mean(x * x, axis=-1, keepdims=True) + EPS)
    h = ((x * r) * g1_ref[...]).astype(BF16)

    def slab(j):
        return jnp.dot(h, w_ref[:, j * D_MODEL:(j + 1) * D_MODEL], preferred_element_type=F32)

    lane = lax.broadcasted_iota(jnp.int32, (1, LANES), 1)
    lo_half = lane < B_HEAD_DIM

    u_ref[...] = slab(0).astype(BF16)

    acc = slab(1)
    r = _segment_rms_scale(acc, ones128_ref, LANES)
    va_ref[...] = ((acc * r) * gv_ref[...]).astype(BF16)

    spare_col = lane & (B_HEAD_DIM - 1)
    pos = (i % seq_blocks) * tm + lax.broadcasted_iota(jnp.int32, (tm, LANES), 0)
    pos_lo = (pos & (MXU_DIM - 1)).astype(F32)
    pos_hi = (pos - (pos & (MXU_DIM - 1))).astype(F32)
    pos_cols = jnp.where(spare_col & SLOPE_PARTS == 0, pos_lo, pos_hi)
    bound = bound_ref[...]
    bound_hi = bound.astype(BF16).astype(F32)
    bound_mid = (bound - bound_hi).astype(BF16).astype(F32)
    bound_lo = (bound - bound_hi - bound_mid).astype(BF16).astype(F32)
    bound_cols = jnp.where(spare_col == 4 * SLOPE_PARTS, bound_hi,
                           jnp.where(spare_col == 4 * SLOPE_PARTS + 1, bound_mid,
                                     jnp.where(spare_col == 4 * SLOPE_PARTS + 2, bound_lo, 0.0)))
    q_shared = jnp.where((spare_col >= 2 * SLOPE_PARTS) & (spare_col < 4 * SLOPE_PARTS),
                         pos_cols, bound_cols)
    k_shared = jnp.where(spare_col < 2 * SLOPE_PARTS, pos_cols, 0.0)

    def slope_cols(hd, sign):
        parts = _bf16_parts(ALIBI_SLOPES[hd] * LOG2_E, SLOPE_PARTS)
        cvec = jnp.zeros((1, LANES), F32)
        for n, part in enumerate(parts):
            cvec = jnp.where(spare_col & (SLOPE_PARTS - 1) == n, sign * part, cvec)
        return cvec

    acc = slab(2)
    r = _segment_rms_scale(acc, ones64_ref, B_HEAD_DIM)
    qn = (acc * r) * gq_ref[...]
    for hd in range(B_HEADS):
        extra = jnp.where(spare_col < 2 * SLOPE_PARTS, slope_cols(hd, 1.0), q_shared)
        blk = qn[:, hd * LANES:(hd + 1) * LANES]
        q_ref[0, 2 * hd] = jnp.where(lo_half, blk, extra).astype(BF16)
        q_ref[0, 2 * hd + 1] = jnp.where(lo_half, extra, blk).astype(BF16)

    acc = slab(3)
    r = _segment_rms_scale(acc, ones64_ref, B_HEAD_DIM)
    kn = (acc * r) * gk_ref[...]
    for hd in range(B_HEADS):
        consts = jnp.where(spare_col < 4 * SLOPE_PARTS, slope_cols(hd, -1.0),
                           jnp.where(spare_col < 4 * SLOPE_PARTS + 3, -1.0, 0.0))
        extra = jnp.where(spare_col < 2 * SLOPE_PARTS, k_shared, consts)
        blk = kn[:, hd * LANES:(hd + 1) * LANES]
        k_ref[0, 2 * hd] = jnp.where(lo_half, blk, extra).astype(BF16)
        k_ref[0, 2 * hd + 1] = jnp.where(lo_half, extra, blk).astype(BF16)

    acc = slab(4)
    for hd in range(B_HEADS):
        vt_ref[0, hd, 0:B_V_DIM, :] = acc[:, hd * LANES:(hd + 1) * LANES].T.astype(BF16)
        vt_ref[0, hd, B_V_DIM:VT_ROWS, :] = jnp.ones((VT_ROWS - B_V_DIM, tm), BF16)

    ga_ref[...] = (1.0 / (1.0 + jnp.exp(-(slab(5) + bias_ref[:, 0:D_MODEL])))).astype(BF16)
    gb_ref[...] = (1.0 / (1.0 + jnp.exp(-(slab(6) + bias_ref[:, D_MODEL:])))).astype(BF16)


def _input_projection(x2, g1, w_cat, b_gate, gv, gq, gk, ones64, ones128, bound, bsz, seq):
    n_tok = x2.shape[0]
    tm = PROJ_TM
    seq_blocks = seq // tm
    row_spec = pl.BlockSpec((tm, D_MODEL), lambda i: (i, 0))
    resident = lambda shape: pl.BlockSpec(shape, lambda i: (0,) * len(shape),
                                          pipeline_mode=pl.Buffered(1))
    head_map = lambda i: (i // seq_blocks, 0, i % seq_blocks, 0)
    tok_bf16 = jax.ShapeDtypeStruct((n_tok, D_MODEL), BF16)
    return pl.pallas_call(
        functools.partial(_inproj_kernel, seq_blocks=seq_blocks),
        grid=(n_tok // tm,),
        in_specs=[
            row_spec,
            resident((1, D_MODEL)),
            resident(w_cat.shape),
            resident(b_gate.shape),
            resident((1, D_MODEL)), resident((1, D_MODEL)), resident((1, D_MODEL)),
            resident((MXU_DIM, MXU_DIM)), resident((MXU_DIM, MXU_DIM)),
            resident((1, 1)),
        ],
        out_specs=[
            row_spec, row_spec,
            pl.BlockSpec((1, 2 * B_HEADS, tm, LANES), head_map),
            pl.BlockSpec((1, 2 * B_HEADS, tm, LANES), head_map),
            pl.BlockSpec((1, B_HEADS, VT_ROWS, tm), lambda i: (i // seq_blocks, 0, 0, i % seq_blocks)),
            row_spec, row_spec,
        ],
        out_shape=[
            tok_bf16, tok_bf16,
            jax.ShapeDtypeStruct((bsz, 2 * B_HEADS, seq, LANES), BF16),
            jax.ShapeDtypeStruct((bsz, 2 * B_HEADS, seq, LANES), BF16),
            jax.ShapeDtypeStruct((bsz, B_HEADS, VT_ROWS, seq), BF16),
            tok_bf16, tok_bf16,
        ],
        compiler_params=pltpu.CompilerParams(
            dimension_semantics=("arbitrary",),
            vmem_limit_bytes=VMEM_LIMIT_BYTES),
        name="input_projection",
    )(x2, g1, w_cat, b_gate, gv, gq, gk, ones64, ones128, bound)


def _attn_kernel(lam_ref, sg_ref, q_ref, k_ref, vt_ref, o_ref, acc_scr, sa_scr, sb_scr):
    qi = pl.program_id(2)
    tq = q_ref.shape[2]
    tk = ATT_TK
    nt_dims = (((1,), (1,)), ((), ()))

    assert tq == 2 * tk
    tri = (lax.broadcasted_iota(jnp.int32, (tk, tk), 0)
           <= lax.broadcasted_iota(jnp.int32, (tk, tk), 1))

    def scores(kb, s_scr, diag=None):
        ks = pl.multiple_of(kb * tk, tk)
        maxima = []
        for c in range(2):
            k_blk = k_ref[0, c, pl.ds(ks, tk), :]
            if diag is None:
                s = lax.dot_general(k_blk, q_ref[0, c], nt_dims, preferred_element_type=F32)
                s_scr[c] = s
                maxima.append(jnp.max(s, axis=0, keepdims=True))
                continue
            parts = []
            for half in range(diag, 2):
                cols = slice(half * tk, (half + 1) * tk)
                s = lax.dot_general(k_blk, q_ref[0, c, cols, :], nt_dims,
                                    preferred_element_type=F32)
                if half == diag:
                    s = jnp.where(tri, s, NEG_BIG)
                s_scr[c, :, cols] = s
                parts.append(jnp.max(s, axis=0, keepdims=True))
            maxima.append(parts[0] if len(parts) == 1 else jnp.concatenate(parts, axis=1))
        return tuple(maxima)

    def consume(kb, s_scr, m_old, blk_max, col0=0, first=False):
        ks = pl.multiple_of(kb * tk, tk)
        vt = vt_ref[0, 0, :, pl.ds(ks, tk)]
        m_new = []
        for c in range(2):
            m_c = blk_max[c] if first else jnp.maximum(m_old[c][:, col0:], blk_max[c])
            p = jnp.exp2(s_scr[c, :, col0:] - m_c).astype(BF16)
            pv = jnp.dot(vt, p, preferred_element_type=F32)
            cols = slice(c * tq + col0, (c + 1) * tq)
            if first:
                acc_scr[:, cols] = pv
            else:
                acc_scr[:, cols] = acc_scr[:, cols] * jnp.exp2(m_old[c][:, col0:] - m_c) + pv
            m_new.append(m_c)
        return tuple(m_new)

    max_a = scores(2 * qi, sa_scr, diag=0)
    max_b = scores(2 * qi + 1, sb_scr, diag=1)
    m = consume(2 * qi, sa_scr, None, max_a, first=True)
    max_a = scores(0, sa_scr)
    m_right = consume(2 * qi + 1, sb_scr, m, max_b, col0=tk)
    m = tuple(jnp.concatenate([m[c][:, :tk], m_right[c]], axis=1) for c in range(2))

    def trip(t, carry):
        m, max_a = carry
        max_b = scores(2 * t + 1, sb_scr)
        m = consume(2 * t, sa_scr, m, max_a)
        max_a = scores(2 * t + 2, sa_scr)
        m = consume(2 * t + 1, sb_scr, m, max_b)
        return m, max_a

    m, max_a = lax.fori_loop(0, qi - 1, trip, (m, max_a))

    @pl.when(qi > 0)
    def _():
        max_b = scores(2 * qi - 1, sb_scr)
        m_mid = consume(2 * qi - 2, sa_scr, m, max_a)
        consume(2 * qi - 1, sb_scr, m_mid, max_b)

    _attn_finalize(lam_ref, sg_ref, acc_scr, o_ref)


def _attn_finalize(lam_ref, sg_ref, acc_scr, o_ref):
    tq = o_ref.shape[1]
    lam_v = lam_ref[...]
    lam = (jnp.exp(jnp.sum(lam_v[0:1] * lam_v[1:2], axis=1, keepdims=True))
           - jnp.exp(jnp.sum(lam_v[2:3] * lam_v[3:4], axis=1, keepdims=True)) + LAM_INIT)
    o0 = acc_scr[0:B_V_DIM, 0:tq] / acc_scr[B_V_DIM:B_V_DIM + 1, 0:tq]
    o1 = acc_scr[0:B_V_DIM, tq:2 * tq] / acc_scr[B_V_DIM:B_V_DIM + 1, tq:2 * tq]
    o = o0 - lam * o1
    r = lax.rsqrt(jnp.mean(o * o, axis=0, keepdims=True) + EPS)
    o = ((o * r) * sg_ref[...]) * (1.0 - LAM_INIT)
    o_ref[0] = o.T.astype(BF16)


def _attn_bounded_kernel(lam_ref, sg_ref, q_ref, k_ref, vt_ref, o_ref, acc_scr):
    qi = pl.program_id(2)
    tq = q_ref.shape[2]
    tk = ATT_TK
    nt_dims = (((1,), (1,)), ((), ()))
    tri = (lax.broadcasted_iota(jnp.int32, (tk, tk), 0)
           <= lax.broadcasted_iota(jnp.int32, (tk, tk), 1))

    def fold(key_start, n_keys, q_cols, masked, first):
        vt = vt_ref[0, 0, 0:B_V_DIM, pl.ds(key_start, n_keys)]
        for c in range(2):
            e = lax.dot_general(k_ref[0, c, pl.ds(key_start, n_keys), :], q_ref[0, c, q_cols, :],
                                nt_dims, preferred_element_type=F32)
            if masked:
                e_diag = jnp.where(tri, e[n_keys - tk:, :], NEG_BIG)
                e = e_diag if n_keys == tk else jnp.concatenate([e[:n_keys - tk, :], e_diag], axis=0)
            p = jnp.exp2(e)
            p_sum = jnp.sum(p, axis=0, keepdims=True)
            pv = jnp.dot(vt, p.astype(BF16), preferred_element_type=F32)
            cols = slice(c * tq + q_cols.start, c * tq + q_cols.stop)
            if first:
                acc_scr[0:B_V_DIM, cols] = pv
                acc_scr[B_V_DIM:B_V_DIM + 1, cols] = p_sum
            else:
                acc_scr[0:B_V_DIM, cols] += pv
                acc_scr[B_V_DIM:B_V_DIM + 1, cols] += p_sum

    strips = [slice(j * tk, (j + 1) * tk) for j in range(tq // tk)]
    diag0 = pl.multiple_of(qi * tq, tq)
    for j, strip in enumerate(strips):
        fold(diag0, (j + 1) * tk, strip, True, True)

    def trip(t, carry):
        for strip in strips:
            fold(pl.multiple_of(t * tq, tq), tq, strip, False, False)
        return carry

    lax.fori_loop(0, qi, trip, 0)
    _attn_finalize(lam_ref, sg_ref, acc_scr, o_ref)


def _diff_attention(lam_vecs, sub_g, q_arr, k_arr, vt_arr, bounded):
    bsz, _, seq, _ = q_arr.shape
    tq = ATT_TQ_BOUNDED if bounded else ATT_TQ
    scratch = [pltpu.VMEM((VT_ROWS, 2 * tq), F32)]
    if not bounded:
        scratch += [pltpu.VMEM((2, ATT_TK, tq), F32), pltpu.VMEM((2, ATT_TK, tq), F32)]
    return pl.pallas_call(
        _attn_bounded_kernel if bounded else _attn_kernel,
        grid=(bsz, B_HEADS, seq // tq),
        in_specs=[
            pl.BlockSpec((4, B_HEAD_DIM), lambda b, h, qi: (0, 0)),
            pl.BlockSpec((B_V_DIM, 1), lambda b, h, qi: (0, 0)),
            pl.BlockSpec((1, 2, tq, LANES), lambda b, h, qi: (b, h, qi, 0)),
            pl.BlockSpec((1, 2, seq, LANES), lambda b, h, qi: (b, h, 0, 0)),
            pl.BlockSpec((1, 1, VT_ROWS, seq), lambda b, h, qi: (b, h, 0, 0)),
        ],
        out_specs=pl.BlockSpec((1, tq, B_V_DIM), lambda b, h, qi: (b, qi, h)),
        out_shape=jax.ShapeDtypeStruct((bsz, seq, B_HEADS * B_V_DIM), BF16),
        scratch_shapes=scratch,
        compiler_params=pltpu.CompilerParams(
            dimension_semantics=("arbitrary", "arbitrary", "arbitrary"),
            vmem_limit_bytes=VMEM_LIMIT_BYTES),
        name="diff_attention_bounded" if bounded else "diff_attention",
    )(lam_vecs, sub_g, q_arr, k_arr, vt_arr)


def _first_argmax(vals, iota, top):
    return jnp.min(jnp.where(vals == top, iota, vals.shape[0]), axis=0, keepdims=True)


def _merge_kernel(u_ref, va_ref, yb_ref, ga_ref, gb_ref, x_ref, ws_ref, bs_ref,
                  wua_ref, wub_ref, wo_ref, g2_ref, wr_hi_ref, wr_lo_ref, br_ref,
                  x1_ref, h2_ref, comb_ref, gidx_ref):
    tm = x_ref.shape[0]
    row = lax.broadcasted_iota(jnp.int32, (CHUNK, CHUNK), 0)
    col = lax.broadcasted_iota(jnp.int32, (CHUNK, CHUNK), 1)
    causal = row >= col

    n_chunks = tm // CHUNK
    group_cols = []
    for g in range(A_GROUPS):
        w_g = jnp.where(causal, ws_ref[g], 0.0).astype(BF16)
        gl = slice(g * LANES, (g + 1) * LANES)
        v_g = jnp.concatenate([va_ref[c * CHUNK:(c + 1) * CHUNK, gl] for c in range(n_chunks)],
                              axis=1)
        z = jnp.dot(w_g, v_g, preferred_element_type=F32) + bs_ref[g]
        z = jnp.concatenate([z[:, c * LANES:(c + 1) * LANES] for c in range(n_chunks)], axis=0)
        group_cols.append((u_ref[:, gl].astype(F32) * z).astype(BF16))
    y_a = jnp.concatenate(group_cols, axis=1)

    up_a = jnp.dot(y_a, wua_ref[...], preferred_element_type=F32)
    up_b = jnp.dot(yb_ref[...], wub_ref[...], preferred_element_type=F32)
    merged = ga_ref[...].astype(F32) * up_a + gb_ref[...].astype(F32) * up_b
    x1 = x_ref[...] + jnp.dot(merged.astype(BF16), wo_ref[...], preferred_element_type=F32)
    x1_ref[...] = x1

    r = lax.rsqrt(jnp.mean(x1 * x1, axis=-1, keepdims=True) + EPS)
    h2 = (x1 * r) * g2_ref[...]
    h2_hi = h2.astype(BF16)
    h2_ref[...] = h2_hi
    h2_lo = (h2 - h2_hi.astype(F32)).astype(BF16)

    nt_dims = (((1,), (1,)), ((), ()))
    lt = (lax.dot_general(wr_hi_ref[...], h2_hi, nt_dims, preferred_element_type=F32)
          + lax.dot_general(wr_hi_ref[...], h2_lo, nt_dims, preferred_element_type=F32)
          + lax.dot_general(wr_lo_ref[...], h2_hi, nt_dims, preferred_element_type=F32)
          + br_ref[...])

    iota8 = lax.broadcasted_iota(jnp.int32, (8, tm), 0)
    gl = lt[0:8]
    ge = jnp.exp(gl - jnp.max(gl, axis=0, keepdims=True))
    gp = ge / jnp.sum(ge, axis=0, keepdims=True)
    g_gate = jnp.max(gp, axis=0, keepdims=True)
    g_idx = _first_argmax(gp, iota8, g_gate)

    el = jnp.zeros((EXPERTS_PER_GROUP, tm), F32)
    for gi in range(N_GROUPS):
        el = jnp.where(g_idx == gi, lt[8 + 8 * gi:16 + 8 * gi], el)
    ee = jnp.exp(el - jnp.max(el, axis=0, keepdims=True))
    ep = ee / jnp.sum(ee, axis=0, keepdims=True)
    p_top1 = jnp.max(ep, axis=0, keepdims=True)
    i_top1 = _first_argmax(ep, iota8, p_top1)
    ep_rest = jnp.where(iota8 == i_top1, -1.0, ep)
    p_top2 = jnp.max(ep_rest, axis=0, keepdims=True)
    i_top2 = _first_argmax(ep_rest, iota8, p_top2)
    den = p_top1 + p_top2
    w_top1 = g_gate * (p_top1 / den)
    w_top2 = g_gate * (p_top2 / den)
    in_group = jnp.where(iota8 == i_top1, w_top1, 0.0) + jnp.where(iota8 == i_top2, w_top2, 0.0)
    blocks = [jnp.where(g_idx == gi, in_group, 0.0) for gi in range(N_GROUPS)]
    blocks.append(jnp.zeros((LANES - N_EXPERTS, tm), F32))
    comb_t = jnp.concatenate(blocks, axis=0)
    comb_ref[...] = comb_t.T
    gidx_ref[...] = jnp.broadcast_to(g_idx, gidx_ref.shape)


def _merge_and_route(u, va, yb, ga, gb, x2, ws, bs, wua, wub, wo, g2, wr_hi, wr_lo, br):
    n_tok = x2.shape[0]
    tm = MERGE_TM
    row_spec = pl.BlockSpec((tm, D_MODEL), lambda i: (i, 0))
    full = lambda shape: pl.BlockSpec(shape, lambda i: (0,) * len(shape))
    return pl.pallas_call(
        _merge_kernel,
        grid=(n_tok // tm,),
        in_specs=[
            row_spec, row_spec, row_spec, row_spec, row_spec, row_spec,
            full((A_GROUPS, CHUNK, CHUNK)), full((A_GROUPS, CHUNK, 1)),
            full((D_MODEL, D_MODEL)), full((D_MODEL, D_MODEL)), full((D_MODEL, D_MODEL)),
            full((1, D_MODEL)),
            full((ROUTER_ROWS, D_MODEL)), full((ROUTER_ROWS, D_MODEL)), full((ROUTER_ROWS, 1)),
        ],
        out_specs=[row_spec, row_spec, pl.BlockSpec((tm, LANES), lambda i: (i, 0)),
                   pl.BlockSpec((8, tm), lambda i: (0, i))],
        out_shape=[
            jax.ShapeDtypeStruct((n_tok, D_MODEL), F32),
            jax.ShapeDtypeStruct((n_tok, D_MODEL), BF16),
            jax.ShapeDtypeStruct((n_tok, LANES), F32),
            jax.ShapeDtypeStruct((8, n_tok), jnp.int32),
        ],
        compiler_params=pltpu.CompilerParams(
            dimension_semantics=("arbitrary",),
            vmem_limit_bytes=VMEM_LIMIT_BYTES),
        name="merge_and_route",
    )(u, va, yb, ga, gb, x2, ws, bs, wua, wub, wo, g2, wr_hi, wr_lo, br)


def _moe_kernel(g_ref, h_ref, c_ref, x1_ref, w13_ref, w2_ref, o_ref,
                p_scr, xg_scr, cg_scr, y_scr, bounds_ref):
    e = pl.program_id(1)
    n_steps = pl.num_programs(1)
    tt = h_ref.shape[0]
    rows = p_scr.shape[0]
    rb = MOE_ROW_BLOCK

    @pl.when(e == 0)
    def _():
        g_idx = g_ref[0:1, :]
        slot = lax.broadcasted_iota(jnp.int32, (16, tt), 0)
        onehot = jnp.where(slot == g_idx, 1.0, 0.0)
        upper = (lax.broadcasted_iota(jnp.int32, (LANES, LANES), 0)
                 <= lax.broadcasted_iota(jnp.int32, (LANES, LANES), 1)).astype(BF16)
        carry = jnp.zeros((16, 1), F32)
        counts = []
        for j in range(tt // LANES):
            seg = onehot[:, j * LANES:(j + 1) * LANES].astype(BF16)
            cs = jnp.dot(seg, upper, preferred_element_type=F32) + carry
            counts.append(cs)
            carry = cs[:, LANES - 1:LANES]
        running = jnp.concatenate(counts, axis=1)
        n_blocks = jnp.floor((carry + (rb - 1)) * (1.0 / rb))
        slot_col = lax.broadcasted_iota(jnp.int32, (16, 1), 0)
        first_block = jnp.zeros((16, 1), F32)
        for g in range(N_GROUPS - 1):
            first_block = first_block + jnp.where(slot_col > g, n_blocks[g:g + 1, :], 0.0)
        end_block = first_block + n_blocks
        for g in range(N_GROUPS):
            bounds_ref[g] = first_block[g:g + 1, :][0, 0].astype(jnp.int32)
            bounds_ref[N_GROUPS + g] = end_block[g:g + 1, :][0, 0].astype(jnp.int32)
        pos = jnp.sum(onehot * (first_block * rb + running - 1.0), axis=0, keepdims=True)
        pos = pos.astype(jnp.int32)

        c = c_ref[...]
        c_hi = c.astype(BF16)
        c_rest = c - c_hi.astype(F32)
        c_mid = c_rest.astype(BF16)
        c_lo = (c_rest - c_mid.astype(F32)).astype(BF16)
        h_ext = jnp.concatenate([h_ref[...], c_hi, c_mid, c_lo], axis=1)
        for ch in range(rows // MXU_DIM):
            sl = slice(ch * MXU_DIM, (ch + 1) * MXU_DIM)
            row_id = lax.broadcasted_iota(jnp.int32, (MXU_DIM, tt), 0) + ch * MXU_DIM
            perm = jnp.where(row_id == pos, 1.0, 0.0).astype(BF16)
            p_scr[sl, :] = perm
            got = jnp.dot(perm, h_ext, preferred_element_type=F32)
            xg_scr[sl, :] = got[:, :D_MODEL].astype(BF16)
            cg_scr[sl, :] = (got[:, D_MODEL:D_MODEL + LANES]
                             + got[:, D_MODEL + LANES:D_MODEL + 2 * LANES]
                             + got[:, D_MODEL + 2 * LANES:])
        y_scr[...] = jnp.zeros_like(y_scr)

    group = e // (EXPERTS_PER_GROUP // MOE_NE)
    to_front = (LANES - e * MOE_NE) & (LANES - 1)

    def row_block(b):
        r0 = pl.multiple_of(b * rb, rb)
        xb = xg_scr[pl.ds(r0, rb), :]
        cb = pltpu.roll(cg_scr[pl.ds(r0, rb), :], to_front, axis=1)
        hids = []
        for n in range(MOE_NE):
            ab = jnp.dot(xb, w13_ref[n], preferred_element_type=F32)
            a = ab[:, :D_EXPERT]
            b3 = ab[:, D_EXPERT:]
            hid = (a * (1.0 / (1.0 + jnp.exp(-a)))) * b3
            hids.append((hid * cb[:, n:n + 1]).astype(BF16))
        hid_all = jnp.concatenate(hids, axis=1)
        y_scr[pl.ds(r0, rb), :] = jnp.dot(hid_all, w2_ref[...],
                                          preferred_element_type=F32).astype(BF16)

    first = bounds_ref[group]
    n_blocks = bounds_ref[N_GROUPS + group] - first

    def pair(i, carry):
        row_block(first + 2 * i)
        row_block(first + 2 * i + 1)
        return carry

    lax.fori_loop(0, n_blocks // 2, pair, 0)

    @pl.when(n_blocks % 2 == 1)
    def _():
        row_block(first + n_blocks - 1)

    @pl.when(e == n_steps - 1)
    def _():
        tn_dims = (((0,), (0,)), ((), ()))
        y = lax.dot_general(p_scr[...], y_scr[...], tn_dims,
                            preferred_element_type=F32)
        o_ref[...] = x1_ref[...] + y


def _moe(gidx, h2, comb, x1, w13, w2f):
    assert MOE_NE == EXPERTS_PER_GROUP, "each row block is written by exactly one grid step"
    n_tok = h2.shape[0]
    tt = MOE_TM
    rows = tt + N_GROUPS * MOE_ROW_BLOCK
    return pl.pallas_call(
        _moe_kernel,
        grid=(n_tok // tt, N_EXPERTS // MOE_NE),
        in_specs=[
            pl.BlockSpec((8, tt), lambda i, e: (0, i)),
            pl.BlockSpec((tt, D_MODEL), lambda i, e: (i, 0)),
            pl.BlockSpec((tt, LANES), lambda i, e: (i, 0)),
            pl.BlockSpec((tt, D_MODEL), lambda i, e: (i, 0)),
            pl.BlockSpec((MOE_NE, D_MODEL, 2 * D_EXPERT), lambda i, e: (e, 0, 0)),
            pl.BlockSpec((MOE_NE * D_EXPERT, D_MODEL), lambda i, e: (e, 0)),
        ],
        out_specs=pl.BlockSpec((tt, D_MODEL), lambda i, e: (i, 0)),
        out_shape=jax.ShapeDtypeStruct((n_tok, D_MODEL), F32),
        scratch_shapes=[
            pltpu.VMEM((rows, tt), BF16),
            pltpu.VMEM((rows, D_MODEL), BF16),
            pltpu.VMEM((rows, LANES), F32),
            pltpu.VMEM((rows, D_MODEL), BF16),
            pltpu.SMEM((2 * N_GROUPS,), jnp.int32),
        ],
        compiler_params=pltpu.CompilerParams(
            dimension_semantics=("arbitrary", "arbitrary"),
            vmem_limit_bytes=MOE_VMEM_LIMIT_BYTES),
        name="moe_experts",
    )(gidx, h2, comb, x1, w13, w2f)


def _segment_ones(seg):
    idx = jnp.arange(MXU_DIM) // seg
    return (idx[:, None] == idx[None, :]).astype(BF16)


def kernel(x, norm1_g, w_in, v_norm_g, w_s, b_s, q_norm_g, k_norm_g, lambda_q1, lambda_k1, lambda_q2, lambda_k2, sub_norm_g, w_up_a, w_up_b, w_gate, b_gate, w_out, norm2_g, w_rg, b_rg, w_re, b_re, w1, w3, w2):
    bsz, seq, d = x.shape
    assert d == D_MODEL and seq % PROJ_TM == 0 and seq % ATT_TQ == 0 and ATT_TQ == 2 * ATT_TK
    assert seq % ATT_TQ_BOUNDED == 0 and ATT_TQ_BOUNDED % ATT_TK == 0
    assert norm1_g.shape[0] == 1, "single layer"
    n_tok = bsz * seq
    assert n_tok % MOE_TM == 0 and n_tok % MERGE_TM == 0
    x2 = x.reshape(n_tok, d)

    w_cat = jnp.concatenate([w_in[0], w_gate[0]], axis=1).astype(BF16)
    gv = v_norm_g[0].reshape(1, D_MODEL)
    gq = jnp.tile(q_norm_g[0] * (B_HEAD_DIM ** -0.5 * LOG2_E), 2 * B_HEADS)[None, :]
    gk = jnp.tile(k_norm_g[0], 2 * B_HEADS)[None, :]
    score_bound = (SCORE_BOUND_MARGIN * B_HEAD_DIM ** 0.5 * LOG2_E
                   * jnp.max(jnp.abs(q_norm_g[0] * k_norm_g[0]))).reshape(1, 1)
    u, va, q_arr, k_arr, vt_arr, ga, gb = _input_projection(
        x2, norm1_g, w_cat, b_gate, gv, gq, gk, _segment_ones(B_HEAD_DIM), _segment_ones(LANES),
        score_bound, bsz, seq)

    lam_vecs = jnp.concatenate([lambda_q1, lambda_k1, lambda_q2, lambda_k2], axis=0)
    attn_args = (lam_vecs, sub_norm_g[0][:, None], q_arr, k_arr, vt_arr)
    yb = lax.cond(score_bound[0, 0] <= MAX_BOUNDED_SCORE,
                  lambda args: _diff_attention(*args, bounded=True),
                  lambda args: _diff_attention(*args, bounded=False), attn_args)
    yb = yb.reshape(n_tok, D_MODEL)

    pad_rows = ROUTER_ROWS - 8 - N_EXPERTS
    wr = jnp.concatenate([w_rg[0].T, jnp.zeros((4, d), F32), w_re[0].T,
                          jnp.zeros((pad_rows, d), F32)], axis=0)
    br = jnp.concatenate([b_rg[0], jnp.full((4,), NEG_BIG, F32), b_re[0],
                          jnp.zeros((pad_rows,), F32)])[:, None]
    wr_hi = wr.astype(BF16)
    wr_lo = (wr - wr_hi.astype(F32)).astype(BF16)
    x1, h2, comb, gidx = _merge_and_route(
        u, va, yb, ga, gb, x2, w_s[0], b_s[0][:, :, None],
        w_up_a[0].astype(BF16), w_up_b[0].astype(BF16), w_out[0].astype(BF16),
        norm2_g, wr_hi, wr_lo, br)

    w13 = jnp.concatenate([w1[0], w3[0]], axis=2).astype(BF16)
    w2f = w2[0].astype(BF16).reshape(N_EXPERTS * D_EXPERT, D_MODEL)
    out = _moe(gidx, h2, comb, x1, w13, w2f)
    return out.reshape(bsz, seq, d)
```

```python
import functools
import math

import jax
import jax.numpy as jnp
import numpy as np
from jax import lax
from jax.experimental import pallas as pl
from jax.experimental.pallas import tpu as pltpu

F32 = jnp.float32
BF16 = jnp.bfloat16

D_MODEL = 1024
EPS = 1e-6
A_GROUPS = 8
CHUNK = 128
B_HEADS = 8
B_HEAD_DIM = 64
B_V_DIM = 128
IN_COLS = 5 * D_MODEL
N_GROUPS = 4
EXPERTS_PER_GROUP = 8
N_EXPERTS = 32
D_EXPERT = 256
LAM_INIT = 0.8 - 0.6 * math.exp(-0.3 * 0)

LANES = 128
MXU_DIM = 256
VMEM_LIMIT_BYTES = 56 * 1024 * 1024
MOE_VMEM_LIMIT_BYTES = 63 * 1024 * 1024

NEG_BIG = -1e30
LOG2_E = math.log2(math.e)
ALIBI_SLOPES = tuple(2.0 ** (-8.0 * (h + 1) / B_HEADS) for h in range(B_HEADS))
SLOPE_PARTS = 4
VT_ROWS = B_V_DIM + 16
SCORE_BOUND_MARGIN = 1.02
MAX_BOUNDED_SCORE = 48.0

PROJ_TM = 512
ATT_TQ_BOUNDED = 2048
ATT_TQ = 1024
ATT_TK = 512
MERGE_TM = 512
MOE_TM = 1024
MOE_NE = 8
MOE_ROW_UNIT = 64
ROUTER_ROWS = 48


def _bf16_parts(value, n_parts):
    parts, rest = [], np.float64(value)
    for _ in range(n_parts):
        part = np.float64(np.asarray(rest, np.float32).astype(BF16).astype(np.float32))
        parts.append(float(part))
        rest = rest - part
    return parts


def _segment_rms_scale(acc, seg_ones_ref, seg):
    sq = (acc * acc).astype(BF16)
    parts = []
    for p in range(acc.shape[1] // MXU_DIM):
        parts.append(jnp.dot(sq[:, p * MXU_DIM:(p + 1) * MXU_DIM], seg_ones_ref[...],
                             preferred_element_type=F32))
    ss = jnp.concatenate(parts, axis=1)
    return lax.rsqrt(ss * (1.0 / seg) + EPS)


def _inproj_kernel(x_ref, g1_ref, w_ref, bias_ref, gv_ref, gq_ref, gk_ref, ones64_ref, ones128_ref,
                   bound_ref,
                   u_ref, va_ref, q_ref, k_ref, vt_ref, ga_ref, gb_ref, *, seq_blocks):
    i = pl.program_id(0)
    tm = x_ref.shape[0]
    x = x_ref[...]
    r = lax.rsqrt(jnp.mean(x * x, axis=-1, keepdims=True) + EPS)
    h = ((x * r) * g1_ref[...]).astype(BF16)

    def slab(j):
        return jnp.dot(h, w_ref[:, j * D_MODEL:(j + 1) * D_MODEL], preferred_element_type=F32)

    lane = lax.broadcasted_iota(jnp.int32, (1, LANES), 1)
    lo_half = lane < B_HEAD_DIM

    u_ref[...] = slab(0).astype(BF16)

    acc = slab(1)
    r = _segment_rms_scale(acc, ones128_ref, LANES)
    va_ref[...] = ((acc * r) * gv_ref[...]).astype(BF16)

    spare_col = lane & (B_HEAD_DIM - 1)
    pos = (i % seq_blocks) * tm + lax.broadcasted_iota(jnp.int32, (tm, LANES), 0)
    pos_lo = (pos & (MXU_DIM - 1)).astype(F32)
    pos_hi = (pos - (pos & (MXU_DIM - 1))).astype(F32)
    pos_cols = jnp.where(spare_col & SLOPE_PARTS == 0, pos_lo, pos_hi)
    bound = bound_ref[...]
    bound_hi = bound.astype(BF16).astype(F32)
    bound_mid = (bound - bound_hi).astype(BF16).astype(F32)
    bound_lo = (bound - bound_hi - bound_mid).astype(BF16).astype(F32)
    bound_cols = jnp.where(spare_col == 4 * SLOPE_PARTS, bound_hi,
                           jnp.where(spare_col == 4 * SLOPE_PARTS + 1, bound_mid,
                                     jnp.where(spare_col == 4 * SLOPE_PARTS + 2, bound_lo, 0.0)))
    q_shared = jnp.where((spare_col >= 2 * SLOPE_PARTS) & (spare_col < 4 * SLOPE_PARTS),
                         pos_cols, bound_cols)
    k_shared = jnp.where(spare_col < 2 * SLOPE_PARTS, pos_cols, 0.0)

    def slope_cols(hd, sign):
        parts = _bf16_parts(ALIBI_SLOPES[hd] * LOG2_E, SLOPE_PARTS)
        cvec = jnp.zeros((1, LANES), F32)
        for n, part in enumerate(parts):
            cvec = jnp.where(spare_col & (SLOPE_PARTS - 1) == n, sign * part, cvec)
        return cvec

    acc = slab(2)
    r = _segment_rms_scale(acc, ones64_ref, B_HEAD_DIM)
    qn = (acc * r) * gq_ref[...]
    for hd in range(B_HEADS):
        extra = jnp.where(spare_col < 2 * SLOPE_PARTS, slope_cols(hd, 1.0), q_shared)
        blk = qn[:, hd * LANES:(hd + 1) * LANES]
        q_ref[0, 2 * hd] = jnp.where(lo_half, blk, extra).astype(BF16)
        q_ref[0, 2 * hd + 1] = jnp.where(lo_half, extra, blk).astype(BF16)

    acc = slab(3)
    r = _segment_rms_scale(acc, ones64_ref, B_HEAD_DIM)
    kn = (acc * r) * gk_ref[...]
    for hd in range(B_HEADS):
        consts = jnp.where(spare_col < 4 * SLOPE_PARTS, slope_cols(hd, -1.0),
                           jnp.where(spare_col < 4 * SLOPE_PARTS + 3, -1.0, 0.0))
        extra = jnp.where(spare_col < 2 * SLOPE_PARTS, k_shared, consts)
        blk = kn[:, hd * LANES:(hd + 1) * LANES]
        k_ref[0, 2 * hd] = jnp.where(lo_half, blk, extra).astype(BF16)
        k_ref[0, 2 * hd + 1] = jnp.where(lo_half, extra, blk).astype(BF16)

    acc = slab(4)
    for hd in range(B_HEADS):
        vt_ref[0, hd, 0:B_V_DIM, :] = acc[:, hd * LANES:(hd + 1) * LANES].T.astype(BF16)
        vt_ref[0, hd, B_V_DIM:VT_ROWS, :] = jnp.ones((VT_ROWS - B_V_DIM, tm), BF16)

    ga_ref[...] = (1.0 / (1.0 + jnp.exp(-(slab(5) + bias_ref[:, 0:D_MODEL])))).astype(BF16)
    gb_ref[...] = (1.0 / (1.0 + jnp.exp(-(slab(6) + bias_ref[:, D_MODEL:])))).astype(BF16)


def _input_projection(x2, g1, w_cat, b_gate, gv, gq, gk, ones64, ones128, bound, bsz, seq):
    n_tok = x2.shape[0]
    tm = PROJ_TM
    seq_blocks = seq // tm
    row_spec = pl.BlockSpec((tm, D_MODEL), lambda i: (i, 0))
    resident = lambda shape: pl.BlockSpec(shape, lambda i: (0,) * len(shape),
                                          pipeline_mode=pl.Buffered(1))
    head_map = lambda i: (i // seq_blocks, 0, i % seq_blocks, 0)
    tok_bf16 = jax.ShapeDtypeStruct((n_tok, D_MODEL), BF16)
    return pl.pallas_call(
        functools.partial(_inproj_kernel, seq_blocks=seq_blocks),
        grid=(n_tok // tm,),
        in_specs=[
            row_spec,
            resident((1, D_MODEL)),
            resident(w_cat.shape),
            resident(b_gate.shape),
            resident((1, D_MODEL)), resident((1, D_MODEL)), resident((1, D_MODEL)),
            resident((MXU_DIM, MXU_DIM)), resident((MXU_DIM, MXU_DIM)),
            resident((1, 1)),
        ],
        out_specs=[
            row_spec, row_spec,
            pl.BlockSpec((1, 2 * B_HEADS, tm, LANES), head_map),
            pl.BlockSpec((1, 2 * B_HEADS, tm, LANES), head_map),
            pl.BlockSpec((1, B_HEADS, VT_ROWS, tm), lambda i: (i // seq_blocks, 0, 0, i % seq_blocks)),
            row_spec, row_spec,
        ],
        out_shape=[
            tok_bf16, tok_bf16,
            jax.ShapeDtypeStruct((bsz, 2 * B_HEADS, seq, LANES), BF16),
            jax.ShapeDtypeStruct((bsz, 2 * B_HEADS, seq, LANES), BF16),
            jax.ShapeDtypeStruct((bsz, B_HEADS, VT_ROWS, seq), BF16),
            tok_bf16, tok_bf16,
        ],
        compiler_params=pltpu.CompilerParams(
            dimension_semantics=("arbitrary",),
            vmem_limit_bytes=VMEM_LIMIT_BYTES),
        name="input_projection",
    )(x2, g1, w_cat, b_gate, gv, gq, gk, ones64, ones128, bound)


def _attn_kernel(lam_ref, sg_ref, q_ref, k_ref, vt_ref, o_ref, acc_scr, sa_scr, sb_scr):
    qi = pl.program_id(2)
    tq = q_ref.shape[2]
    tk = ATT_TK
    nt_dims = (((1,), (1,)), ((), ()))

    assert tq == 2 * tk
    tri = (lax.broadcasted_iota(jnp.int32, (tk, tk), 0)
           <= lax.broadcasted_iota(jnp.int32, (tk, tk), 1))

    def scores(kb, s_scr, diag=None):
        ks = pl.multiple_of(kb * tk, tk)
        maxima = []
        for c in range(2):
            k_blk = k_ref[0, c, pl.ds(ks, tk), :]
            if diag is None:
                s = lax.dot_general(k_blk, q_ref[0, c], nt_dims, preferred_element_type=F32)
                s_scr[c] = s
                maxima.append(jnp.max(s, axis=0, keepdims=True))
                continue
            parts = []
            for half in range(diag, 2):
                cols = slice(half * tk, (half + 1) * tk)
                s = lax.dot_general(k_blk, q_ref[0, c, cols, :], nt_dims,
                                    preferred_element_type=F32)
                if half == diag:
                    s = jnp.where(tri, s, NEG_BIG)
                s_scr[c, :, cols] = s
                parts.append(jnp.max(s, axis=0, keepdims=True))
            maxima.append(parts[0] if len(parts) == 1 else jnp.concatenate(parts, axis=1))
        return tuple(maxima)

    def consume(kb, s_scr, m_old, blk_max, col0=0, first=False):
        ks = pl.multiple_of(kb * tk, tk)
        vt = vt_ref[0, 0, :, pl.ds(ks, tk)]
        m_new = []
        for c in range(2):
            m_c = blk_max[c] if first else jnp.maximum(m_old[c][:, col0:], blk_max[c])
            p = jnp.exp2(s_scr[c, :, col0:] - m_c).astype(BF16)
            pv = jnp.dot(vt, p, preferred_element_type=F32)
            cols = slice(c * tq + col0, (c + 1) * tq)
            if first:
                acc_scr[:, cols] = pv
            else:
                acc_scr[:, cols] = acc_scr[:, cols] * jnp.exp2(m_old[c][:, col0:] - m_c) + pv
            m_new.append(m_c)
        return tuple(m_new)

    max_a = scores(2 * qi, sa_scr, diag=0)
    max_b = scores(2 * qi + 1, sb_scr, diag=1)
    m = consume(2 * qi, sa_scr, None, max_a, first=True)
    max_a = scores(0, sa_scr)
    m_right = consume(2 * qi + 1, sb_scr, m, max_b, col0=tk)
    m = tuple(jnp.concatenate([m[c][:, :tk], m_right[c]], axis=1) for c in range(2))

    def trip(t, carry):
        m, max_a = carry
        max_b = scores(2 * t + 1, sb_scr)
        m = consume(2 * t, sa_scr, m, max_a)
        max_a = scores(2 * t + 2, sa_scr)
        m = consume(2 * t + 1, sb_scr, m, max_b)
        return m, max_a

    m, max_a = lax.fori_loop(0, qi - 1, trip, (m, max_a))

    @pl.when(qi > 0)
    def _():
        max_b = scores(2 * qi - 1, sb_scr)
        m_mid = consume(2 * qi - 2, sa_scr, m, max_a)
        consume(2 * qi - 1, sb_scr, m_mid, max_b)

    _attn_finalize(lam_ref, sg_ref, acc_scr, o_ref)


def _attn_finalize(lam_ref, sg_ref, acc_scr, o_ref):
    tq = o_ref.shape[1]
    lam_v = lam_ref[...]
    lam = (jnp.exp(jnp.sum(lam_v[0:1] * lam_v[1:2], axis=1, keepdims=True))
           - jnp.exp(jnp.sum(lam_v[2:3] * lam_v[3:4], axis=1, keepdims=True)) + LAM_INIT)
    o0 = acc_scr[0:B_V_DIM, 0:tq] / acc_scr[B_V_DIM:B_V_DIM + 1, 0:tq]
    o1 = acc_scr[0:B_V_DIM, tq:2 * tq] / acc_scr[B_V_DIM:B_V_DIM + 1, tq:2 * tq]
    o = o0 - lam * o1
    r = lax.rsqrt(jnp.mean(o * o, axis=0, keepdims=True) + EPS)
    o = ((o * r) * sg_ref[...]) * (1.0 - LAM_INIT)
    o_ref[0] = o.T.astype(BF16)


def _attn_bounded_kernel(lam_ref, sg_ref, q_ref, k_ref, vt_ref, o_ref, acc_scr):
    qi = pl.program_id(2)
    tq = q_ref.shape[2]
    tk = ATT_TK
    nt_dims = (((1,), (1,)), ((), ()))
    tri = (lax.broadcasted_iota(jnp.int32, (tk, tk), 0)
           <= lax.broadcasted_iota(jnp.int32, (tk, tk), 1))

    def fold(key_start, n_keys, q_cols, masked, first):
        vt = vt_ref[0, 0, 0:B_V_DIM, pl.ds(key_start, n_keys)]
        for c in range(2):
            e = lax.dot_general(k_ref[0, c, pl.ds(key_start, n_keys), :], q_ref[0, c, q_cols, :],
                                nt_dims, preferred_element_type=F32)
            if masked:
                e_diag = jnp.where(tri, e[n_keys - tk:, :], NEG_BIG)
                e = e_diag if n_keys == tk else jnp.concatenate([e[:n_keys - tk, :], e_diag], axis=0)
            p = jnp.exp2(e)
            p_sum = jnp.sum(p, axis=0, keepdims=True)
            pv = jnp.dot(vt, p.astype(BF16), preferred_element_type=F32)
            cols = slice(c * tq + q_cols.start, c * tq + q_cols.stop)
            if first:
                acc_scr[0:B_V_DIM, cols] = pv
                acc_scr[B_V_DIM:B_V_DIM + 1, cols] = p_sum
            else:
                acc_scr[0:B_V_DIM, cols] += pv
                acc_scr[B_V_DIM:B_V_DIM + 1, cols] += p_sum

    strips = [slice(j * tk, (j + 1) * tk) for j in range(tq // tk)]
    diag0 = pl.multiple_of(qi * tq, tq)
    for j, strip in enumerate(strips):
        fold(diag0, (j + 1) * tk, strip, True, True)

    def trip(t, carry):
        for strip in strips:
            fold(pl.multiple_of(t * tq, tq), tq, strip, False, False)
        return carry

    lax.fori_loop(0, qi, trip, 0)
    _attn_finalize(lam_ref, sg_ref, acc_scr, o_ref)


def _diff_attention(lam_vecs, sub_g, q_arr, k_arr, vt_arr, bounded):
    bsz, _, seq, _ = q_arr.shape
    tq = ATT_TQ_BOUNDED if bounded else ATT_TQ
    scratch = [pltpu.VMEM((VT_ROWS, 2 * tq), F32)]
    if not bounded:
        scratch += [pltpu.VMEM((2, ATT_TK, tq), F32), pltpu.VMEM((2, ATT_TK, tq), F32)]
    return pl.pallas_call(
        _attn_bounded_kernel if bounded else _attn_kernel,
        grid=(bsz, B_HEADS, seq // tq),
        in_specs=[
            pl.BlockSpec((4, B_HEAD_DIM), lambda b, h, qi: (0, 0)),
            pl.BlockSpec((B_V_DIM, 1), lambda b, h, qi: (0, 0)),
            pl.BlockSpec((1, 2, tq, LANES), lambda b, h, qi: (b, h, qi, 0)),
            pl.BlockSpec((1, 2, seq, LANES), lambda b, h, qi: (b, h, 0, 0)),
            pl.BlockSpec((1, 1, VT_ROWS, seq), lambda b, h, qi: (b, h, 0, 0)),
        ],
        out_specs=pl.BlockSpec((1, tq, B_V_DIM), lambda b, h, qi: (b, qi, h)),
        out_shape=jax.ShapeDtypeStruct((bsz, seq, B_HEADS * B_V_DIM), BF16),
        scratch_shapes=scratch,
        compiler_params=pltpu.CompilerParams(
            dimension_semantics=("arbitrary", "arbitrary", "arbitrary"),
            vmem_limit_bytes=VMEM_LIMIT_BYTES),
        name="diff_attention_bounded" if bounded else "diff_attention",
    )(lam_vecs, sub_g, q_arr, k_arr, vt_arr)


def _first_argmax(vals, iota, top):
    return jnp.min(jnp.where(vals == top, iota, vals.shape[0]), axis=0, keepdims=True)


def _merge_kernel(u_ref, va_ref, yb_ref, ga_ref, gb_ref, x_ref, ws_ref, bs_ref,
                  wua_ref, wub_ref, wo_ref, g2_ref, wr_hi_ref, wr_lo_ref, br_ref,
                  x1_ref, h2_ref, comb_ref, gidx_ref):
    tm = x_ref.shape[0]
    row = lax.broadcasted_iota(jnp.int32, (CHUNK, CHUNK), 0)
    col = lax.broadcasted_iota(jnp.int32, (CHUNK, CHUNK), 1)
    causal = row >= col

    n_chunks = tm // CHUNK
    group_cols = []
    for g in range(A_GROUPS):
        w_g = jnp.where(causal, ws_ref[g], 0.0).astype(BF16)
        gl = slice(g * LANES, (g + 1) * LANES)
        v_g = jnp.concatenate([va_ref[c * CHUNK:(c + 1) * CHUNK, gl] for c in range(n_chunks)],
                              axis=1)
        z = jnp.dot(w_g, v_g, preferred_element_type=F32) + bs_ref[g]
        z = jnp.concatenate([z[:, c * LANES:(c + 1) * LANES] for c in range(n_chunks)], axis=0)
        group_cols.append((u_ref[:, gl].astype(F32) * z).astype(BF16))
    y_a = jnp.concatenate(group_cols, axis=1)

    up_a = jnp.dot(y_a, wua_ref[...], preferred_element_type=F32)
    up_b = jnp.dot(yb_ref[...], wub_ref[...], preferred_element_type=F32)
    merged = ga_ref[...].astype(F32) * up_a + gb_ref[...].astype(F32) * up_b
    x1 = x_ref[...] + jnp.dot(merged.astype(BF16), wo_ref[...], preferred_element_type=F32)
    x1_ref[...] = x1

    r = lax.rsqrt(jnp.mean(x1 * x1, axis=-1, keepdims=True) + EPS)
    h2 = (x1 * r) * g2_ref[...]
    h2_hi = h2.astype(BF16)
    h2_ref[...] = h2_hi
    h2_lo = (h2 - h2_hi.astype(F32)).astype(BF16)

    nt_dims = (((1,), (1,)), ((), ()))
    lt = (lax.dot_general(wr_hi_ref[...], h2_hi, nt_dims, preferred_element_type=F32)
          + lax.dot_general(wr_hi_ref[...], h2_lo, nt_dims, preferred_element_type=F32)
          + lax.dot_general(wr_lo_ref[...], h2_hi, nt_dims, preferred_element_type=F32)
          + br_ref[...])

    iota8 = lax.broadcasted_iota(jnp.int32, (8, tm), 0)
    gl = lt[0:8]
    ge = jnp.exp(gl - jnp.max(gl, axis=0, keepdims=True))
    gp = ge / jnp.sum(ge, axis=0, keepdims=True)
    g_gate = jnp.max(gp, axis=0, keepdims=True)
    g_idx = _first_argmax(gp, iota8, g_gate)

    el = jnp.zeros((EXPERTS_PER_GROUP, tm), F32)
    for gi in range(N_GROUPS):
        el = jnp.where(g_idx == gi, lt[8 + 8 * gi:16 + 8 * gi], el)
    ee = jnp.exp(el - jnp.max(el, axis=0, keepdims=True))
    ep = ee / jnp.sum(ee, axis=0, keepdims=True)
    p_top1 = jnp.max(ep, axis=0, keepdims=True)
    i_top1 = _first_argmax(ep, iota8, p_top1)
    ep_rest = jnp.where(iota8 == i_top1, -1.0, ep)
    p_top2 = jnp.max(ep_rest, axis=0, keepdims=True)
    i_top2 = _first_argmax(ep_rest, iota8, p_top2)
    den = p_top1 + p_top2
    w_top1 = g_gate * (p_top1 / den)
    w_top2 = g_gate * (p_top2 / den)
    in_group = jnp.where(iota8 == i_top1, w_top1, 0.0) + jnp.where(iota8 == i_top2, w_top2, 0.0)
    blocks = [jnp.where(g_idx == gi, in_group, 0.0) for gi in range(N_GROUPS)]
    blocks.append(jnp.zeros((LANES - N_EXPERTS, tm), F32))
    comb_t = jnp.concatenate(blocks, axis=0)
    comb_ref[...] = comb_t.T
    gidx_ref[...] = jnp.broadcast_to(g_idx, gidx_ref.shape)


def _merge_and_route(u, va, yb, ga, gb, x2, ws, bs, wua, wub, wo, g2, wr_hi, wr_lo, br):
    n_tok = x2.shape[0]
    tm = MERGE_TM
    row_spec = pl.BlockSpec((tm, D_MODEL), lambda i: (i, 0))
    full = lambda shape: pl.BlockSpec(shape, lambda i: (0,) * len(shape))
    return pl.pallas_call(
        _merge_kernel,
        grid=(n_tok // tm,),
        in_specs=[
            row_spec, row_spec, row_spec, row_spec, row_spec, row_spec,
            full((A_GROUPS, CHUNK, CHUNK)), full((A_GROUPS, CHUNK, 1)),
            full((D_MODEL, D_MODEL)), full((D_MODEL, D_MODEL)), full((D_MODEL, D_MODEL)),
            full((1, D_MODEL)),
            full((ROUTER_ROWS, D_MODEL)), full((ROUTER_ROWS, D_MODEL)), full((ROUTER_ROWS, 1)),
        ],
        out_specs=[row_spec, row_spec, pl.BlockSpec((tm, LANES), lambda i: (i, 0)),
                   pl.BlockSpec((8, tm), lambda i: (0, i))],
        out_shape=[
            jax.ShapeDtypeStruct((n_tok, D_MODEL), F32),
            jax.ShapeDtypeStruct((n_tok, D_MODEL), BF16),
            jax.ShapeDtypeStruct((n_tok, LANES), F32),
            jax.ShapeDtypeStruct((8, n_tok), jnp.int32),
        ],
        compiler_params=pltpu.CompilerParams(
            dimension_semantics=("arbitrary",),
            vmem_limit_bytes=VMEM_LIMIT_BYTES),
        name="merge_and_route",
    )(u, va, yb, ga, gb, x2, ws, bs, wua, wub, wo, g2, wr_hi, wr_lo, br)


def _moe_kernel(g_ref, h_ref, c_ref, x1_ref, w13_ref, w2_ref, o_ref,
                p_scr, xg_scr, cg_scr, y_scr, bounds_ref):
    e = pl.program_id(1)
    n_steps = pl.num_programs(1)
    tt = h_ref.shape[0]
    rows = p_scr.shape[0]
    unit = MOE_ROW_UNIT
    rb = 2 * unit

    @pl.when(e == 0)
    def _():
        g_idx = g_ref[0:1, :]
        slot = lax.broadcasted_iota(jnp.int32, (16, tt), 0)
        onehot = jnp.where(slot == g_idx, 1.0, 0.0)
        upper = (lax.broadcasted_iota(jnp.int32, (LANES, LANES), 0)
                 <= lax.broadcasted_iota(jnp.int32, (LANES, LANES), 1)).astype(BF16)
        carry = jnp.zeros((16, 1), F32)
        counts = []
        for j in range(tt // LANES):
            seg = onehot[:, j * LANES:(j + 1) * LANES].astype(BF16)
            cs = jnp.dot(seg, upper, preferred_element_type=F32) + carry
            counts.append(cs)
            carry = cs[:, LANES - 1:LANES]
        running = jnp.concatenate(counts, axis=1)
        n_units = jnp.floor((carry + (unit - 1)) * (1.0 / unit))
        slot_col = lax.broadcasted_iota(jnp.int32, (16, 1), 0)
        first_unit = jnp.zeros((16, 1), F32)
        for g in range(N_GROUPS - 1):
            first_unit = first_unit + jnp.where(slot_col > g, n_units[g:g + 1, :], 0.0)
        for g in range(N_GROUPS):
            bounds_ref[g] = first_unit[g:g + 1, :][0, 0].astype(jnp.int32)
            bounds_ref[N_GROUPS + g] = n_units[g:g + 1, :][0, 0].astype(jnp.int32)
        pos = jnp.sum(onehot * (first_unit * unit + running - 1.0), axis=0, keepdims=True)
        pos = pos.astype(jnp.int32)

        c = c_ref[...]
        c_hi = c.astype(BF16)
        c_rest = c - c_hi.astype(F32)
        c_mid = c_rest.astype(BF16)
        c_lo = (c_rest - c_mid.astype(F32)).astype(BF16)
        h_ext = jnp.concatenate([h_ref[...], c_hi, c_mid, c_lo], axis=1)
        for ch in range(rows // MXU_DIM):
            sl = slice(ch * MXU_DIM, (ch + 1) * MXU_DIM)
            row_id = lax.broadcasted_iota(jnp.int32, (MXU_DIM, tt), 0) + ch * MXU_DIM
            perm = jnp.where(row_id == pos, 1.0, 0.0).astype(BF16)
            p_scr[sl, :] = perm
            got = jnp.dot(perm, h_ext, preferred_element_type=F32)
            xg_scr[sl, :] = got[:, :D_MODEL].astype(BF16)
            cg_scr[sl, :] = (got[:, D_MODEL:D_MODEL + LANES]
                             + got[:, D_MODEL + LANES:D_MODEL + 2 * LANES]
                             + got[:, D_MODEL + 2 * LANES:])
        y_scr[...] = jnp.zeros_like(y_scr)

    group = e // (EXPERTS_PER_GROUP // MOE_NE)
    to_front = (LANES - e * MOE_NE) & (LANES - 1)

    def row_block(r0, n_rows):
        r0 = pl.multiple_of(r0, unit)
        xb = xg_scr[pl.ds(r0, n_rows), :]
        cb = pltpu.roll(cg_scr[pl.ds(r0, n_rows), :], to_front, axis=1)
        hids = []
        for n in range(MOE_NE):
            ab = jnp.dot(xb, w13_ref[n], preferred_element_type=F32)
            a = ab[:, :D_EXPERT]
            b3 = ab[:, D_EXPERT:]
            hid = (a * (1.0 / (1.0 + jnp.exp(-a)))) * b3
            hids.append((hid * cb[:, n:n + 1]).astype(BF16))
        hid_all = jnp.concatenate(hids, axis=1)
        y_scr[pl.ds(r0, n_rows), :] = jnp.dot(hid_all, w2_ref[...],
                                              preferred_element_type=F32).astype(BF16)

    first_row = bounds_ref[group] * unit
    n_units = bounds_ref[N_GROUPS + group]
    n_blocks = n_units // 2

    def pair(i, carry):
        row_block(first_row + (2 * i) * rb, rb)
        row_block(first_row + (2 * i + 1) * rb, rb)
        return carry

    lax.fori_loop(0, n_blocks // 2, pair, 0)

    @pl.when(n_blocks % 2 == 1)
    def _():
        row_block(first_row + (n_blocks - 1) * rb, rb)

    @pl.when(n_units % 2 == 1)
    def _():
        row_block(first_row + n_blocks * rb, unit)

    @pl.when(e == n_steps - 1)
    def _():
        tn_dims = (((0,), (0,)), ((), ()))
        y = lax.dot_general(p_scr[...], y_scr[...], tn_dims,
                            preferred_element_type=F32)
        o_ref[...] = x1_ref[...] + y


def _moe(gidx, h2, comb, x1, w13, w2f):
    assert MOE_NE == EXPERTS_PER_GROUP, "each row block is written by exactly one grid step"
    n_tok = h2.shape[0]
    tt = MOE_TM
    rows = tt + N_GROUPS * MOE_ROW_UNIT
    assert rows % MXU_DIM == 0
    return pl.pallas_call(
        _moe_kernel,
        grid=(n_tok // tt, N_EXPERTS // MOE_NE),
        in_specs=[
            pl.BlockSpec((8, tt), lambda i, e: (0, i)),
            pl.BlockSpec((tt, D_MODEL), lambda i, e: (i, 0)),
            pl.BlockSpec((tt, LANES), lambda i, e: (i, 0)),
            pl.BlockSpec((tt, D_MODEL), lambda i, e: (i, 0)),
            pl.BlockSpec((MOE_NE, D_MODEL, 2 * D_EXPERT), lambda i, e: (e, 0, 0)),
            pl.BlockSpec((MOE_NE * D_EXPERT, D_MODEL), lambda i, e: (e, 0)),
        ],
        out_specs=pl.BlockSpec((tt, D_MODEL), lambda i, e: (i, 0)),
        out_shape=jax.ShapeDtypeStruct((n_tok, D_MODEL), F32),
        scratch_shapes=[
            pltpu.VMEM((rows, tt), BF16),
            pltpu.VMEM((rows, D_MODEL), BF16),
            pltpu.VMEM((rows, LANES), F32),
            pltpu.VMEM((rows, D_MODEL), BF16),
            pltpu.SMEM((2 * N_GROUPS,), jnp.int32),
        ],
        compiler_params=pltpu.CompilerParams(
            dimension_semantics=("arbitrary", "arbitrary"),
            vmem_limit_bytes=MOE_VMEM_LIMIT_BYTES),
        name="moe_experts",
    )(gidx, h2, comb, x1, w13, w2f)


def _segment_ones(seg):
    idx = jnp.arange(MXU_DIM) // seg
    return (idx[:, None] == idx[None, :]).astype(BF16)


def kernel(x, norm1_g, w_in, v_norm_g, w_s, b_s, q_norm_g, k_norm_g, lambda_q1, lambda_k1, lambda_q2, lambda_k2, sub_norm_g, w_up_a, w_up_b, w_gate, b_gate, w_out, norm2_g, w_rg, b_rg, w_re, b_re, w1, w3, w2):
    bsz, seq, d = x.shape
    assert d == D_MODEL and seq % PROJ_TM == 0 and seq % ATT_TQ == 0 and ATT_TQ == 2 * ATT_TK
    assert seq % ATT_TQ_BOUNDED == 0 and ATT_TQ_BOUNDED % ATT_TK == 0
    assert norm1_g.shape[0] == 1, "single layer"
    n_tok = bsz * seq
    assert n_tok % MOE_TM == 0 and n_tok % MERGE_TM == 0
    x2 = x.reshape(n_tok, d)

    w_cat = jnp.concatenate([w_in[0], w_gate[0]], axis=1).astype(BF16)
    gv = v_norm_g[0].reshape(1, D_MODEL)
    gq = jnp.tile(q_norm_g[0] * (B_HEAD_DIM ** -0.5 * LOG2_E), 2 * B_HEADS)[None, :]
    gk = jnp.tile(k_norm_g[0], 2 * B_HEADS)[None, :]
    score_bound = (SCORE_BOUND_MARGIN * B_HEAD_DIM ** 0.5 * LOG2_E
                   * jnp.max(jnp.abs(q_norm_g[0] * k_norm_g[0]))).reshape(1, 1)
    u, va, q_arr, k_arr, vt_arr, ga, gb = _input_projection(
        x2, norm1_g, w_cat, b_gate, gv, gq, gk, _segment_ones(B_HEAD_DIM), _segment_ones(LANES),
        score_bound, bsz, seq)

    lam_vecs = jnp.concatenate([lambda_q1, lambda_k1, lambda_q2, lambda_k2], axis=0)
    attn_args = (lam_vecs, sub_norm_g[0][:, None], q_arr, k_arr, vt_arr)
    yb = lax.cond(score_bound[0, 0] <= MAX_BOUNDED_SCORE,
                  lambda args: _diff_attention(*args, bounded=True),
                  lambda args: _diff_attention(*args, bounded=False), attn_args)
    yb = yb.reshape(n_tok, D_MODEL)

    pad_rows = ROUTER_ROWS - 8 - N_EXPERTS
    wr = jnp.concatenate([w_rg[0].T, jnp.zeros((4, d), F32), w_re[0].T,
                          jnp.zeros((pad_rows, d), F32)], axis=0)
    br = jnp.concatenate([b_rg[0], jnp.full((4,), NEG_BIG, F32), b_re[0],
                          jnp.zeros((pad_rows,), F32)])[:, None]
    wr_hi = wr.astype(BF16)
    wr_lo = (wr - wr_hi.astype(F32)).astype(BF16)
    x1, h2, comb, gidx = _merge_and_route(
        u, va, yb, ga, gb, x2, w_s[0], b_s[0][:, :, None],
        w_up_a[0].astype(BF16), w_up_b[0].astype(BF16), w_out[0].astype(BF16),
        norm2_g, wr_hi, wr_lo, br)

    w13 = jnp.concatenate([w1[0], w3[0]], axis=2).astype(BF16)
    w2f = w2[0].astype(BF16).reshape(N_EXPERTS * D_EXPERT, D_MODEL)
    out = _moe(gidx, h2, comb, x1, w13, w2f)
    return out.reshape(bsz, seq, d)
```

```python
import functools
import math

import jax
import jax.numpy as jnp
import numpy as np
from jax import lax
from jax.experimental import pallas as pl
from jax.experimental.pallas import tpu as pltpu

F32 = jnp.float32
BF16 = jnp.bfloat16

D_MODEL = 1024
EPS = 1e-6
A_GROUPS = 8
CHUNK = 128
B_HEADS = 8
B_HEAD_DIM = 64
B_V_DIM = 128
IN_COLS = 5 * D_MODEL
N_GROUPS = 4
EXPERTS_PER_GROUP = 8
N_EXPERTS = 32
D_EXPERT = 256
LAM_INIT = 0.8 - 0.6 * math.exp(-0.3 * 0)

LANES = 128
MXU_DIM = 256
VMEM_LIMIT_BYTES = 56 * 1024 * 1024
BIG_VMEM_LIMIT_BYTES = 63 * 1024 * 1024

NEG_BIG = -1e30
LOG2_E = math.log2(math.e)
ALIBI_SLOPES = tuple(2.0 ** (-8.0 * (h + 1) / B_HEADS) for h in range(B_HEADS))
SLOPE_PARTS = 4
VT_ROWS = B_V_DIM + 16
SCORE_BOUND_MARGIN = 1.02
MAX_BOUNDED_SCORE = 48.0

PROJ_TM = 512
ATT_TQ_BOUNDED = 2048
ATT_TQ = 1024
ATT_TK = 512
MERGE_TM = 1024
MERGE_SUB_TM = 512
MOE_TM = 1024
MOE_NE = 8
MOE_ROW_UNIT = 64
ROUTER_ROWS = 48


def _bf16_parts(value, n_parts):
    parts, rest = [], np.float64(value)
    for _ in range(n_parts):
        part = np.float64(np.asarray(rest, np.float32).astype(BF16).astype(np.float32))
        parts.append(float(part))
        rest = rest - part
    return parts


def _segment_rms_scale(acc, seg_ones_ref, seg):
    sq = (acc * acc).astype(BF16)
    parts = []
    for p in range(acc.shape[1] // MXU_DIM):
        parts.append(jnp.dot(sq[:, p * MXU_DIM:(p + 1) * MXU_DIM], seg_ones_ref[...],
                             preferred_element_type=F32))
    ss = jnp.concatenate(parts, axis=1)
    return lax.rsqrt(ss * (1.0 / seg) + EPS)


def _inproj_kernel(x_ref, g1_ref, w_ref, bias_ref, gv_ref, gq_ref, gk_ref, ones64_ref, ones128_ref,
                   bound_ref,
                   u_ref, va_ref, q_ref, k_ref, vt_ref, ga_ref, gb_ref, *, seq_blocks):
    i = pl.program_id(0)
    tm = x_ref.shape[0]
    x = x_ref[...]
    r = lax.rsqrt(jnp.mean(x * x, axis=-1, keepdims=True) + EPS)
    h = ((x * r) * g1_ref[...]).astype(BF16)

    def slab(j):
        return jnp.dot(h, w_ref[:, j * D_MODEL:(j + 1) * D_MODEL], preferred_element_type=F32)

    lane = lax.broadcasted_iota(jnp.int32, (1, LANES), 1)
    lo_half = lane < B_HEAD_DIM

    u_ref[...] = slab(0).astype(BF16)

    acc = slab(1)
    r = _segment_rms_scale(acc, ones128_ref, LANES)
    va_ref[...] = ((acc * r) * gv_ref[...]).astype(BF16)

    spare_col = lane & (B_HEAD_DIM - 1)
    pos = (i % seq_blocks) * tm + lax.broadcasted_iota(jnp.int32, (tm, LANES), 0)
    pos_lo = (pos & (MXU_DIM - 1)).astype(F32)
    pos_hi = (pos - (pos & (MXU_DIM - 1))).astype(F32)
    pos_cols = jnp.where(spare_col & SLOPE_PARTS == 0, pos_lo, pos_hi)
    bound = bound_ref[...]
    bound_hi = bound.astype(BF16).astype(F32)
    bound_mid = (bound - bound_hi).astype(BF16).astype(F32)
    bound_lo = (bound - bound_hi - bound_mid).astype(BF16).astype(F32)
    bound_cols = jnp.where(spare_col == 4 * SLOPE_PARTS, bound_hi,
                           jnp.where(spare_col == 4 * SLOPE_PARTS + 1, bound_mid,
                                     jnp.where(spare_col == 4 * SLOPE_PARTS + 2, bound_lo, 0.0)))
    q_shared = jnp.where((spare_col >= 2 * SLOPE_PARTS) & (spare_col < 4 * SLOPE_PARTS),
                         pos_cols, bound_cols)
    k_shared = jnp.where(spare_col < 2 * SLOPE_PARTS, pos_cols, 0.0)

    def slope_cols(hd, sign):
        parts = _bf16_parts(ALIBI_SLOPES[hd] * LOG2_E, SLOPE_PARTS)
        cvec = jnp.zeros((1, LANES), F32)
        for n, part in enumerate(parts):
            cvec = jnp.where(spare_col & (SLOPE_PARTS - 1) == n, sign * part, cvec)
        return cvec

    acc = slab(2)
    r = _segment_rms_scale(acc, ones64_ref, B_HEAD_DIM)
    qn = (acc * r) * gq_ref[...]
    for hd in range(B_HEADS):
        extra = jnp.where(spare_col < 2 * SLOPE_PARTS, slope_cols(hd, 1.0), q_shared)
        blk = qn[:, hd * LANES:(hd + 1) * LANES]
        q_ref[0, 2 * hd] = jnp.where(lo_half, blk, extra).astype(BF16)
        q_ref[0, 2 * hd + 1] = jnp.where(lo_half, extra, blk).astype(BF16)

    acc = slab(3)
    r = _segment_rms_scale(acc, ones64_ref, B_HEAD_DIM)
    kn = (acc * r) * gk_ref[...]
    for hd in range(B_HEADS):
        consts = jnp.where(spare_col < 4 * SLOPE_PARTS, slope_cols(hd, -1.0),
                           jnp.where(spare_col < 4 * SLOPE_PARTS + 3, -1.0, 0.0))
        extra = jnp.where(spare_col < 2 * SLOPE_PARTS, k_shared, consts)
        blk = kn[:, hd * LANES:(hd + 1) * LANES]
        k_ref[0, 2 * hd] = jnp.where(lo_half, blk, extra).astype(BF16)
        k_ref[0, 2 * hd + 1] = jnp.where(lo_half, extra, blk).astype(BF16)

    acc = slab(4)
    for hd in range(B_HEADS):
        vt_ref[0, hd, 0:B_V_DIM, :] = acc[:, hd * LANES:(hd + 1) * LANES].T.astype(BF16)
        vt_ref[0, hd, B_V_DIM:VT_ROWS, :] = jnp.ones((VT_ROWS - B_V_DIM, tm), BF16)

    ga_ref[...] = (1.0 / (1.0 + jnp.exp(-(slab(5) + bias_ref[:, 0:D_MODEL])))).astype(BF16)
    gb_ref[...] = (1.0 / (1.0 + jnp.exp(-(slab(6) + bias_ref[:, D_MODEL:])))).astype(BF16)


def _input_projection(x2, g1, w_cat, b_gate, gv, gq, gk, ones64, ones128, bound, bsz, seq):
    n_tok = x2.shape[0]
    tm = PROJ_TM
    seq_blocks = seq // tm
    row_spec = pl.BlockSpec((tm, D_MODEL), lambda i: (i, 0))
    resident = lambda shape: pl.BlockSpec(shape, lambda i: (0,) * len(shape),
                                          pipeline_mode=pl.Buffered(1))
    head_map = lambda i: (i // seq_blocks, 0, i % seq_blocks, 0)
    tok_bf16 = jax.ShapeDtypeStruct((n_tok, D_MODEL), BF16)
    return pl.pallas_call(
        functools.partial(_inproj_kernel, seq_blocks=seq_blocks),
        grid=(n_tok // tm,),
        in_specs=[
            row_spec,
            resident((1, D_MODEL)),
            resident(w_cat.shape),
            resident(b_gate.shape),
            resident((1, D_MODEL)), resident((1, D_MODEL)), resident((1, D_MODEL)),
            resident((MXU_DIM, MXU_DIM)), resident((MXU_DIM, MXU_DIM)),
            resident((1, 1)),
        ],
        out_specs=[
            row_spec, row_spec,
            pl.BlockSpec((1, 2 * B_HEADS, tm, LANES), head_map),
            pl.BlockSpec((1, 2 * B_HEADS, tm, LANES), head_map),
            pl.BlockSpec((1, B_HEADS, VT_ROWS, tm), lambda i: (i // seq_blocks, 0, 0, i % seq_blocks)),
            row_spec, row_spec,
        ],
        out_shape=[
            tok_bf16, tok_bf16,
            jax.ShapeDtypeStruct((bsz, 2 * B_HEADS, seq, LANES), BF16),
            jax.ShapeDtypeStruct((bsz, 2 * B_HEADS, seq, LANES), BF16),
            jax.ShapeDtypeStruct((bsz, B_HEADS, VT_ROWS, seq), BF16),
            tok_bf16, tok_bf16,
        ],
        compiler_params=pltpu.CompilerParams(
            dimension_semantics=("arbitrary",),
            vmem_limit_bytes=VMEM_LIMIT_BYTES),
        name="input_projection",
    )(x2, g1, w_cat, b_gate, gv, gq, gk, ones64, ones128, bound)


def _attn_kernel(lam_ref, sg_ref, q_ref, k_ref, vt_ref, o_ref, acc_scr, sa_scr, sb_scr):
    qi = pl.program_id(2)
    tq = q_ref.shape[2]
    tk = ATT_TK
    nt_dims = (((1,), (1,)), ((), ()))

    assert tq == 2 * tk
    tri = (lax.broadcasted_iota(jnp.int32, (tk, tk), 0)
           <= lax.broadcasted_iota(jnp.int32, (tk, tk), 1))

    def scores(kb, s_scr, diag=None):
        ks = pl.multiple_of(kb * tk, tk)
        maxima = []
        for c in range(2):
            k_blk = k_ref[0, c, pl.ds(ks, tk), :]
            if diag is None:
                s = lax.dot_general(k_blk, q_ref[0, c], nt_dims, preferred_element_type=F32)
                s_scr[c] = s
                maxima.append(jnp.max(s, axis=0, keepdims=True))
                continue
            parts = []
            for half in range(diag, 2):
                cols = slice(half * tk, (half + 1) * tk)
                s = lax.dot_general(k_blk, q_ref[0, c, cols, :], nt_dims,
                                    preferred_element_type=F32)
                if half == diag:
                    s = jnp.where(tri, s, NEG_BIG)
                s_scr[c, :, cols] = s
                parts.append(jnp.max(s, axis=0, keepdims=True))
            maxima.append(parts[0] if len(parts) == 1 else jnp.concatenate(parts, axis=1))
        return tuple(maxima)

    def consume(kb, s_scr, m_old, blk_max, col0=0, first=False):
        ks = pl.multiple_of(kb * tk, tk)
        vt = vt_ref[0, 0, :, pl.ds(ks, tk)]
        m_new = []
        for c in range(2):
            m_c = blk_max[c] if first else jnp.maximum(m_old[c][:, col0:], blk_max[c])
            p = jnp.exp2(s_scr[c, :, col0:] - m_c).astype(BF16)
            pv = jnp.dot(vt, p, preferred_element_type=F32)
            cols = slice(c * tq + col0, (c + 1) * tq)
            if first:
                acc_scr[:, cols] = pv
            else:
                acc_scr[:, cols] = acc_scr[:, cols] * jnp.exp2(m_old[c][:, col0:] - m_c) + pv
            m_new.append(m_c)
        return tuple(m_new)

    max_a = scores(2 * qi, sa_scr, diag=0)
    max_b = scores(2 * qi + 1, sb_scr, diag=1)
    m = consume(2 * qi, sa_scr, None, max_a, first=True)
    max_a = scores(0, sa_scr)
    m_right = consume(2 * qi + 1, sb_scr, m, max_b, col0=tk)
    m = tuple(jnp.concatenate([m[c][:, :tk], m_right[c]], axis=1) for c in range(2))

    def trip(t, carry):
        m, max_a = carry
        max_b = scores(2 * t + 1, sb_scr)
        m = consume(2 * t, sa_scr, m, max_a)
        max_a = scores(2 * t + 2, sa_scr)
        m = consume(2 * t + 1, sb_scr, m, max_b)
        return m, max_a

    m, max_a = lax.fori_loop(0, qi - 1, trip, (m, max_a))

    @pl.when(qi > 0)
    def _():
        max_b = scores(2 * qi - 1, sb_scr)
        m_mid = consume(2 * qi - 2, sa_scr, m, max_a)
        consume(2 * qi - 1, sb_scr, m_mid, max_b)

    _attn_finalize(lam_ref, sg_ref, acc_scr, o_ref)


def _attn_finalize(lam_ref, sg_ref, acc_scr, o_ref):
    tq = o_ref.shape[1]
    lam_v = lam_ref[...]
    lam = (jnp.exp(jnp.sum(lam_v[0:1] * lam_v[1:2], axis=1, keepdims=True))
           - jnp.exp(jnp.sum(lam_v[2:3] * lam_v[3:4], axis=1, keepdims=True)) + LAM_INIT)
    o0 = acc_scr[0:B_V_DIM, 0:tq] / acc_scr[B_V_DIM:B_V_DIM + 1, 0:tq]
    o1 = acc_scr[0:B_V_DIM, tq:2 * tq] / acc_scr[B_V_DIM:B_V_DIM + 1, tq:2 * tq]
    o = o0 - lam * o1
    r = lax.rsqrt(jnp.mean(o * o, axis=0, keepdims=True) + EPS)
    o = ((o * r) * sg_ref[...]) * (1.0 - LAM_INIT)
    o_ref[0] = o.T.astype(BF16)


def _attn_bounded_kernel(lam_ref, sg_ref, q_ref, k_ref, vt_ref, o_ref, acc_scr):
    qi = pl.program_id(2)
    tq = q_ref.shape[2]
    tk = ATT_TK
    nt_dims = (((1,), (1,)), ((), ()))
    tri = (lax.broadcasted_iota(jnp.int32, (tk, tk), 0)
           <= lax.broadcasted_iota(jnp.int32, (tk, tk), 1))

    def fold(key_start, n_keys, q_cols, masked, first):
        vt = vt_ref[0, 0, 0:B_V_DIM, pl.ds(key_start, n_keys)]
        for c in range(2):
            e = lax.dot_general(k_ref[0, c, pl.ds(key_start, n_keys), :], q_ref[0, c, q_cols, :],
                                nt_dims, preferred_element_type=F32)
            if masked:
                e_diag = jnp.where(tri, e[n_keys - tk:, :], NEG_BIG)
                e = e_diag if n_keys == tk else jnp.concatenate([e[:n_keys - tk, :], e_diag], axis=0)
            p = jnp.exp2(e)
            p_sum = jnp.sum(p, axis=0, keepdims=True)
            pv = jnp.dot(vt, p.astype(BF16), preferred_element_type=F32)
            cols = slice(c * tq + q_cols.start, c * tq + q_cols.stop)
            if first:
                acc_scr[0:B_V_DIM, cols] = pv
                acc_scr[B_V_DIM:B_V_DIM + 1, cols] = p_sum
            else:
                acc_scr[0:B_V_DIM, cols] += pv
                acc_scr[B_V_DIM:B_V_DIM + 1, cols] += p_sum

    strips = [slice(j * tk, (j + 1) * tk) for j in range(tq // tk)]
    diag0 = pl.multiple_of(qi * tq, tq)
    for j, strip in enumerate(strips):
        fold(diag0, (j + 1) * tk, strip, True, True)

    def trip(t, carry):
        for strip in strips:
            fold(pl.multiple_of(t * tq, tq), tq, strip, False, False)
        return carry

    lax.fori_loop(0, qi, trip, 0)
    _attn_finalize(lam_ref, sg_ref, acc_scr, o_ref)


def _diff_attention(lam_vecs, sub_g, q_arr, k_arr, vt_arr, bounded):
    bsz, _, seq, _ = q_arr.shape
    tq = ATT_TQ_BOUNDED if bounded else ATT_TQ
    scratch = [pltpu.VMEM((VT_ROWS, 2 * tq), F32)]
    if not bounded:
        scratch += [pltpu.VMEM((2, ATT_TK, tq), F32), pltpu.VMEM((2, ATT_TK, tq), F32)]
    return pl.pallas_call(
        _attn_bounded_kernel if bounded else _attn_kernel,
        grid=(bsz, B_HEADS, seq // tq),
        in_specs=[
            pl.BlockSpec((4, B_HEAD_DIM), lambda b, h, qi: (0, 0)),
            pl.BlockSpec((B_V_DIM, 1), lambda b, h, qi: (0, 0)),
            pl.BlockSpec((1, 2, tq, LANES), lambda b, h, qi: (b, h, qi, 0)),
            pl.BlockSpec((1, 2, seq, LANES), lambda b, h, qi: (b, h, 0, 0)),
            pl.BlockSpec((1, 1, VT_ROWS, seq), lambda b, h, qi: (b, h, 0, 0)),
        ],
        out_specs=pl.BlockSpec((1, tq, B_V_DIM), lambda b, h, qi: (b, qi, h)),
        out_shape=jax.ShapeDtypeStruct((bsz, seq, B_HEADS * B_V_DIM), BF16),
        scratch_shapes=scratch,
        compiler_params=pltpu.CompilerParams(
            dimension_semantics=("arbitrary", "arbitrary", "arbitrary"),
            vmem_limit_bytes=VMEM_LIMIT_BYTES),
        name="diff_attention_bounded" if bounded else "diff_attention",
    )(lam_vecs, sub_g, q_arr, k_arr, vt_arr)


def _first_argmax(vals, iota, top):
    return jnp.min(jnp.where(vals == top, iota, vals.shape[0]), axis=0, keepdims=True)


def _merge_kernel(u_ref, va_ref, yb_ref, ga_ref, gb_ref, x_ref, ws_ref, bs_ref,
                  wua_ref, wub_ref, wo_ref, g2_ref, wr_hi_ref, wr_lo_ref, br_ref,
                  x1_ref, h2_ref, comb_ref, gidx_ref):
    tm = x_ref.shape[0]
    row = lax.broadcasted_iota(jnp.int32, (CHUNK, CHUNK), 0)
    col = lax.broadcasted_iota(jnp.int32, (CHUNK, CHUNK), 1)
    causal = row >= col

    w_tril = [jnp.where(causal, ws_ref[g], 0.0).astype(BF16) for g in range(A_GROUPS)]
    nt_dims = (((1,), (1,)), ((), ()))

    def gate_a(rows):
        n_chunks = (rows.stop - rows.start) // CHUNK
        group_cols = []
        for g in range(A_GROUPS):
            gl = slice(g * LANES, (g + 1) * LANES)
            v_g = jnp.concatenate(
                [va_ref[rows.start + c * CHUNK:rows.start + (c + 1) * CHUNK, gl]
                 for c in range(n_chunks)], axis=1)
            z = jnp.dot(w_tril[g], v_g, preferred_element_type=F32) + bs_ref[g]
            z = jnp.concatenate([z[:, c * LANES:(c + 1) * LANES] for c in range(n_chunks)], axis=0)
            group_cols.append((u_ref[rows, gl].astype(F32) * z).astype(BF16))
        return jnp.concatenate(group_cols, axis=1)

    def merge(rows, y_a):
        up_a = jnp.dot(y_a, wua_ref[...], preferred_element_type=F32)
        up_b = jnp.dot(yb_ref[rows, :], wub_ref[...], preferred_element_type=F32)
        merged = ga_ref[rows, :].astype(F32) * up_a + gb_ref[rows, :].astype(F32) * up_b
        return merged.astype(BF16)

    def project_and_norm(rows, merged):
        x1 = x_ref[rows, :] + jnp.dot(merged, wo_ref[...], preferred_element_type=F32)
        x1_ref[rows, :] = x1
        r = lax.rsqrt(jnp.mean(x1 * x1, axis=-1, keepdims=True) + EPS)
        h2 = (x1 * r) * g2_ref[...]
        h2_hi = h2.astype(BF16)
        h2_ref[rows, :] = h2_hi
        return h2_hi, (h2 - h2_hi.astype(F32)).astype(BF16)

    def route(rows, h2_parts):
        h2_hi, h2_lo = h2_parts
        tm = rows.stop - rows.start
        lt = (lax.dot_general(wr_hi_ref[...], h2_hi, nt_dims, preferred_element_type=F32)
              + lax.dot_general(wr_hi_ref[...], h2_lo, nt_dims, preferred_element_type=F32)
              + lax.dot_general(wr_lo_ref[...], h2_hi, nt_dims, preferred_element_type=F32)
              + br_ref[...])

        iota8 = lax.broadcasted_iota(jnp.int32, (8, tm), 0)
        gl = lt[0:8]
        ge = jnp.exp(gl - jnp.max(gl, axis=0, keepdims=True))
        gp = ge / jnp.sum(ge, axis=0, keepdims=True)
        g_gate = jnp.max(gp, axis=0, keepdims=True)
        g_idx = _first_argmax(gp, iota8, g_gate)

        el = jnp.zeros((EXPERTS_PER_GROUP, tm), F32)
        for gi in range(N_GROUPS):
            el = jnp.where(g_idx == gi, lt[8 + 8 * gi:16 + 8 * gi], el)
        ee = jnp.exp(el - jnp.max(el, axis=0, keepdims=True))
        ep = ee / jnp.sum(ee, axis=0, keepdims=True)
        p_top1 = jnp.max(ep, axis=0, keepdims=True)
        i_top1 = _first_argmax(ep, iota8, p_top1)
        ep_rest = jnp.where(iota8 == i_top1, -1.0, ep)
        p_top2 = jnp.max(ep_rest, axis=0, keepdims=True)
        i_top2 = _first_argmax(ep_rest, iota8, p_top2)
        den = p_top1 + p_top2
        w_top1 = g_gate * (p_top1 / den)
        w_top2 = g_gate * (p_top2 / den)
        in_group = (jnp.where(iota8 == i_top1, w_top1, 0.0)
                    + jnp.where(iota8 == i_top2, w_top2, 0.0))
        blocks = [jnp.where(g_idx == gi, in_group, 0.0) for gi in range(N_GROUPS)]
        blocks.append(jnp.zeros((LANES - N_EXPERTS, tm), F32))
        comb_t = jnp.concatenate(blocks, axis=0)
        comb_ref[rows, :] = comb_t.T
        gidx_ref[:, rows] = jnp.broadcast_to(g_idx, (gidx_ref.shape[0], tm))

    stages = (gate_a, merge, project_and_norm, route)
    subs = [slice(s0, s0 + MERGE_SUB_TM) for s0 in range(0, tm, MERGE_SUB_TM)]
    state = [None] * len(subs)
    for step in range(len(subs) + len(stages) - 1):
        for si, rows in enumerate(subs):
            stage = step - si
            if 0 <= stage < len(stages):
                state[si] = stages[stage](rows, state[si]) if stage else stages[0](rows)


def _merge_and_route(u, va, yb, ga, gb, x2, ws, bs, wua, wub, wo, g2, wr_hi, wr_lo, br):
    n_tok = x2.shape[0]
    tm = MERGE_TM
    row_spec = pl.BlockSpec((tm, D_MODEL), lambda i: (i, 0))
    full = lambda shape: pl.BlockSpec(shape, lambda i: (0,) * len(shape),
                                      pipeline_mode=pl.Buffered(1))
    return pl.pallas_call(
        _merge_kernel,
        grid=(n_tok // tm,),
        in_specs=[
            row_spec, row_spec, row_spec, row_spec, row_spec, row_spec,
            full((A_GROUPS, CHUNK, CHUNK)), full((A_GROUPS, CHUNK, 1)),
            full((D_MODEL, D_MODEL)), full((D_MODEL, D_MODEL)), full((D_MODEL, D_MODEL)),
            full((1, D_MODEL)),
            full((ROUTER_ROWS, D_MODEL)), full((ROUTER_ROWS, D_MODEL)), full((ROUTER_ROWS, 1)),
        ],
        out_specs=[row_spec, row_spec, pl.BlockSpec((tm, LANES), lambda i: (i, 0)),
                   pl.BlockSpec((8, tm), lambda i: (0, i))],
        out_shape=[
            jax.ShapeDtypeStruct((n_tok, D_MODEL), F32),
            jax.ShapeDtypeStruct((n_tok, D_MODEL), BF16),
            jax.ShapeDtypeStruct((n_tok, LANES), F32),
            jax.ShapeDtypeStruct((8, n_tok), jnp.int32),
        ],
        compiler_params=pltpu.CompilerParams(
            dimension_semantics=("arbitrary",),
            vmem_limit_bytes=BIG_VMEM_LIMIT_BYTES),
        name="merge_and_route",
    )(u, va, yb, ga, gb, x2, ws, bs, wua, wub, wo, g2, wr_hi, wr_lo, br)


def _moe_kernel(g_ref, h_ref, c_ref, x1_ref, w13_ref, w2_ref, o_ref,
                p_scr, xg_scr, cg_scr, y_scr, bounds_ref):
    e = pl.program_id(1)
    n_steps = pl.num_programs(1)
    tt = h_ref.shape[0]
    rows = p_scr.shape[0]
    unit = MOE_ROW_UNIT
    rb = 2 * unit

    @pl.when(e == 0)
    def _():
        g_idx = g_ref[0:1, :]
        slot = lax.broadcasted_iota(jnp.int32, (16, tt), 0)
        onehot = jnp.where(slot == g_idx, 1.0, 0.0)
        upper = (lax.broadcasted_iota(jnp.int32, (LANES, LANES), 0)
                 <= lax.broadcasted_iota(jnp.int32, (LANES, LANES), 1)).astype(BF16)
        carry = jnp.zeros((16, 1), F32)
        counts = []
        for j in range(tt // LANES):
            seg = onehot[:, j * LANES:(j + 1) * LANES].astype(BF16)
            cs = jnp.dot(seg, upper, preferred_element_type=F32) + carry
            counts.append(cs)
            carry = cs[:, LANES - 1:LANES]
        running = jnp.concatenate(counts, axis=1)
        n_units = jnp.floor((carry + (unit - 1)) * (1.0 / unit))
        slot_col = lax.broadcasted_iota(jnp.int32, (16, 1), 0)
        first_unit = jnp.zeros((16, 1), F32)
        for g in range(N_GROUPS - 1):
            first_unit = first_unit + jnp.where(slot_col > g, n_units[g:g + 1, :], 0.0)
        for g in range(N_GROUPS):
            bounds_ref[g] = first_unit[g:g + 1, :][0, 0].astype(jnp.int32)
            bounds_ref[N_GROUPS + g] = n_units[g:g + 1, :][0, 0].astype(jnp.int32)
        pos = jnp.sum(onehot * (first_unit * unit + running - 1.0), axis=0, keepdims=True)
        pos = pos.astype(jnp.int32)

        c = c_ref[...]
        c_hi = c.astype(BF16)
        c_rest = c - c_hi.astype(F32)
        c_mid = c_rest.astype(BF16)
        c_lo = (c_rest - c_mid.astype(F32)).astype(BF16)
        h_ext = jnp.concatenate([h_ref[...], c_hi, c_mid, c_lo], axis=1)
        for ch in range(rows // MXU_DIM):
            sl = slice(ch * MXU_DIM, (ch + 1) * MXU_DIM)
            row_id = lax.broadcasted_iota(jnp.int32, (MXU_DIM, tt), 0) + ch * MXU_DIM
            perm = jnp.where(row_id == pos, 1.0, 0.0).astype(BF16)
            p_scr[sl, :] = perm
            got = jnp.dot(perm, h_ext, preferred_element_type=F32)
            xg_scr[sl, :] = got[:, :D_MODEL].astype(BF16)
            cg_scr[sl, :] = (got[:, D_MODEL:D_MODEL + LANES]
                             + got[:, D_MODEL + LANES:D_MODEL + 2 * LANES]
                             + got[:, D_MODEL + 2 * LANES:])
        y_scr[...] = jnp.zeros_like(y_scr)

    group = e // (EXPERTS_PER_GROUP // MOE_NE)
    to_front = (LANES - e * MOE_NE) & (LANES - 1)

    def row_block(r0, n_rows):
        r0 = pl.multiple_of(r0, unit)
        xb = xg_scr[pl.ds(r0, n_rows), :]
        cb = pltpu.roll(cg_scr[pl.ds(r0, n_rows), :], to_front, axis=1)
        hids = []
        for n in range(MOE_NE):
            ab = jnp.dot(xb, w13_ref[n], preferred_element_type=F32)
            a = ab[:, :D_EXPERT]
            b3 = ab[:, D_EXPERT:]
            hid = (a * (1.0 / (1.0 + jnp.exp(-a)))) * b3
            hids.append((hid * cb[:, n:n + 1]).astype(BF16))
        hid_all = jnp.concatenate(hids, axis=1)
        y_scr[pl.ds(r0, n_rows), :] = jnp.dot(hid_all, w2_ref[...],
                                              preferred_element_type=F32).astype(BF16)

    first_row = bounds_ref[group] * unit
    n_units = bounds_ref[N_GROUPS + group]
    n_blocks = n_units // 2

    def pair(i, carry):
        row_block(first_row + (2 * i) * rb, rb)
        row_block(first_row + (2 * i + 1) * rb, rb)
        return carry

    lax.fori_loop(0, n_blocks // 2, pair, 0)

    @pl.when(n_blocks % 2 == 1)
    def _():
        row_block(first_row + (n_blocks - 1) * rb, rb)

    @pl.when(n_units % 2 == 1)
    def _():
        row_block(first_row + n_blocks * rb, unit)

    @pl.when(e == n_steps - 1)
    def _():
        tn_dims = (((0,), (0,)), ((), ()))
        y = lax.dot_general(p_scr[...], y_scr[...], tn_dims,
                            preferred_element_type=F32)
        o_ref[...] = x1_ref[...] + y


def _moe(gidx, h2, comb, x1, w13, w2f):
    assert MOE_NE == EXPERTS_PER_GROUP, "each row block is written by exactly one grid step"
    n_tok = h2.shape[0]
    tt = MOE_TM
    rows = tt + N_GROUPS * MOE_ROW_UNIT
    assert rows % MXU_DIM == 0
    return pl.pallas_call(
        _moe_kernel,
        grid=(n_tok // tt, N_EXPERTS // MOE_NE),
        in_specs=[
            pl.BlockSpec((8, tt), lambda i, e: (0, i)),
            pl.BlockSpec((tt, D_MODEL), lambda i, e: (i, 0)),
            pl.BlockSpec((tt, LANES), lambda i, e: (i, 0)),
            pl.BlockSpec((tt, D_MODEL), lambda i, e: (i, 0)),
            pl.BlockSpec((MOE_NE, D_MODEL, 2 * D_EXPERT), lambda i, e: (e, 0, 0)),
            pl.BlockSpec((MOE_NE * D_EXPERT, D_MODEL), lambda i, e: (e, 0)),
        ],
        out_specs=pl.BlockSpec((tt, D_MODEL), lambda i, e: (i, 0)),
        out_shape=jax.ShapeDtypeStruct((n_tok, D_MODEL), F32),
        scratch_shapes=[
            pltpu.VMEM((rows, tt), BF16),
            pltpu.VMEM((rows, D_MODEL), BF16),
            pltpu.VMEM((rows, LANES), F32),
            pltpu.VMEM((rows, D_MODEL), BF16),
            pltpu.SMEM((2 * N_GROUPS,), jnp.int32),
        ],
        compiler_params=pltpu.CompilerParams(
            dimension_semantics=("arbitrary", "arbitrary"),
            vmem_limit_bytes=BIG_VMEM_LIMIT_BYTES),
        name="moe_experts",
    )(gidx, h2, comb, x1, w13, w2f)


def _segment_ones(seg):
    idx = jnp.arange(MXU_DIM) // seg
    return (idx[:, None] == idx[None, :]).astype(BF16)


def kernel(x, norm1_g, w_in, v_norm_g, w_s, b_s, q_norm_g, k_norm_g, lambda_q1, lambda_k1, lambda_q2, lambda_k2, sub_norm_g, w_up_a, w_up_b, w_gate, b_gate, w_out, norm2_g, w_rg, b_rg, w_re, b_re, w1, w3, w2):
    bsz, seq, d = x.shape
    assert d == D_MODEL and seq % PROJ_TM == 0 and seq % ATT_TQ == 0 and ATT_TQ == 2 * ATT_TK
    assert seq % ATT_TQ_BOUNDED == 0 and ATT_TQ_BOUNDED % ATT_TK == 0
    assert norm1_g.shape[0] == 1, "single layer"
    n_tok = bsz * seq
    assert n_tok % MOE_TM == 0 and n_tok % MERGE_TM == 0
    x2 = x.reshape(n_tok, d)

    w_cat = jnp.concatenate([w_in[0], w_gate[0]], axis=1).astype(BF16)
    gv = v_norm_g[0].reshape(1, D_MODEL)
    gq = jnp.tile(q_norm_g[0] * (B_HEAD_DIM ** -0.5 * LOG2_E), 2 * B_HEADS)[None, :]
    gk = jnp.tile(k_norm_g[0], 2 * B_HEADS)[None, :]
    score_bound = (SCORE_BOUND_MARGIN * B_HEAD_DIM ** 0.5 * LOG2_E
                   * jnp.max(jnp.abs(q_norm_g[0] * k_norm_g[0]))).reshape(1, 1)
    u, va, q_arr, k_arr, vt_arr, ga, gb = _input_projection(
        x2, norm1_g, w_cat, b_gate, gv, gq, gk, _segment_ones(B_HEAD_DIM), _segment_ones(LANES),
        score_bound, bsz, seq)

    lam_vecs = jnp.concatenate([lambda_q1, lambda_k1, lambda_q2, lambda_k2], axis=0)
    attn_args = (lam_vecs, sub_norm_g[0][:, None], q_arr, k_arr, vt_arr)
    yb = lax.cond(score_bound[0, 0] <= MAX_BOUNDED_SCORE,
                  lambda args: _diff_attention(*args, bounded=True),
                  lambda args: _diff_attention(*args, bounded=False), attn_args)
    yb = yb.reshape(n_tok, D_MODEL)

    pad_rows = ROUTER_ROWS - 8 - N_EXPERTS
    wr = jnp.concatenate([w_rg[0].T, jnp.zeros((4, d), F32), w_re[0].T,
                          jnp.zeros((pad_rows, d), F32)], axis=0)
    br = jnp.concatenate([b_rg[0], jnp.full((4,), NEG_BIG, F32), b_re[0],
                          jnp.zeros((pad_rows,), F32)])[:, None]
    wr_hi = wr.astype(BF16)
    wr_lo = (wr - wr_hi.astype(F32)).astype(BF16)
    x1, h2, comb, gidx = _merge_and_route(
        u, va, yb, ga, gb, x2, w_s[0], b_s[0][:, :, None],
        w_up_a[0].astype(BF16), w_up_b[0].astype(BF16), w_out[0].astype(BF16),
        norm2_g, wr_hi, wr_lo, br)

    w13 = jnp.concatenate([w1[0], w3[0]], axis=2).astype(BF16)
    w2f = w2[0].astype(BF16).reshape(N_EXPERTS * D_EXPERT, D_MODEL)
    out = _moe(gidx, h2, comb, x1, w13, w2f)
    return out.reshape(bsz, seq, d)
```

```python
import functools
import math

import jax
import jax.numpy as jnp
import numpy as np
from jax import lax
from jax.experimental import pallas as pl
from jax.experimental.pallas import tpu as pltpu

F32 = jnp.float32
BF16 = jnp.bfloat16

D_MODEL = 1024
EPS = 1e-6
A_GROUPS = 8
CHUNK = 128
B_HEADS = 8
B_HEAD_DIM = 64
B_V_DIM = 128
N_GROUPS = 4
EXPERTS_PER_GROUP = 8
N_EXPERTS = 32
D_EXPERT = 256
LAM_INIT = 0.8 - 0.6 * math.exp(-0.3 * 0)

LANES = 128
MXU_DIM = 256
VMEM_BYTES = 64 * 1024 * 1024
VMEM_LIMIT_BYTES = VMEM_BYTES - 8 * 1024 * 1024
BIG_VMEM_LIMIT_BYTES = VMEM_BYTES - 1024 * 1024

NEG_BIG = -1e30
LOG2_E = math.log2(math.e)
ALIBI_SLOPES = tuple(2.0 ** (-8.0 * (h + 1) / B_HEADS) for h in range(B_HEADS))
SLOPE_PARTS = 4
VT_ROWS = B_V_DIM + 16
SCORE_BOUND_MARGIN = 1.02
MAX_BOUNDED_SCORE = 48.0

PROJ_TM = 512
ATT_TQ_BOUNDED = 2048
ATT_TQ = 1024
ATT_TK = 512
MERGE_TM = 1024
MERGE_SUB_TM = 512
MOE_TM = 1024
MOE_NE = 8
MOE_ROW_UNIT = 64
ROUTER_ROWS = 48


def _bf16_parts(value, n_parts):
    parts, rest = [], np.float64(value)
    for _ in range(n_parts):
        part = np.float64(np.asarray(rest, np.float32).astype(BF16).astype(np.float32))
        parts.append(float(part))
        rest = rest - part
    return parts


def _segment_rms_scale(acc, seg_ones_ref, seg):
    sq = (acc * acc).astype(BF16)
    parts = []
    for p in range(acc.shape[1] // MXU_DIM):
        parts.append(jnp.dot(sq[:, p * MXU_DIM:(p + 1) * MXU_DIM], seg_ones_ref[...],
                             preferred_element_type=F32))
    ss = jnp.concatenate(parts, axis=1)
    return lax.rsqrt(ss * (1.0 / seg) + EPS)


def _inproj_kernel(x_ref, g1_ref, w_ref, wg_ref, bias_ref, gv_ref, gq_ref, gk_ref, ones64_ref, ones128_ref,
                   bound_ref,
                   u_ref, va_ref, q_ref, k_ref, vt_ref, ga_ref, gb_ref, *, seq_blocks):
    i = pl.program_id(0)
    tm = x_ref.shape[0]
    x = x_ref[...]
    r = lax.rsqrt(jnp.mean(x * x, axis=-1, keepdims=True) + EPS)
    h = ((x * r) * g1_ref[...]).astype(BF16)

    n_in = w_ref.shape[1] // D_MODEL

    def slab(j):
        ref, col = (w_ref, j) if j < n_in else (wg_ref, j - n_in)
        return jnp.dot(h, ref[:, col * D_MODEL:(col + 1) * D_MODEL], preferred_element_type=F32)

    lane = lax.broadcasted_iota(jnp.int32, (1, LANES), 1)
    lo_half = lane < B_HEAD_DIM

    spare_col = lane & (B_HEAD_DIM - 1)
    pos = (i % seq_blocks) * tm + lax.broadcasted_iota(jnp.int32, (tm, LANES), 0)
    pos_lo = (pos & (MXU_DIM - 1)).astype(F32)
    pos_hi = (pos - (pos & (MXU_DIM - 1))).astype(F32)
    pos_cols = jnp.where(spare_col & SLOPE_PARTS == 0, pos_lo, pos_hi)
    bound = bound_ref[...]
    bound_hi = bound.astype(BF16).astype(F32)
    bound_mid = (bound - bound_hi).astype(BF16).astype(F32)
    bound_lo = (bound - bound_hi - bound_mid).astype(BF16).astype(F32)
    bound_cols = jnp.where(spare_col == 4 * SLOPE_PARTS, bound_hi,
                           jnp.where(spare_col == 4 * SLOPE_PARTS + 1, bound_mid,
                                     jnp.where(spare_col == 4 * SLOPE_PARTS + 2, bound_lo, 0.0)))
    q_shared = jnp.where((spare_col >= 2 * SLOPE_PARTS) & (spare_col < 4 * SLOPE_PARTS),
                         pos_cols, bound_cols)
    k_shared = jnp.where(spare_col < 2 * SLOPE_PARTS, pos_cols, 0.0)

    def slope_cols(hd, sign):
        parts = _bf16_parts(ALIBI_SLOPES[hd] * LOG2_E, SLOPE_PARTS)
        cvec = jnp.zeros((1, LANES), F32)
        for n, part in enumerate(parts):
            cvec = jnp.where(spare_col & (SLOPE_PARTS - 1) == n, sign * part, cvec)
        return cvec

    def store_u(acc):
        u_ref[...] = acc.astype(BF16)

    def store_va(acc):
        r = _segment_rms_scale(acc, ones128_ref, LANES)
        va_ref[...] = ((acc * r) * gv_ref[...]).astype(BF16)

    def store_q(acc):
        r = _segment_rms_scale(acc, ones64_ref, B_HEAD_DIM)
        qn = (acc * r) * gq_ref[...]
        for hd in range(B_HEADS):
            extra = jnp.where(spare_col < 2 * SLOPE_PARTS, slope_cols(hd, 1.0), q_shared)
            blk = qn[:, hd * LANES:(hd + 1) * LANES]
            q_ref[0, 2 * hd] = jnp.where(lo_half, blk, extra).astype(BF16)
            q_ref[0, 2 * hd + 1] = jnp.where(lo_half, extra, blk).astype(BF16)

    def store_k(acc):
        r = _segment_rms_scale(acc, ones64_ref, B_HEAD_DIM)
        kn = (acc * r) * gk_ref[...]
        for hd in range(B_HEADS):
            consts = jnp.where(spare_col < 4 * SLOPE_PARTS, slope_cols(hd, -1.0),
                               jnp.where(spare_col < 4 * SLOPE_PARTS + 3, -1.0, 0.0))
            extra = jnp.where(spare_col < 2 * SLOPE_PARTS, k_shared, consts)
            blk = kn[:, hd * LANES:(hd + 1) * LANES]
            k_ref[0, 2 * hd] = jnp.where(lo_half, blk, extra).astype(BF16)
            k_ref[0, 2 * hd + 1] = jnp.where(lo_half, extra, blk).astype(BF16)

    def store_vt(acc):
        for hd in range(B_HEADS):
            vt_ref[0, hd, 0:B_V_DIM, :] = acc[:, hd * LANES:(hd + 1) * LANES].T.astype(BF16)
            vt_ref[0, hd, B_V_DIM:VT_ROWS, :] = jnp.ones((VT_ROWS - B_V_DIM, tm), BF16)

    def store_ga(acc):
        ga_ref[...] = (1.0 / (1.0 + jnp.exp(-(acc + bias_ref[:, 0:D_MODEL])))).astype(BF16)

    def store_gb(acc):
        gb_ref[...] = (1.0 / (1.0 + jnp.exp(-(acc + bias_ref[:, D_MODEL:])))).astype(BF16)

    for j, store in enumerate((store_u, store_va, store_q, store_k, store_vt, store_ga, store_gb)):
        store(slab(j))


def _input_projection(x2, g1, w_in, w_gate, b_gate, gv, gq, gk, ones64, ones128, bound, bsz, seq):
    n_tok = x2.shape[0]
    tm = PROJ_TM
    seq_blocks = seq // tm
    row_spec = pl.BlockSpec((tm, D_MODEL), lambda i: (i, 0))
    resident = lambda shape: pl.BlockSpec(shape, lambda i: (0,) * len(shape),
                                          pipeline_mode=pl.Buffered(1))
    head_map = lambda i: (i // seq_blocks, 0, i % seq_blocks, 0)
    tok_bf16 = jax.ShapeDtypeStruct((n_tok, D_MODEL), BF16)
    return pl.pallas_call(
        functools.partial(_inproj_kernel, seq_blocks=seq_blocks),
        grid=(n_tok // tm,),
        in_specs=[
            row_spec,
            resident((1, D_MODEL)),
            resident(w_in.shape), resident(w_gate.shape),
            resident(b_gate.shape),
            resident((1, D_MODEL)), resident((1, D_MODEL)), resident((1, D_MODEL)),
            resident((MXU_DIM, MXU_DIM)), resident((MXU_DIM, MXU_DIM)),
            resident((1, 1)),
        ],
        out_specs=[
            row_spec, row_spec,
            pl.BlockSpec((1, 2 * B_HEADS, tm, LANES), head_map),
            pl.BlockSpec((1, 2 * B_HEADS, tm, LANES), head_map),
            pl.BlockSpec((1, B_HEADS, VT_ROWS, tm), lambda i: (i // seq_blocks, 0, 0, i % seq_blocks)),
            row_spec, row_spec,
        ],
        out_shape=[
            tok_bf16, tok_bf16,
            jax.ShapeDtypeStruct((bsz, 2 * B_HEADS, seq, LANES), BF16),
            jax.ShapeDtypeStruct((bsz, 2 * B_HEADS, seq, LANES), BF16),
            jax.ShapeDtypeStruct((bsz, B_HEADS, VT_ROWS, seq), BF16),
            tok_bf16, tok_bf16,
        ],
        compiler_params=pltpu.CompilerParams(
            dimension_semantics=("arbitrary",),
            vmem_limit_bytes=VMEM_LIMIT_BYTES),
        name="input_projection",
    )(x2, g1, w_in, w_gate, b_gate, gv, gq, gk, ones64, ones128, bound)


def _attn_kernel(lam_ref, sg_ref, q_ref, k_ref, vt_ref, o_ref, acc_scr, sa_scr, sb_scr):
    qi = pl.program_id(2)
    tq = q_ref.shape[2]
    tk = ATT_TK
    nt_dims = (((1,), (1,)), ((), ()))

    assert tq == 2 * tk
    tri = (lax.broadcasted_iota(jnp.int32, (tk, tk), 0)
           <= lax.broadcasted_iota(jnp.int32, (tk, tk), 1))

    def scores(kb, s_scr, diag=None):
        ks = pl.multiple_of(kb * tk, tk)
        maxima = []
        for c in range(2):
            k_blk = k_ref[0, c, pl.ds(ks, tk), :]
            if diag is None:
                s = lax.dot_general(k_blk, q_ref[0, c], nt_dims, preferred_element_type=F32)
                s_scr[c] = s
                maxima.append(jnp.max(s, axis=0, keepdims=True))
                continue
            parts = []
            for half in range(diag, 2):
                cols = slice(half * tk, (half + 1) * tk)
                s = lax.dot_general(k_blk, q_ref[0, c, cols, :], nt_dims,
                                    preferred_element_type=F32)
                if half == diag:
                    s = jnp.where(tri, s, NEG_BIG)
                s_scr[c, :, cols] = s
                parts.append(jnp.max(s, axis=0, keepdims=True))
            maxima.append(parts[0] if len(parts) == 1 else jnp.concatenate(parts, axis=1))
        return tuple(maxima)

    def consume(kb, s_scr, m_old, blk_max, col0=0, first=False):
        ks = pl.multiple_of(kb * tk, tk)
        vt = vt_ref[0, 0, :, pl.ds(ks, tk)]
        m_new = []
        for c in range(2):
            m_c = blk_max[c] if first else jnp.maximum(m_old[c][:, col0:], blk_max[c])
            p = jnp.exp2(s_scr[c, :, col0:] - m_c).astype(BF16)
            pv = jnp.dot(vt, p, preferred_element_type=F32)
            cols = slice(c * tq + col0, (c + 1) * tq)
            if first:
                acc_scr[:, cols] = pv
            else:
                acc_scr[:, cols] = acc_scr[:, cols] * jnp.exp2(m_old[c][:, col0:] - m_c) + pv
            m_new.append(m_c)
        return tuple(m_new)

    max_a = scores(2 * qi, sa_scr, diag=0)
    max_b = scores(2 * qi + 1, sb_scr, diag=1)
    m = consume(2 * qi, sa_scr, None, max_a, first=True)
    max_a = scores(0, sa_scr)
    m_right = consume(2 * qi + 1, sb_scr, m, max_b, col0=tk)
    m = tuple(jnp.concatenate([m[c][:, :tk], m_right[c]], axis=1) for c in range(2))

    def trip(t, carry):
        m, max_a = carry
        max_b = scores(2 * t + 1, sb_scr)
        m = consume(2 * t, sa_scr, m, max_a)
        max_a = scores(2 * t + 2, sa_scr)
        m = consume(2 * t + 1, sb_scr, m, max_b)
        return m, max_a

    m, max_a = lax.fori_loop(0, qi - 1, trip, (m, max_a))

    @pl.when(qi > 0)
    def _():
        max_b = scores(2 * qi - 1, sb_scr)
        m_mid = consume(2 * qi - 2, sa_scr, m, max_a)
        consume(2 * qi - 1, sb_scr, m_mid, max_b)

    _attn_finalize(lam_ref, sg_ref, acc_scr, o_ref)


def _attn_finalize(lam_ref, sg_ref, acc_scr, o_ref):
    tq = o_ref.shape[1]
    lam_v = lam_ref[...]
    lam = (jnp.exp(jnp.sum(lam_v[0:1] * lam_v[1:2], axis=1, keepdims=True))
           - jnp.exp(jnp.sum(lam_v[2:3] * lam_v[3:4], axis=1, keepdims=True)) + LAM_INIT)
    o0 = acc_scr[0:B_V_DIM, 0:tq] / acc_scr[B_V_DIM:B_V_DIM + 1, 0:tq]
    o1 = acc_scr[0:B_V_DIM, tq:2 * tq] / acc_scr[B_V_DIM:B_V_DIM + 1, tq:2 * tq]
    o = o0 - lam * o1
    r = lax.rsqrt(jnp.mean(o * o, axis=0, keepdims=True) + EPS)
    o = ((o * r) * sg_ref[...]) * (1.0 - LAM_INIT)
    o_ref[0] = o.T.astype(BF16)


def _attn_bounded_kernel(lam_ref, sg_ref, q_ref, k_ref, vt_ref, o_ref, acc_scr):
    qi = pl.program_id(2)
    tq = q_ref.shape[2]
    tk = ATT_TK
    nt_dims = (((1,), (1,)), ((), ()))
    tri = (lax.broadcasted_iota(jnp.int32, (tk, tk), 0)
           <= lax.broadcasted_iota(jnp.int32, (tk, tk), 1))

    def fold(key_start, n_keys, q_cols, masked, first):
        vt = vt_ref[0, 0, 0:B_V_DIM, pl.ds(key_start, n_keys)]
        for c in range(2):
            e = lax.dot_general(k_ref[0, c, pl.ds(key_start, n_keys), :], q_ref[0, c, q_cols, :],
                                nt_dims, preferred_element_type=F32)
            if masked:
                e_diag = jnp.where(tri, e[n_keys - tk:, :], NEG_BIG)
                e = e_diag if n_keys == tk else jnp.concatenate([e[:n_keys - tk, :], e_diag], axis=0)
            p = jnp.exp2(e)
            p_sum = jnp.sum(p, axis=0, keepdims=True)
            pv = jnp.dot(vt, p.astype(BF16), preferred_element_type=F32)
            cols = slice(c * tq + q_cols.start, c * tq + q_cols.stop)
            if first:
                acc_scr[0:B_V_DIM, cols] = pv
                acc_scr[B_V_DIM:B_V_DIM + 1, cols] = p_sum
            else:
                acc_scr[0:B_V_DIM, cols] += pv
                acc_scr[B_V_DIM:B_V_DIM + 1, cols] += p_sum

    strips = [slice(j * tk, (j + 1) * tk) for j in range(tq // tk)]
    diag0 = pl.multiple_of(qi * tq, tq)
    for j, strip in enumerate(strips):
        fold(diag0, (j + 1) * tk, strip, True, True)

    def trip(t, carry):
        for strip in strips:
            fold(pl.multiple_of(t * tq, tq), tq, strip, False, False)
        return carry

    lax.fori_loop(0, qi, trip, 0)
    _attn_finalize(lam_ref, sg_ref, acc_scr, o_ref)


def _diff_attention(lam_vecs, sub_g, q_arr, k_arr, vt_arr, bounded):
    bsz, _, seq, _ = q_arr.shape
    tq = ATT_TQ_BOUNDED if bounded else ATT_TQ
    scratch = [pltpu.VMEM((VT_ROWS, 2 * tq), F32)]
    if not bounded:
        scratch += [pltpu.VMEM((2, ATT_TK, tq), F32), pltpu.VMEM((2, ATT_TK, tq), F32)]
    return pl.pallas_call(
        _attn_bounded_kernel if bounded else _attn_kernel,
        grid=(bsz, B_HEADS, seq // tq),
        in_specs=[
            pl.BlockSpec((4, B_HEAD_DIM), lambda b, h, qi: (0, 0)),
            pl.BlockSpec((B_V_DIM, 1), lambda b, h, qi: (0, 0)),
            pl.BlockSpec((1, 2, tq, LANES), lambda b, h, qi: (b, h, qi, 0)),
            pl.BlockSpec((1, 2, seq, LANES), lambda b, h, qi: (b, h, 0, 0)),
            pl.BlockSpec((1, 1, VT_ROWS, seq), lambda b, h, qi: (b, h, 0, 0)),
        ],
        out_specs=pl.BlockSpec((1, tq, B_V_DIM), lambda b, h, qi: (b, qi, h)),
        out_shape=jax.ShapeDtypeStruct((bsz, seq, B_HEADS * B_V_DIM), BF16),
        scratch_shapes=scratch,
        compiler_params=pltpu.CompilerParams(
            dimension_semantics=("arbitrary", "arbitrary", "arbitrary"),
            vmem_limit_bytes=VMEM_LIMIT_BYTES),
        name="diff_attention_bounded" if bounded else "diff_attention",
    )(lam_vecs, sub_g, q_arr, k_arr, vt_arr)


def _first_argmax(vals, iota, top):
    return jnp.min(jnp.where(vals == top, iota, vals.shape[0]), axis=0, keepdims=True)


def _merge_kernel(u_ref, va_ref, yb_ref, ga_ref, gb_ref, x_ref, ws_ref, bs_ref,
                  w3_ref, g2_ref, wr_hi_ref, wr_lo_ref, br_ref,
                  x1_ref, h2_ref, comb_ref, gidx_ref):
    tm = x_ref.shape[0]
    row = lax.broadcasted_iota(jnp.int32, (CHUNK, CHUNK), 0)
    col = lax.broadcasted_iota(jnp.int32, (CHUNK, CHUNK), 1)
    causal = row >= col

    w_tril = [jnp.where(causal, ws_ref[g], 0.0).astype(BF16) for g in range(A_GROUPS)]
    nt_dims = (((1,), (1,)), ((), ()))

    def gate_a(rows):
        n_chunks = (rows.stop - rows.start) // CHUNK
        group_cols = []
        for g in range(A_GROUPS):
            gl = slice(g * LANES, (g + 1) * LANES)
            v_g = jnp.concatenate(
                [va_ref[rows.start + c * CHUNK:rows.start + (c + 1) * CHUNK, gl]
                 for c in range(n_chunks)], axis=1)
            z = jnp.dot(w_tril[g], v_g, preferred_element_type=F32) + bs_ref[g]
            z = jnp.concatenate([z[:, c * LANES:(c + 1) * LANES] for c in range(n_chunks)], axis=0)
            group_cols.append((u_ref[rows, gl].astype(F32) * z).astype(BF16))
        return jnp.concatenate(group_cols, axis=1)

    def merge(rows, y_a):
        up_a = jnp.dot(y_a, w3_ref[0], preferred_element_type=F32)
        up_b = jnp.dot(yb_ref[rows, :], w3_ref[1], preferred_element_type=F32)
        merged = ga_ref[rows, :].astype(F32) * up_a + gb_ref[rows, :].astype(F32) * up_b
        return merged.astype(BF16)

    def project_and_norm(rows, merged):
        x1 = x_ref[rows, :] + jnp.dot(merged, w3_ref[2], preferred_element_type=F32)
        x1_ref[rows, :] = x1
        r = lax.rsqrt(jnp.mean(x1 * x1, axis=-1, keepdims=True) + EPS)
        h2 = (x1 * r) * g2_ref[...]
        h2_hi = h2.astype(BF16)
        h2_ref[rows, :] = h2_hi
        return h2_hi, (h2 - h2_hi.astype(F32)).astype(BF16)

    def route(rows, h2_parts):
        h2_hi, h2_lo = h2_parts
        tm = rows.stop - rows.start
        lt = (lax.dot_general(wr_hi_ref[...], h2_hi, nt_dims, preferred_element_type=F32)
              + lax.dot_general(wr_hi_ref[...], h2_lo, nt_dims, preferred_element_type=F32)
              + lax.dot_general(wr_lo_ref[...], h2_hi, nt_dims, preferred_element_type=F32)
              + br_ref[...])

        iota8 = lax.broadcasted_iota(jnp.int32, (8, tm), 0)
        gl = lt[0:8]
        ge = jnp.exp(gl - jnp.max(gl, axis=0, keepdims=True))
        gp = ge / jnp.sum(ge, axis=0, keepdims=True)
        g_gate = jnp.max(gp, axis=0, keepdims=True)
        g_idx = _first_argmax(gp, iota8, g_gate)

        el = jnp.zeros((EXPERTS_PER_GROUP, tm), F32)
        for gi in range(N_GROUPS):
            el = jnp.where(g_idx == gi, lt[8 + 8 * gi:16 + 8 * gi], el)
        ee = jnp.exp(el - jnp.max(el, axis=0, keepdims=True))
        ep = ee / jnp.sum(ee, axis=0, keepdims=True)
        p_top1 = jnp.max(ep, axis=0, keepdims=True)
        i_top1 = _first_argmax(ep, iota8, p_top1)
        ep_rest = jnp.where(iota8 == i_top1, -1.0, ep)
        p_top2 = jnp.max(ep_rest, axis=0, keepdims=True)
        i_top2 = _first_argmax(ep_rest, iota8, p_top2)
        den = p_top1 + p_top2
        w_top1 = g_gate * (p_top1 / den)
        w_top2 = g_gate * (p_top2 / den)
        in_group = (jnp.where(iota8 == i_top1, w_top1, 0.0)
                    + jnp.where(iota8 == i_top2, w_top2, 0.0))
        blocks = [jnp.where(g_idx == gi, in_group, 0.0) for gi in range(N_GROUPS)]
        blocks.append(jnp.zeros((LANES - N_EXPERTS, tm), F32))
        comb_t = jnp.concatenate(blocks, axis=0)
        comb_ref[rows, :] = comb_t.T
        gidx_ref[:, rows] = jnp.broadcast_to(g_idx, (gidx_ref.shape[0], tm))

    stages = (gate_a, merge, project_and_norm, route)
    subs = [slice(s0, s0 + MERGE_SUB_TM) for s0 in range(0, tm, MERGE_SUB_TM)]
    state = [None] * len(subs)
    for step in range(len(subs) + len(stages) - 1):
        for si, rows in enumerate(subs):
            stage = step - si
            if 0 <= stage < len(stages):
                state[si] = stages[stage](rows, state[si]) if stage else stages[0](rows)


def _merge_and_route(u, va, yb, ga, gb, x2, ws, bs, w_up_a_up_b_out, g2, wr_hi, wr_lo, br):
    n_tok = x2.shape[0]
    tm = MERGE_TM
    row_spec = pl.BlockSpec((tm, D_MODEL), lambda i: (i, 0))
    full = lambda shape: pl.BlockSpec(shape, lambda i: (0,) * len(shape),
                                      pipeline_mode=pl.Buffered(1))
    return pl.pallas_call(
        _merge_kernel,
        grid=(n_tok // tm,),
        in_specs=[
            row_spec, row_spec, row_spec, row_spec, row_spec, row_spec,
            full((A_GROUPS, CHUNK, CHUNK)), full((A_GROUPS, CHUNK, 1)),
            full((3, D_MODEL, D_MODEL)),
            full((1, D_MODEL)),
            full((ROUTER_ROWS, D_MODEL)), full((ROUTER_ROWS, D_MODEL)), full((ROUTER_ROWS, 1)),
        ],
        out_specs=[row_spec, row_spec, pl.BlockSpec((tm, LANES), lambda i: (i, 0)),
                   pl.BlockSpec((8, tm), lambda i: (0, i))],
        out_shape=[
            jax.ShapeDtypeStruct((n_tok, D_MODEL), F32),
            jax.ShapeDtypeStruct((n_tok, D_MODEL), BF16),
            jax.ShapeDtypeStruct((n_tok, LANES), F32),
            jax.ShapeDtypeStruct((8, n_tok), jnp.int32),
        ],
        compiler_params=pltpu.CompilerParams(
            dimension_semantics=("arbitrary",),
            vmem_limit_bytes=BIG_VMEM_LIMIT_BYTES),
        name="merge_and_route",
    )(u, va, yb, ga, gb, x2, ws, bs, w_up_a_up_b_out, g2, wr_hi, wr_lo, br)


def _moe_kernel(g_ref, h_ref, c_ref, x1_ref, w13_ref, w2_ref, o_ref,
                p_scr, xg_scr, cg_scr, y_scr, bounds_ref):
    e = pl.program_id(1)
    n_steps = pl.num_programs(1)
    tt = h_ref.shape[0]
    rows = p_scr.shape[0]
    unit = MOE_ROW_UNIT
    rb = 2 * unit

    @pl.when(e == 0)
    def _():
        g_idx = g_ref[0:1, :]
        slot = lax.broadcasted_iota(jnp.int32, (16, tt), 0)
        onehot = jnp.where(slot == g_idx, 1.0, 0.0)
        upper = (lax.broadcasted_iota(jnp.int32, (LANES, LANES), 0)
                 <= lax.broadcasted_iota(jnp.int32, (LANES, LANES), 1)).astype(BF16)
        carry = jnp.zeros((16, 1), F32)
        counts = []
        for j in range(tt // LANES):
            seg = onehot[:, j * LANES:(j + 1) * LANES].astype(BF16)
            cs = jnp.dot(seg, upper, preferred_element_type=F32) + carry
            counts.append(cs)
            carry = cs[:, LANES - 1:LANES]
        running = jnp.concatenate(counts, axis=1)
        n_units = jnp.floor((carry + (unit - 1)) * (1.0 / unit))
        slot_col = lax.broadcasted_iota(jnp.int32, (16, 1), 0)
        first_unit = jnp.zeros((16, 1), F32)
        for g in range(N_GROUPS - 1):
            first_unit = first_unit + jnp.where(slot_col > g, n_units[g:g + 1, :], 0.0)
        for g in range(N_GROUPS):
            bounds_ref[g] = first_unit[g:g + 1, :][0, 0].astype(jnp.int32)
            bounds_ref[N_GROUPS + g] = n_units[g:g + 1, :][0, 0].astype(jnp.int32)
        pos = jnp.sum(onehot * (first_unit * unit + running - 1.0), axis=0, keepdims=True)
        pos = pos.astype(jnp.int32)

        c = c_ref[...]
        c_hi = c.astype(BF16)
        c_rest = c - c_hi.astype(F32)
        c_mid = c_rest.astype(BF16)
        c_lo = (c_rest - c_mid.astype(F32)).astype(BF16)
        h_ext = jnp.concatenate([h_ref[...], c_hi, c_mid, c_lo], axis=1)
        for ch in range(rows // MXU_DIM):
            sl = slice(ch * MXU_DIM, (ch + 1) * MXU_DIM)
            row_id = lax.broadcasted_iota(jnp.int32, (MXU_DIM, tt), 0) + ch * MXU_DIM
            perm = jnp.where(row_id == pos, 1.0, 0.0).astype(BF16)
            p_scr[sl, :] = perm
            got = jnp.dot(perm, h_ext, preferred_element_type=F32)
            xg_scr[sl, :] = got[:, :D_MODEL].astype(BF16)
            cg_scr[sl, :] = (got[:, D_MODEL:D_MODEL + LANES]
                             + got[:, D_MODEL + LANES:D_MODEL + 2 * LANES]
                             + got[:, D_MODEL + 2 * LANES:])
        y_scr[...] = jnp.zeros_like(y_scr)

    group = e // (EXPERTS_PER_GROUP // MOE_NE)
    to_front = (LANES - e * MOE_NE) & (LANES - 1)

    def row_block(r0, n_rows):
        r0 = pl.multiple_of(r0, unit)
        xb = xg_scr[pl.ds(r0, n_rows), :]
        cb = pltpu.roll(cg_scr[pl.ds(r0, n_rows), :], to_front, axis=1)
        hids = []
        for n in range(MOE_NE):
            ab = jnp.dot(xb, w13_ref[n], preferred_element_type=F32)
            a = ab[:, :D_EXPERT]
            b3 = ab[:, D_EXPERT:]
            hid = (a * (1.0 / (1.0 + jnp.exp(-a)))) * b3
            hids.append((hid * cb[:, n:n + 1]).astype(BF16))
        hid_all = jnp.concatenate(hids, axis=1)
        y_scr[pl.ds(r0, n_rows), :] = jnp.dot(hid_all, w2_ref[...],
                                              preferred_element_type=F32).astype(BF16)

    first_row = bounds_ref[group] * unit
    n_units = bounds_ref[N_GROUPS + group]
    n_blocks = n_units // 2

    def pair(i, carry):
        row_block(first_row + (2 * i) * rb, rb)
        row_block(first_row + (2 * i + 1) * rb, rb)
        return carry

    lax.fori_loop(0, n_blocks // 2, pair, 0)

    @pl.when(n_blocks % 2 == 1)
    def _():
        row_block(first_row + (n_blocks - 1) * rb, rb)

    @pl.when(n_units % 2 == 1)
    def _():
        row_block(first_row + n_blocks * rb, unit)

    @pl.when(e == n_steps - 1)
    def _():
        tn_dims = (((0,), (0,)), ((), ()))
        y = lax.dot_general(p_scr[...], y_scr[...], tn_dims,
                            preferred_element_type=F32)
        o_ref[...] = x1_ref[...] + y


def _moe(gidx, h2, comb, x1, w13, w2f):
    assert MOE_NE == EXPERTS_PER_GROUP, "each row block is written by exactly one grid step"
    n_tok = h2.shape[0]
    tt = MOE_TM
    rows = tt + N_GROUPS * MOE_ROW_UNIT
    assert rows % MXU_DIM == 0
    return pl.pallas_call(
        _moe_kernel,
        grid=(n_tok // tt, N_EXPERTS // MOE_NE),
        in_specs=[
            pl.BlockSpec((8, tt), lambda i, e: (0, i)),
            pl.BlockSpec((tt, D_MODEL), lambda i, e: (i, 0)),
            pl.BlockSpec((tt, LANES), lambda i, e: (i, 0)),
            pl.BlockSpec((tt, D_MODEL), lambda i, e: (i, 0)),
            pl.BlockSpec((MOE_NE, D_MODEL, 2 * D_EXPERT), lambda i, e: (e, 0, 0)),
            pl.BlockSpec((MOE_NE * D_EXPERT, D_MODEL), lambda i, e: (e, 0)),
        ],
        out_specs=pl.BlockSpec((tt, D_MODEL), lambda i, e: (i, 0)),
        out_shape=jax.ShapeDtypeStruct((n_tok, D_MODEL), F32),
        scratch_shapes=[
            pltpu.VMEM((rows, tt), BF16),
            pltpu.VMEM((rows, D_MODEL), BF16),
            pltpu.VMEM((rows, LANES), F32),
            pltpu.VMEM((rows, D_MODEL), BF16),
            pltpu.SMEM((2 * N_GROUPS,), jnp.int32),
        ],
        compiler_params=pltpu.CompilerParams(
            dimension_semantics=("arbitrary", "arbitrary"),
            vmem_limit_bytes=BIG_VMEM_LIMIT_BYTES),
        name="moe_experts",
    )(gidx, h2, comb, x1, w13, w2f)


def _segment_ones(seg):
    idx = jnp.arange(MXU_DIM) // seg
    return (idx[:, None] == idx[None, :]).astype(BF16)


def kernel(x, norm1_g, w_in, v_norm_g, w_s, b_s, q_norm_g, k_norm_g, lambda_q1, lambda_k1, lambda_q2, lambda_k2, sub_norm_g, w_up_a, w_up_b, w_gate, b_gate, w_out, norm2_g, w_rg, b_rg, w_re, b_re, w1, w3, w2):
    bsz, seq, d = x.shape
    assert d == D_MODEL and seq % PROJ_TM == 0 and seq % ATT_TQ == 0 and ATT_TQ == 2 * ATT_TK
    assert seq % ATT_TQ_BOUNDED == 0 and ATT_TQ_BOUNDED % ATT_TK == 0
    assert norm1_g.shape[0] == 1, "single layer"
    n_tok = bsz * seq
    assert n_tok % MOE_TM == 0 and n_tok % MERGE_TM == 0
    x2 = x.reshape(n_tok, d)

    gv = v_norm_g[0].reshape(1, D_MODEL)
    gq = jnp.tile(q_norm_g[0] * (B_HEAD_DIM ** -0.5 * LOG2_E), 2 * B_HEADS)[None, :]
    gk = jnp.tile(k_norm_g[0], 2 * B_HEADS)[None, :]
    score_bound = (SCORE_BOUND_MARGIN * B_HEAD_DIM ** 0.5 * LOG2_E
                   * jnp.max(jnp.abs(q_norm_g[0] * k_norm_g[0]))).reshape(1, 1)
    u, va, q_arr, k_arr, vt_arr, ga, gb = _input_projection(
        x2, norm1_g, w_in[0].astype(BF16), w_gate[0].astype(BF16), b_gate, gv, gq, gk, _segment_ones(B_HEAD_DIM), _segment_ones(LANES),
        score_bound, bsz, seq)

    lam_vecs = jnp.concatenate([lambda_q1, lambda_k1, lambda_q2, lambda_k2], axis=0)
    attn_args = (lam_vecs, sub_norm_g[0][:, None], q_arr, k_arr, vt_arr)
    yb = lax.cond(score_bound[0, 0] <= MAX_BOUNDED_SCORE,
                  lambda args: _diff_attention(*args, bounded=True),
                  lambda args: _diff_attention(*args, bounded=False), attn_args)
    yb = yb.reshape(n_tok, D_MODEL)

    pad_rows = ROUTER_ROWS - 8 - N_EXPERTS
    wr = jnp.concatenate([w_rg[0].T, jnp.zeros((4, d), F32), w_re[0].T,
                          jnp.zeros((pad_rows, d), F32)], axis=0)
    br = jnp.concatenate([b_rg[0], jnp.full((4,), NEG_BIG, F32), b_re[0],
                          jnp.zeros((pad_rows,), F32)])[:, None]
    wr_hi = wr.astype(BF16)
    wr_lo = (wr - wr_hi.astype(F32)).astype(BF16)
    x1, h2, comb, gidx = _merge_and_route(
        u, va, yb, ga, gb, x2, w_s[0], b_s[0][:, :, None],
        jnp.concatenate([w_up_a, w_up_b, w_out], axis=0).astype(BF16),
        norm2_g, wr_hi, wr_lo, br)

    w13 = jnp.concatenate([w1[0], w3[0]], axis=2).astype(BF16)
    w2f = w2[0].astype(BF16).reshape(N_EXPERTS * D_EXPERT, D_MODEL)
    out = _moe(gidx, h2, comb, x1, w13, w2f)
    return out.reshape(bsz, seq, d)
```

```python
import functools
import math

import jax
import jax.numpy as jnp
import numpy as np
from jax import lax
from jax.experimental import pallas as pl
from jax.experimental.pallas import tpu as pltpu

F32 = jnp.float32
BF16 = jnp.bfloat16

D_MODEL = 1024
EPS = 1e-6
A_GROUPS = 8
CHUNK = 128
B_HEADS = 8
B_HEAD_DIM = 64
B_V_DIM = 128
N_GROUPS = 4
EXPERTS_PER_GROUP = 8
N_EXPERTS = 32
D_EXPERT = 256
LAM_INIT = 0.8 - 0.6 * math.exp(-0.3 * 0)

LANES = 128
MXU_DIM = 256
VMEM_BYTES = 64 * 1024 * 1024
VMEM_LIMIT_BYTES = VMEM_BYTES - 8 * 1024 * 1024
BIG_VMEM_LIMIT_BYTES = VMEM_BYTES - 1024 * 1024

NEG_BIG = -1e30
LOG2_E = math.log2(math.e)
ALIBI_SLOPES = tuple(2.0 ** (-8.0 * (h + 1) / B_HEADS) for h in range(B_HEADS))
SLOPE_PARTS = 4
VT_ROWS = B_V_DIM + 16
SCORE_BOUND_MARGIN = 1.02
MAX_BOUNDED_SCORE = 48.0

PROJ_TM = 512
ATT_TQ_BOUNDED = 2048
ATT_TQ = 1024
ATT_TK = 512
MERGE_TM = 1024
MERGE_SUB_TM = 512
MOE_TM = 1024
MOE_NE = 8
MOE_ROW_UNIT = 64
ROUTER_ROWS = 48


def _bf16_parts(value, n_parts):
    parts, rest = [], np.float64(value)
    for _ in range(n_parts):
        part = np.float64(np.asarray(rest, np.float32).astype(BF16).astype(np.float32))
        parts.append(float(part))
        rest = rest - part
    return parts


def _segment_rms_scale(acc, seg_ones_ref, seg):
    sq = (acc * acc).astype(BF16)
    parts = []
    for p in range(acc.shape[1] // MXU_DIM):
        parts.append(jnp.dot(sq[:, p * MXU_DIM:(p + 1) * MXU_DIM], seg_ones_ref[...],
                             preferred_element_type=F32))
    ss = jnp.concatenate(parts, axis=1)
    return lax.rsqrt(ss * (1.0 / seg) + EPS)


def _inproj_kernel(x_ref, g1_ref, w_ref, wg_ref, bias_ref, gv_ref, gq_ref, gk_ref, ones64_ref, ones128_ref,
                   bound_ref,
                   u_ref, va_ref, q_ref, k_ref, vt_ref, ga_ref, gb_ref, *, seq_blocks):
    i = pl.program_id(0)
    tm = x_ref.shape[0]
    x = x_ref[...]
    r = lax.rsqrt(jnp.mean(x * x, axis=-1, keepdims=True) + EPS)
    h = ((x * r) * g1_ref[...]).astype(BF16)

    n_in = w_ref.shape[1] // D_MODEL

    def slab(j):
        ref, col = (w_ref, j) if j < n_in else (wg_ref, j - n_in)
        return jnp.dot(h, ref[:, col * D_MODEL:(col + 1) * D_MODEL], preferred_element_type=F32)

    lane = lax.broadcasted_iota(jnp.int32, (1, LANES), 1)
    lo_half = lane < B_HEAD_DIM

    spare_col = lane & (B_HEAD_DIM - 1)
    pos = (i % seq_blocks) * tm + lax.broadcasted_iota(jnp.int32, (tm, LANES), 0)
    pos_lo = (pos & (MXU_DIM - 1)).astype(F32)
    pos_hi = (pos - (pos & (MXU_DIM - 1))).astype(F32)
    pos_cols = jnp.where(spare_col & SLOPE_PARTS == 0, pos_lo, pos_hi)
    bound = bound_ref[...]
    bound_hi = bound.astype(BF16).astype(F32)
    bound_mid = (bound - bound_hi).astype(BF16).astype(F32)
    bound_lo = (bound - bound_hi - bound_mid).astype(BF16).astype(F32)
    bound_cols = jnp.where(spare_col == 4 * SLOPE_PARTS, bound_hi,
                           jnp.where(spare_col == 4 * SLOPE_PARTS + 1, bound_mid,
                                     jnp.where(spare_col == 4 * SLOPE_PARTS + 2, bound_lo, 0.0)))
    q_shared = jnp.where((spare_col >= 2 * SLOPE_PARTS) & (spare_col < 4 * SLOPE_PARTS),
                         pos_cols, bound_cols)
    k_shared = jnp.where(spare_col < 2 * SLOPE_PARTS, pos_cols, 0.0)

    def slope_cols(hd, sign):
        parts = _bf16_parts(ALIBI_SLOPES[hd] * LOG2_E, SLOPE_PARTS)
        cvec = jnp.zeros((1, LANES), F32)
        for n, part in enumerate(parts):
            cvec = jnp.where(spare_col & (SLOPE_PARTS - 1) == n, sign * part, cvec)
        return cvec

    def store_u(acc):
        u_ref[...] = acc.astype(BF16)

    def store_va(acc):
        r = _segment_rms_scale(acc, ones128_ref, LANES)
        va_ref[...] = ((acc * r) * gv_ref[...]).astype(BF16)

    def store_q(acc):
        r = _segment_rms_scale(acc, ones64_ref, B_HEAD_DIM)
        qn = (acc * r) * gq_ref[...]
        for hd in range(B_HEADS):
            extra = jnp.where(spare_col < 2 * SLOPE_PARTS, slope_cols(hd, 1.0), q_shared)
            blk = qn[:, hd * LANES:(hd + 1) * LANES]
            q_ref[0, 2 * hd] = jnp.where(lo_half, blk, extra).astype(BF16)
            q_ref[0, 2 * hd + 1] = jnp.where(lo_half, extra, blk).astype(BF16)

    def store_k(acc):
        r = _segment_rms_scale(acc, ones64_ref, B_HEAD_DIM)
        kn = (acc * r) * gk_ref[...]
        for hd in range(B_HEADS):
            consts = jnp.where(spare_col < 4 * SLOPE_PARTS, slope_cols(hd, -1.0),
                               jnp.where(spare_col < 4 * SLOPE_PARTS + 3, -1.0, 0.0))
            extra = jnp.where(spare_col < 2 * SLOPE_PARTS, k_shared, consts)
            blk = kn[:, hd * LANES:(hd + 1) * LANES]
            k_ref[0, 2 * hd] = jnp.where(lo_half, blk, extra).astype(BF16)
            k_ref[0, 2 * hd + 1] = jnp.where(lo_half, extra, blk).astype(BF16)

    def store_vt(acc):
        for hd in range(B_HEADS):
            vt_ref[0, hd, 0:B_V_DIM, :] = acc[:, hd * LANES:(hd + 1) * LANES].T.astype(BF16)
            vt_ref[0, hd, B_V_DIM:VT_ROWS, :] = jnp.ones((VT_ROWS - B_V_DIM, tm), BF16)

    def store_ga(acc):
        ga_ref[...] = (1.0 / (1.0 + jnp.exp(-(acc + bias_ref[:, 0:D_MODEL])))).astype(BF16)

    def store_gb(acc):
        gb_ref[...] = (1.0 / (1.0 + jnp.exp(-(acc + bias_ref[:, D_MODEL:])))).astype(BF16)

    for j, store in enumerate((store_u, store_va, store_q, store_k, store_vt, store_ga, store_gb)):
        store(slab(j))


def _input_projection(x2, g1, w_in, w_gate, b_gate, gv, gq, gk, ones64, ones128, bound, bsz, seq):
    n_tok = x2.shape[0]
    tm = PROJ_TM
    seq_blocks = seq // tm
    row_spec = pl.BlockSpec((tm, D_MODEL), lambda i: (i, 0))
    resident = lambda shape: pl.BlockSpec(shape, lambda i: (0,) * len(shape),
                                          pipeline_mode=pl.Buffered(1))
    head_map = lambda i: (i // seq_blocks, 0, i % seq_blocks, 0)
    tok_bf16 = jax.ShapeDtypeStruct((n_tok, D_MODEL), BF16)
    return pl.pallas_call(
        functools.partial(_inproj_kernel, seq_blocks=seq_blocks),
        grid=(n_tok // tm,),
        in_specs=[
            row_spec,
            resident((1, D_MODEL)),
            resident(w_in.shape), resident(w_gate.shape),
            resident(b_gate.shape),
            resident((1, D_MODEL)), resident((1, D_MODEL)), resident((1, D_MODEL)),
            resident((MXU_DIM, MXU_DIM)), resident((MXU_DIM, MXU_DIM)),
            resident((1, 1)),
        ],
        out_specs=[
            row_spec, row_spec,
            pl.BlockSpec((1, 2 * B_HEADS, tm, LANES), head_map),
            pl.BlockSpec((1, 2 * B_HEADS, tm, LANES), head_map),
            pl.BlockSpec((1, B_HEADS, VT_ROWS, tm), lambda i: (i // seq_blocks, 0, 0, i % seq_blocks)),
            row_spec, row_spec,
        ],
        out_shape=[
            tok_bf16, tok_bf16,
            jax.ShapeDtypeStruct((bsz, 2 * B_HEADS, seq, LANES), BF16),
            jax.ShapeDtypeStruct((bsz, 2 * B_HEADS, seq, LANES), BF16),
            jax.ShapeDtypeStruct((bsz, B_HEADS, VT_ROWS, seq), BF16),
            tok_bf16, tok_bf16,
        ],
        compiler_params=pltpu.CompilerParams(
            dimension_semantics=("arbitrary",),
            vmem_limit_bytes=VMEM_LIMIT_BYTES),
        name="input_projection",
    )(x2, g1, w_in, w_gate, b_gate, gv, gq, gk, ones64, ones128, bound)


def _attn_kernel(lam_ref, sg_ref, q_ref, k_ref, vt_ref, o_ref, acc_scr, sa_scr, sb_scr):
    qi = pl.program_id(2)
    tq = q_ref.shape[2]
    tk = ATT_TK
    nt_dims = (((1,), (1,)), ((), ()))

    assert tq == 2 * tk
    tri = (lax.broadcasted_iota(jnp.int32, (tk, tk), 0)
           <= lax.broadcasted_iota(jnp.int32, (tk, tk), 1))

    def scores(kb, s_scr, diag=None):
        ks = pl.multiple_of(kb * tk, tk)
        maxima = []
        for c in range(2):
            k_blk = k_ref[0, c, pl.ds(ks, tk), :]
            if diag is None:
                s = lax.dot_general(k_blk, q_ref[0, c], nt_dims, preferred_element_type=F32)
                s_scr[c] = s
                maxima.append(jnp.max(s, axis=0, keepdims=True))
                continue
            parts = []
            for half in range(diag, 2):
                cols = slice(half * tk, (half + 1) * tk)
                s = lax.dot_general(k_blk, q_ref[0, c, cols, :], nt_dims,
                                    preferred_element_type=F32)
                if half == diag:
                    s = jnp.where(tri, s, NEG_BIG)
                s_scr[c, :, cols] = s
                parts.append(jnp.max(s, axis=0, keepdims=True))
            maxima.append(parts[0] if len(parts) == 1 else jnp.concatenate(parts, axis=1))
        return tuple(maxima)

    def consume(kb, s_scr, m_old, blk_max, col0=0, first=False):
        ks = pl.multiple_of(kb * tk, tk)
        vt = vt_ref[0, 0, :, pl.ds(ks, tk)]
        m_new = []
        for c in range(2):
            m_c = blk_max[c] if first else jnp.maximum(m_old[c][:, col0:], blk_max[c])
            p = jnp.exp2(s_scr[c, :, col0:] - m_c).astype(BF16)
            pv = jnp.dot(vt, p, preferred_element_type=F32)
            cols = slice(c * tq + col0, (c + 1) * tq)
            if first:
                acc_scr[:, cols] = pv
            else:
                acc_scr[:, cols] = acc_scr[:, cols] * jnp.exp2(m_old[c][:, col0:] - m_c) + pv
            m_new.append(m_c)
        return tuple(m_new)

    max_a = scores(2 * qi, sa_scr, diag=0)
    max_b = scores(2 * qi + 1, sb_scr, diag=1)
    m = consume(2 * qi, sa_scr, None, max_a, first=True)
    max_a = scores(0, sa_scr)
    m_right = consume(2 * qi + 1, sb_scr, m, max_b, col0=tk)
    m = tuple(jnp.concatenate([m[c][:, :tk], m_right[c]], axis=1) for c in range(2))

    def trip(t, carry):
        m, max_a = carry
        max_b = scores(2 * t + 1, sb_scr)
        m = consume(2 * t, sa_scr, m, max_a)
        max_a = scores(2 * t + 2, sa_scr)
        m = consume(2 * t + 1, sb_scr, m, max_b)
        return m, max_a

    m, max_a = lax.fori_loop(0, qi - 1, trip, (m, max_a))

    @pl.when(qi > 0)
    def _():
        max_b = scores(2 * qi - 1, sb_scr)
        m_mid = consume(2 * qi - 2, sa_scr, m, max_a)
        consume(2 * qi - 1, sb_scr, m_mid, max_b)

    _attn_finalize(lam_ref, sg_ref, acc_scr, o_ref)


def _attn_finalize(lam_ref, sg_ref, acc_scr, o_ref):
    tq = o_ref.shape[1]
    lam_v = lam_ref[...]
    lam = (jnp.exp(jnp.sum(lam_v[0:1] * lam_v[1:2], axis=1, keepdims=True))
           - jnp.exp(jnp.sum(lam_v[2:3] * lam_v[3:4], axis=1, keepdims=True)) + LAM_INIT)
    o0 = acc_scr[0:B_V_DIM, 0:tq] / acc_scr[B_V_DIM:B_V_DIM + 1, 0:tq]
    o1 = acc_scr[0:B_V_DIM, tq:2 * tq] / acc_scr[B_V_DIM:B_V_DIM + 1, tq:2 * tq]
    o = o0 - lam * o1
    r = lax.rsqrt(jnp.mean(o * o, axis=0, keepdims=True) + EPS)
    o = ((o * r) * sg_ref[...]) * (1.0 - LAM_INIT)
    o_ref[0] = o.T.astype(BF16)


def _attn_bounded_kernel(lam_ref, sg_ref, q_ref, k_ref, vt_ref, o_ref, acc_scr):
    qi = pl.program_id(2)
    tq = q_ref.shape[2]
    tk = ATT_TK
    nt_dims = (((1,), (1,)), ((), ()))
    tri = (lax.broadcasted_iota(jnp.int32, (tk, tk), 0)
           <= lax.broadcasted_iota(jnp.int32, (tk, tk), 1))

    def fold(key_start, n_keys, q_cols, masked, first):
        vt = vt_ref[0, 0, 0:B_V_DIM, pl.ds(key_start, n_keys)]
        for c in range(2):
            e = lax.dot_general(k_ref[0, c, pl.ds(key_start, n_keys), :], q_ref[0, c, q_cols, :],
                                nt_dims, preferred_element_type=F32)
            if masked:
                e_diag = jnp.where(tri, e[n_keys - tk:, :], NEG_BIG)
                e = e_diag if n_keys == tk else jnp.concatenate([e[:n_keys - tk, :], e_diag], axis=0)
            p = jnp.exp2(e)
            p_sum = jnp.sum(p, axis=0, keepdims=True)
            pv = jnp.dot(vt, p.astype(BF16), preferred_element_type=F32)
            cols = slice(c * tq + q_cols.start, c * tq + q_cols.stop)
            if first:
                acc_scr[0:B_V_DIM, cols] = pv
                acc_scr[B_V_DIM:B_V_DIM + 1, cols] = p_sum
            else:
                acc_scr[0:B_V_DIM, cols] += pv
                acc_scr[B_V_DIM:B_V_DIM + 1, cols] += p_sum

    strips = [slice(j * tk, (j + 1) * tk) for j in range(tq // tk)]
    diag0 = pl.multiple_of(qi * tq, tq)
    for j, strip in enumerate(strips):
        fold(diag0, (j + 1) * tk, strip, True, True)

    def trip(t, carry):
        for strip in strips:
            fold(pl.multiple_of(t * tq, tq), tq, strip, False, False)
        return carry

    lax.fori_loop(0, qi, trip, 0)
    _attn_finalize(lam_ref, sg_ref, acc_scr, o_ref)


def _diff_attention(lam_vecs, sub_g, q_arr, k_arr, vt_arr, bounded):
    bsz, _, seq, _ = q_arr.shape
    tq = ATT_TQ_BOUNDED if bounded else ATT_TQ
    scratch = [pltpu.VMEM((VT_ROWS, 2 * tq), F32)]
    if not bounded:
        scratch += [pltpu.VMEM((2, ATT_TK, tq), F32), pltpu.VMEM((2, ATT_TK, tq), F32)]
    return pl.pallas_call(
        _attn_bounded_kernel if bounded else _attn_kernel,
        grid=(bsz, B_HEADS, seq // tq),
        in_specs=[
            pl.BlockSpec((4, B_HEAD_DIM), lambda b, h, qi: (0, 0)),
            pl.BlockSpec((B_V_DIM, 1), lambda b, h, qi: (0, 0)),
            pl.BlockSpec((1, 2, tq, LANES), lambda b, h, qi: (b, h, qi, 0)),
            pl.BlockSpec((1, 2, seq, LANES), lambda b, h, qi: (b, h, 0, 0)),
            pl.BlockSpec((1, 1, VT_ROWS, seq), lambda b, h, qi: (b, h, 0, 0)),
        ],
        out_specs=pl.BlockSpec((1, tq, B_V_DIM), lambda b, h, qi: (b, qi, h)),
        out_shape=jax.ShapeDtypeStruct((bsz, seq, B_HEADS * B_V_DIM), BF16),
        scratch_shapes=scratch,
        compiler_params=pltpu.CompilerParams(
            dimension_semantics=("arbitrary", "arbitrary", "arbitrary"),
            vmem_limit_bytes=VMEM_LIMIT_BYTES),
        name="diff_attention_bounded" if bounded else "diff_attention",
    )(lam_vecs, sub_g, q_arr, k_arr, vt_arr)


def _first_argmax(vals, iota, top):
    return jnp.min(jnp.where(vals == top, iota, vals.shape[0]), axis=0, keepdims=True)


def _merge_kernel(u_ref, va_ref, yb_ref, ga_ref, gb_ref, x_ref, ws_ref, bs_ref,
                  w3_ref, g2_ref, wr_hi_ref, wr_lo_ref, br_ref,
                  x1_ref, h2_ref, comb_ref, gidx_ref):
    tm = x_ref.shape[0]
    row = lax.broadcasted_iota(jnp.int32, (CHUNK, CHUNK), 0)
    col = lax.broadcasted_iota(jnp.int32, (CHUNK, CHUNK), 1)
    causal = row >= col

    w_tril = [jnp.where(causal, ws_ref[g], 0.0).astype(BF16) for g in range(A_GROUPS)]
    nt_dims = (((1,), (1,)), ((), ()))

    def gate_a(rows):
        n_chunks = (rows.stop - rows.start) // CHUNK
        group_cols = []
        for g in range(A_GROUPS):
            gl = slice(g * LANES, (g + 1) * LANES)
            v_g = jnp.concatenate(
                [va_ref[rows.start + c * CHUNK:rows.start + (c + 1) * CHUNK, gl]
                 for c in range(n_chunks)], axis=1)
            z = jnp.dot(w_tril[g], v_g, preferred_element_type=F32) + bs_ref[g]
            z = jnp.concatenate([z[:, c * LANES:(c + 1) * LANES] for c in range(n_chunks)], axis=0)
            group_cols.append((u_ref[rows, gl].astype(F32) * z).astype(BF16))
        return jnp.concatenate(group_cols, axis=1)

    def merge(rows, y_a):
        up_a = jnp.dot(y_a, w3_ref[0], preferred_element_type=F32)
        up_b = jnp.dot(yb_ref[rows, :], w3_ref[1], preferred_element_type=F32)
        merged = ga_ref[rows, :].astype(F32) * up_a + gb_ref[rows, :].astype(F32) * up_b
        return merged.astype(BF16)

    def project_and_norm(rows, merged):
        x1 = x_ref[rows, :] + jnp.dot(merged, w3_ref[2], preferred_element_type=F32)
        x1_ref[rows, :] = x1
        r = lax.rsqrt(jnp.mean(x1 * x1, axis=-1, keepdims=True) + EPS)
        h2 = (x1 * r) * g2_ref[...]
        h2_hi = h2.astype(BF16)
        h2_ref[rows, :] = h2_hi
        return h2_hi, (h2 - h2_hi.astype(F32)).astype(BF16)

    def route(rows, h2_parts):
        h2_hi, h2_lo = h2_parts
        tm = rows.stop - rows.start
        lt = (lax.dot_general(wr_hi_ref[...], h2_hi, nt_dims, preferred_element_type=F32)
              + lax.dot_general(wr_hi_ref[...], h2_lo, nt_dims, preferred_element_type=F32)
              + lax.dot_general(wr_lo_ref[...], h2_hi, nt_dims, preferred_element_type=F32)
              + br_ref[...])

        iota8 = lax.broadcasted_iota(jnp.int32, (8, tm), 0)
        gl = lt[0:8]
        ge = jnp.exp(gl - jnp.max(gl, axis=0, keepdims=True))
        gp = ge / jnp.sum(ge, axis=0, keepdims=True)
        g_gate = jnp.max(gp, axis=0, keepdims=True)
        g_idx = _first_argmax(gp, iota8, g_gate)

        el = jnp.zeros((EXPERTS_PER_GROUP, tm), F32)
        for gi in range(N_GROUPS):
            el = jnp.where(g_idx == gi, lt[8 + 8 * gi:16 + 8 * gi], el)
        ee = jnp.exp(el - jnp.max(el, axis=0, keepdims=True))
        ep = ee / jnp.sum(ee, axis=0, keepdims=True)
        p_top1 = jnp.max(ep, axis=0, keepdims=True)
        i_top1 = _first_argmax(ep, iota8, p_top1)
        ep_rest = jnp.where(iota8 == i_top1, -1.0, ep)
        p_top2 = jnp.max(ep_rest, axis=0, keepdims=True)
        i_top2 = _first_argmax(ep_rest, iota8, p_top2)
        den = p_top1 + p_top2
        w_top1 = g_gate * (p_top1 / den)
        w_top2 = g_gate * (p_top2 / den)
        in_group = (jnp.where(iota8 == i_top1, w_top1, 0.0)
                    + jnp.where(iota8 == i_top2, w_top2, 0.0))
        blocks = [jnp.where(g_idx == gi, in_group, 0.0) for gi in range(N_GROUPS)]
        blocks.append(jnp.zeros((LANES - N_EXPERTS, tm), F32))
        comb_t = jnp.concatenate(blocks, axis=0)
        comb_ref[rows, :] = comb_t.T
        gidx_ref[:, rows] = jnp.broadcast_to(g_idx, (gidx_ref.shape[0], tm))

    stages = (gate_a, merge, project_and_norm, route)
    subs = [slice(s0, s0 + MERGE_SUB_TM) for s0 in range(0, tm, MERGE_SUB_TM)]
    state = [None] * len(subs)
    for step in range(len(subs) + len(stages) - 1):
        for si, rows in enumerate(subs):
            stage = step - si
            if 0 <= stage < len(stages):
                state[si] = stages[stage](rows, state[si]) if stage else stages[0](rows)


def _merge_and_route(u, va, yb, ga, gb, x2, ws, bs, w_up_a_up_b_out, g2, wr_hi, wr_lo, br):
    n_tok = x2.shape[0]
    tm = MERGE_TM
    row_spec = pl.BlockSpec((tm, D_MODEL), lambda i: (i, 0))
    full = lambda shape: pl.BlockSpec(shape, lambda i: (0,) * len(shape),
                                      pipeline_mode=pl.Buffered(1))
    return pl.pallas_call(
        _merge_kernel,
        grid=(n_tok // tm,),
        in_specs=[
            row_spec, row_spec, row_spec, row_spec, row_spec, row_spec,
            full((A_GROUPS, CHUNK, CHUNK)), full((A_GROUPS, CHUNK, 1)),
            full((3, D_MODEL, D_MODEL)),
            full((1, D_MODEL)),
            full((ROUTER_ROWS, D_MODEL)), full((ROUTER_ROWS, D_MODEL)), full((ROUTER_ROWS, 1)),
        ],
        out_specs=[row_spec, row_spec, pl.BlockSpec((tm, LANES), lambda i: (i, 0)),
                   pl.BlockSpec((8, tm), lambda i: (0, i))],
        out_shape=[
            jax.ShapeDtypeStruct((n_tok, D_MODEL), F32),
            jax.ShapeDtypeStruct((n_tok, D_MODEL), BF16),
            jax.ShapeDtypeStruct((n_tok, LANES), F32),
            jax.ShapeDtypeStruct((8, n_tok), jnp.int32),
        ],
        compiler_params=pltpu.CompilerParams(
            dimension_semantics=("arbitrary",),
            vmem_limit_bytes=BIG_VMEM_LIMIT_BYTES),
        name="merge_and_route",
    )(u, va, yb, ga, gb, x2, ws, bs, w_up_a_up_b_out, g2, wr_hi, wr_lo, br)


def _moe_kernel(g_ref, h_ref, c_ref, x1_ref, w1_ref, w3_ref, w2_ref, o_ref,
                p_scr, xg_scr, cg_scr, y_scr, bounds_ref):
    e = pl.program_id(1)
    n_steps = pl.num_programs(1)
    tt = h_ref.shape[0]
    rows = p_scr.shape[0]
    unit = MOE_ROW_UNIT
    rb = 2 * unit

    @pl.when(e == 0)
    def _():
        g_idx = g_ref[0:1, :]
        slot = lax.broadcasted_iota(jnp.int32, (16, tt), 0)
        onehot = jnp.where(slot == g_idx, 1.0, 0.0)
        upper = (lax.broadcasted_iota(jnp.int32, (LANES, LANES), 0)
                 <= lax.broadcasted_iota(jnp.int32, (LANES, LANES), 1)).astype(BF16)
        carry = jnp.zeros((16, 1), F32)
        counts = []
        for j in range(tt // LANES):
            seg = onehot[:, j * LANES:(j + 1) * LANES].astype(BF16)
            cs = jnp.dot(seg, upper, preferred_element_type=F32) + carry
            counts.append(cs)
            carry = cs[:, LANES - 1:LANES]
        running = jnp.concatenate(counts, axis=1)
        n_units = jnp.floor((carry + (unit - 1)) * (1.0 / unit))
        slot_col = lax.broadcasted_iota(jnp.int32, (16, 1), 0)
        first_unit = jnp.zeros((16, 1), F32)
        for g in range(N_GROUPS - 1):
            first_unit = first_unit + jnp.where(slot_col > g, n_units[g:g + 1, :], 0.0)
        for g in range(N_GROUPS):
            bounds_ref[g] = first_unit[g:g + 1, :][0, 0].astype(jnp.int32)
            bounds_ref[N_GROUPS + g] = n_units[g:g + 1, :][0, 0].astype(jnp.int32)
        pos = jnp.sum(onehot * (first_unit * unit + running - 1.0), axis=0, keepdims=True)
        pos = pos.astype(jnp.int32)

        c = c_ref[...]
        c_hi = c.astype(BF16)
        c_rest = c - c_hi.astype(F32)
        c_mid = c_rest.astype(BF16)
        c_lo = (c_rest - c_mid.astype(F32)).astype(BF16)
        h_ext = jnp.concatenate([h_ref[...], c_hi, c_mid, c_lo], axis=1)
        for ch in range(rows // MXU_DIM):
            sl = slice(ch * MXU_DIM, (ch + 1) * MXU_DIM)
            row_id = lax.broadcasted_iota(jnp.int32, (MXU_DIM, tt), 0) + ch * MXU_DIM
            perm = jnp.where(row_id == pos, 1.0, 0.0).astype(BF16)
            p_scr[sl, :] = perm
            got = jnp.dot(perm, h_ext, preferred_element_type=F32)
            xg_scr[sl, :] = got[:, :D_MODEL].astype(BF16)
            cg_scr[sl, :] = (got[:, D_MODEL:D_MODEL + LANES]
                             + got[:, D_MODEL + LANES:D_MODEL + 2 * LANES]
                             + got[:, D_MODEL + 2 * LANES:])
        y_scr[...] = jnp.zeros_like(y_scr)

    group = e // (EXPERTS_PER_GROUP // MOE_NE)
    to_front = (LANES - e * MOE_NE) & (LANES - 1)

    def row_block(r0, n_rows):
        r0 = pl.multiple_of(r0, unit)
        xb = xg_scr[pl.ds(r0, n_rows), :]
        cb = pltpu.roll(cg_scr[pl.ds(r0, n_rows), :], to_front, axis=1)
        hids = []
        for n in range(MOE_NE):
            a = jnp.dot(xb, w1_ref[n], preferred_element_type=F32)
            b3 = jnp.dot(xb, w3_ref[n], preferred_element_type=F32)
            hid = (a * (1.0 / (1.0 + jnp.exp(-a)))) * b3
            hids.append((hid * cb[:, n:n + 1]).astype(BF16))
        hid_all = jnp.concatenate(hids, axis=1)
        y_scr[pl.ds(r0, n_rows), :] = jnp.dot(hid_all, w2_ref[...],
                                              preferred_element_type=F32).astype(BF16)

    first_row = bounds_ref[group] * unit
    n_units = bounds_ref[N_GROUPS + group]
    n_blocks = n_units // 2

    def pair(i, carry):
        row_block(first_row + (2 * i) * rb, rb)
        row_block(first_row + (2 * i + 1) * rb, rb)
        return carry

    lax.fori_loop(0, n_blocks // 2, pair, 0)

    @pl.when(n_blocks % 2 == 1)
    def _():
        row_block(first_row + (n_blocks - 1) * rb, rb)

    @pl.when(n_units % 2 == 1)
    def _():
        row_block(first_row + n_blocks * rb, unit)

    @pl.when(e == n_steps - 1)
    def _():
        tn_dims = (((0,), (0,)), ((), ()))
        y = lax.dot_general(p_scr[...], y_scr[...], tn_dims,
                            preferred_element_type=F32)
        o_ref[...] = x1_ref[...] + y


def _moe(gidx, h2, comb, x1, w1, w3, w2f):
    assert MOE_NE == EXPERTS_PER_GROUP, "each row block is written by exactly one grid step"
    n_tok = h2.shape[0]
    tt = MOE_TM
    rows = tt + N_GROUPS * MOE_ROW_UNIT
    assert rows % MXU_DIM == 0
    return pl.pallas_call(
        _moe_kernel,
        grid=(n_tok // tt, N_EXPERTS // MOE_NE),
        in_specs=[
            pl.BlockSpec((8, tt), lambda i, e: (0, i)),
            pl.BlockSpec((tt, D_MODEL), lambda i, e: (i, 0)),
            pl.BlockSpec((tt, LANES), lambda i, e: (i, 0)),
            pl.BlockSpec((tt, D_MODEL), lambda i, e: (i, 0)),
            pl.BlockSpec((MOE_NE, D_MODEL, D_EXPERT), lambda i, e: (e, 0, 0)),
            pl.BlockSpec((MOE_NE, D_MODEL, D_EXPERT), lambda i, e: (e, 0, 0)),
            pl.BlockSpec((MOE_NE * D_EXPERT, D_MODEL), lambda i, e: (e, 0)),
        ],
        out_specs=pl.BlockSpec((tt, D_MODEL), lambda i, e: (i, 0)),
        out_shape=jax.ShapeDtypeStruct((n_tok, D_MODEL), F32),
        scratch_shapes=[
            pltpu.VMEM((rows, tt), BF16),
            pltpu.VMEM((rows, D_MODEL), BF16),
            pltpu.VMEM((rows, LANES), F32),
            pltpu.VMEM((rows, D_MODEL), BF16),
            pltpu.SMEM((2 * N_GROUPS,), jnp.int32),
        ],
        compiler_params=pltpu.CompilerParams(
            dimension_semantics=("arbitrary", "arbitrary"),
            vmem_limit_bytes=BIG_VMEM_LIMIT_BYTES),
        name="moe_experts",
    )(gidx, h2, comb, x1, w1, w3, w2f)


def _segment_ones(seg):
    idx = jnp.arange(MXU_DIM) // seg
    return (idx[:, None] == idx[None, :]).astype(BF16)


def kernel(x, norm1_g, w_in, v_norm_g, w_s, b_s, q_norm_g, k_norm_g, lambda_q1, lambda_k1, lambda_q2, lambda_k2, sub_norm_g, w_up_a, w_up_b, w_gate, b_gate, w_out, norm2_g, w_rg, b_rg, w_re, b_re, w1, w3, w2):
    bsz, seq, d = x.shape
    assert d == D_MODEL and seq % PROJ_TM == 0 and seq % ATT_TQ == 0 and ATT_TQ == 2 * ATT_TK
    assert seq % ATT_TQ_BOUNDED == 0 and ATT_TQ_BOUNDED % ATT_TK == 0
    assert norm1_g.shape[0] == 1, "single layer"
    n_tok = bsz * seq
    assert n_tok % MOE_TM == 0 and n_tok % MERGE_TM == 0
    x2 = x.reshape(n_tok, d)

    gv = v_norm_g[0].reshape(1, D_MODEL)
    gq = jnp.tile(q_norm_g[0] * (B_HEAD_DIM ** -0.5 * LOG2_E), 2 * B_HEADS)[None, :]
    gk = jnp.tile(k_norm_g[0], 2 * B_HEADS)[None, :]
    score_bound = (SCORE_BOUND_MARGIN * B_HEAD_DIM ** 0.5 * LOG2_E
                   * jnp.max(jnp.abs(q_norm_g[0] * k_norm_g[0]))).reshape(1, 1)
    u, va, q_arr, k_arr, vt_arr, ga, gb = _input_projection(
        x2, norm1_g, w_in[0].astype(BF16), w_gate[0].astype(BF16), b_gate, gv, gq, gk, _segment_ones(B_HEAD_DIM), _segment_ones(LANES),
        score_bound, bsz, seq)

    lam_vecs = jnp.concatenate([lambda_q1, lambda_k1, lambda_q2, lambda_k2], axis=0)
    attn_args = (lam_vecs, sub_norm_g[0][:, None], q_arr, k_arr, vt_arr)
    yb = lax.cond(score_bound[0, 0] <= MAX_BOUNDED_SCORE,
                  lambda args: _diff_attention(*args, bounded=True),
                  lambda args: _diff_attention(*args, bounded=False), attn_args)
    yb = yb.reshape(n_tok, D_MODEL)

    pad_rows = ROUTER_ROWS - 8 - N_EXPERTS
    wr = jnp.concatenate([w_rg[0].T, jnp.zeros((4, d), F32), w_re[0].T,
                          jnp.zeros((pad_rows, d), F32)], axis=0)
    br = jnp.concatenate([b_rg[0], jnp.full((4,), NEG_BIG, F32), b_re[0],
                          jnp.zeros((pad_rows,), F32)])[:, None]
    wr_hi = wr.astype(BF16)
    wr_lo = (wr - wr_hi.astype(F32)).astype(BF16)
    x1, h2, comb, gidx = _merge_and_route(
        u, va, yb, ga, gb, x2, w_s[0], b_s[0][:, :, None],
        jnp.concatenate([w_up_a, w_up_b, w_out], axis=0).astype(BF16),
        norm2_g, wr_hi, wr_lo, br)

    w2f = w2[0].astype(BF16).reshape(N_EXPERTS * D_EXPERT, D_MODEL)
    out = _moe(gidx, h2, comb, x1, w1[0].astype(BF16), w3[0].astype(BF16), w2f)
    return out.reshape(bsz, seq, d)
```

```python
import functools
import math

import jax
import jax.numpy as jnp
import numpy as np
from jax import lax
from jax.experimental import pallas as pl
from jax.experimental.pallas import tpu as pltpu

F32 = jnp.float32
BF16 = jnp.bfloat16

D_MODEL = 1024
EPS = 1e-6
A_GROUPS = 8
CHUNK = 128
B_HEADS = 8
B_HEAD_DIM = 64
B_V_DIM = 128
N_GROUPS = 4
EXPERTS_PER_GROUP = 8
N_EXPERTS = 32
D_EXPERT = 256
LAM_INIT = 0.8 - 0.6 * math.exp(-0.3 * 0)

LANES = 128
MXU_DIM = 256
VMEM_BYTES = 64 * 1024 * 1024
VMEM_LIMIT_BYTES = VMEM_BYTES - 8 * 1024 * 1024
BIG_VMEM_LIMIT_BYTES = VMEM_BYTES - 1024 * 1024

NEG_BIG = -1e30
LOG2_E = math.log2(math.e)
ALIBI_SLOPES = tuple(2.0 ** (-8.0 * (h + 1) / B_HEADS) for h in range(B_HEADS))
SLOPE_PARTS = 4
VT_ROWS = B_V_DIM + 16
SCORE_BOUND_MARGIN = 1.02
MAX_BOUNDED_SCORE = 48.0

PROJ_TM = 512
ATT_TQ_BOUNDED = 2048
ATT_TQ = 1024
ATT_TK = 512
MERGE_TM = 1024
MERGE_SUB_TM = 512
MOE_TM = 1024
MOE_NE = 8
MOE_ROW_UNIT = 64
ROUTER_ROWS = 48


def _bf16_parts(value, n_parts):
    parts, rest = [], np.float64(value)
    for _ in range(n_parts):
        part = np.float64(np.asarray(rest, np.float32).astype(BF16).astype(np.float32))
        parts.append(float(part))
        rest = rest - part
    return parts


def _segment_rms_scale(acc, seg_ones_ref, seg):
    sq = (acc * acc).astype(BF16)
    parts = []
    for p in range(acc.shape[1] // MXU_DIM):
        parts.append(jnp.dot(sq[:, p * MXU_DIM:(p + 1) * MXU_DIM], seg_ones_ref[...],
                             preferred_element_type=F32))
    ss = jnp.concatenate(parts, axis=1)
    return lax.rsqrt(ss * (1.0 / seg) + EPS)


def _inproj_kernel(x_ref, g1_ref, w_ref, wg_ref, bias_ref, gv_ref, gq_ref, gk_ref, ones64_ref, ones128_ref,
                   bound_ref,
                   u_ref, va_ref, q_ref, k_ref, vt_ref, ga_ref, gb_ref, *, seq_blocks):
    i = pl.program_id(0)
    tm = x_ref.shape[0]
    x = x_ref[...]
    r = lax.rsqrt(jnp.mean(x * x, axis=-1, keepdims=True) + EPS)
    h = ((x * r) * g1_ref[...]).astype(BF16)

    n_in = w_ref.shape[1] // D_MODEL

    def slab(j):
        ref, col = (w_ref, j) if j < n_in else (wg_ref, j - n_in)
        return jnp.dot(h, ref[:, col * D_MODEL:(col + 1) * D_MODEL], preferred_element_type=F32)

    lane = lax.broadcasted_iota(jnp.int32, (1, LANES), 1)
    lo_half = lane < B_HEAD_DIM

    spare_col = lane & (B_HEAD_DIM - 1)
    pos = (i % seq_blocks) * tm + lax.broadcasted_iota(jnp.int32, (tm, LANES), 0)
    pos_lo = (pos & (MXU_DIM - 1)).astype(F32)
    pos_hi = (pos - (pos & (MXU_DIM - 1))).astype(F32)
    pos_cols = jnp.where(spare_col & SLOPE_PARTS == 0, pos_lo, pos_hi)
    bound = bound_ref[...]
    bound_hi = bound.astype(BF16).astype(F32)
    bound_mid = (bound - bound_hi).astype(BF16).astype(F32)
    bound_lo = (bound - bound_hi - bound_mid).astype(BF16).astype(F32)
    bound_cols = jnp.where(spare_col == 4 * SLOPE_PARTS, bound_hi,
                           jnp.where(spare_col == 4 * SLOPE_PARTS + 1, bound_mid,
                                     jnp.where(spare_col == 4 * SLOPE_PARTS + 2, bound_lo, 0.0)))
    q_shared = jnp.where((spare_col >= 2 * SLOPE_PARTS) & (spare_col < 4 * SLOPE_PARTS),
                         pos_cols, bound_cols)
    k_shared = jnp.where(spare_col < 2 * SLOPE_PARTS, pos_cols, 0.0)

    def slope_cols(hd, sign):
        parts = _bf16_parts(ALIBI_SLOPES[hd] * LOG2_E, SLOPE_PARTS)
        cvec = jnp.zeros((1, LANES), F32)
        for n, part in enumerate(parts):
            cvec = jnp.where(spare_col & (SLOPE_PARTS - 1) == n, sign * part, cvec)
        return cvec

    def store_u(acc):
        u_ref[...] = acc.astype(BF16)

    def store_va(acc):
        r = _segment_rms_scale(acc, ones128_ref, LANES)
        va_ref[...] = ((acc * r) * gv_ref[...]).astype(BF16)

    def store_q(acc):
        r = _segment_rms_scale(acc, ones64_ref, B_HEAD_DIM)
        qn = (acc * r) * gq_ref[...]
        for hd in range(B_HEADS):
            extra = jnp.where(spare_col < 2 * SLOPE_PARTS, slope_cols(hd, 1.0), q_shared)
            blk = qn[:, hd * LANES:(hd + 1) * LANES]
            q_ref[0, 2 * hd] = jnp.where(lo_half, blk, extra).astype(BF16)
            q_ref[0, 2 * hd + 1] = jnp.where(lo_half, extra, blk).astype(BF16)

    def store_k(acc):
        r = _segment_rms_scale(acc, ones64_ref, B_HEAD_DIM)
        kn = (acc * r) * gk_ref[...]
        for hd in range(B_HEADS):
            consts = jnp.where(spare_col < 4 * SLOPE_PARTS, slope_cols(hd, -1.0),
                               jnp.where(spare_col < 4 * SLOPE_PARTS + 3, -1.0, 0.0))
            extra = jnp.where(spare_col < 2 * SLOPE_PARTS, k_shared, consts)
            blk = kn[:, hd * LANES:(hd + 1) * LANES]
            k_ref[0, 2 * hd] = jnp.where(lo_half, blk, extra).astype(BF16)
            k_ref[0, 2 * hd + 1] = jnp.where(lo_half, extra, blk).astype(BF16)

    def store_vt(acc):
        for hd in range(B_HEADS):
            vt_ref[0, hd, 0:B_V_DIM, :] = acc[:, hd * LANES:(hd + 1) * LANES].T.astype(BF16)
            vt_ref[0, hd, B_V_DIM:VT_ROWS, :] = jnp.ones((VT_ROWS - B_V_DIM, tm), BF16)

    def store_ga(acc):
        ga_ref[...] = (1.0 / (1.0 + jnp.exp(-(acc + bias_ref[:, 0:D_MODEL])))).astype(BF16)

    def store_gb(acc):
        gb_ref[...] = (1.0 / (1.0 + jnp.exp(-(acc + bias_ref[:, D_MODEL:])))).astype(BF16)

    for j, store in enumerate((store_u, store_va, store_q, store_k, store_vt, store_ga, store_gb)):
        store(slab(j))


def _input_projection(x2, g1, w_in, w_gate, b_gate, gv, gq, gk, ones64, ones128, bound, bsz, seq):
    n_tok = x2.shape[0]
    tm = PROJ_TM
    seq_blocks = seq // tm
    row_spec = pl.BlockSpec((tm, D_MODEL), lambda i: (i, 0))
    resident = lambda shape: pl.BlockSpec(shape, lambda i: (0,) * len(shape),
                                          pipeline_mode=pl.Buffered(1))
    head_map = lambda i: (i // seq_blocks, 0, i % seq_blocks, 0)
    tok_bf16 = jax.ShapeDtypeStruct((n_tok, D_MODEL), BF16)
    return pl.pallas_call(
        functools.partial(_inproj_kernel, seq_blocks=seq_blocks),
        grid=(n_tok // tm,),
        in_specs=[
            row_spec,
            resident((1, D_MODEL)),
            resident(w_in.shape), resident(w_gate.shape),
            resident(b_gate.shape),
            resident((1, D_MODEL)), resident((1, D_MODEL)), resident((1, D_MODEL)),
            resident((MXU_DIM, MXU_DIM)), resident((MXU_DIM, MXU_DIM)),
            resident((1, 1)),
        ],
        out_specs=[
            row_spec, row_spec,
            pl.BlockSpec((1, 2 * B_HEADS, tm, LANES), head_map),
            pl.BlockSpec((1, 2 * B_HEADS, tm, LANES), head_map),
            pl.BlockSpec((1, B_HEADS, VT_ROWS, tm), lambda i: (i // seq_blocks, 0, 0, i % seq_blocks)),
            row_spec, row_spec,
        ],
        out_shape=[
            tok_bf16, tok_bf16,
            jax.ShapeDtypeStruct((bsz, 2 * B_HEADS, seq, LANES), BF16),
            jax.ShapeDtypeStruct((bsz, 2 * B_HEADS, seq, LANES), BF16),
            jax.ShapeDtypeStruct((bsz, B_HEADS, VT_ROWS, seq), BF16),
            tok_bf16, tok_bf16,
        ],
        compiler_params=pltpu.CompilerParams(
            dimension_semantics=("arbitrary",),
            vmem_limit_bytes=VMEM_LIMIT_BYTES),
        name="input_projection",
    )(x2, g1, w_in, w_gate, b_gate, gv, gq, gk, ones64, ones128, bound)


def _attn_kernel(lam_ref, sg_ref, q_ref, k_ref, vt_ref, o_ref, acc_scr, sa_scr, sb_scr):
    qi = pl.program_id(2)
    tq = q_ref.shape[2]
    tk = ATT_TK
    nt_dims = (((1,), (1,)), ((), ()))

    assert tq == 2 * tk
    tri = (lax.broadcasted_iota(jnp.int32, (tk, tk), 0)
           <= lax.broadcasted_iota(jnp.int32, (tk, tk), 1))

    def scores(kb, s_scr, diag=None):
        ks = pl.multiple_of(kb * tk, tk)
        maxima = []
        for c in range(2):
            k_blk = k_ref[0, c, pl.ds(ks, tk), :]
            if diag is None:
                s = lax.dot_general(k_blk, q_ref[0, c], nt_dims, preferred_element_type=F32)
                s_scr[c] = s
                maxima.append(jnp.max(s, axis=0, keepdims=True))
                continue
            parts = []
            for half in range(diag, 2):
                cols = slice(half * tk, (half + 1) * tk)
                s = lax.dot_general(k_blk, q_ref[0, c, cols, :], nt_dims,
                                    preferred_element_type=F32)
                if half == diag:
                    s = jnp.where(tri, s, NEG_BIG)
                s_scr[c, :, cols] = s
                parts.append(jnp.max(s, axis=0, keepdims=True))
            maxima.append(parts[0] if len(parts) == 1 else jnp.concatenate(parts, axis=1))
        return tuple(maxima)

    def consume(kb, s_scr, m_old, blk_max, col0=0, first=False):
        ks = pl.multiple_of(kb * tk, tk)
        vt = vt_ref[0, 0, :, pl.ds(ks, tk)]
        m_new = []
        for c in range(2):
            m_c = blk_max[c] if first else jnp.maximum(m_old[c][:, col0:], blk_max[c])
            p = jnp.exp2(s_scr[c, :, col0:] - m_c).astype(BF16)
            pv = jnp.dot(vt, p, preferred_element_type=F32)
            cols = slice(c * tq + col0, (c + 1) * tq)
            if first:
                acc_scr[:, cols] = pv
            else:
                acc_scr[:, cols] = acc_scr[:, cols] * jnp.exp2(m_old[c][:, col0:] - m_c) + pv
            m_new.append(m_c)
        return tuple(m_new)

    max_a = scores(2 * qi, sa_scr, diag=0)
    max_b = scores(2 * qi + 1, sb_scr, diag=1)
    m = consume(2 * qi, sa_scr, None, max_a, first=True)
    max_a = scores(0, sa_scr)
    m_right = consume(2 * qi + 1, sb_scr, m, max_b, col0=tk)
    m = tuple(jnp.concatenate([m[c][:, :tk], m_right[c]], axis=1) for c in range(2))

    def trip(t, carry):
        m, max_a = carry
        max_b = scores(2 * t + 1, sb_scr)
        m = consume(2 * t, sa_scr, m, max_a)
        max_a = scores(2 * t + 2, sa_scr)
        m = consume(2 * t + 1, sb_scr, m, max_b)
        return m, max_a

    m, max_a = lax.fori_loop(0, qi - 1, trip, (m, max_a))

    @pl.when(qi > 0)
    def _():
        max_b = scores(2 * qi - 1, sb_scr)
        m_mid = consume(2 * qi - 2, sa_scr, m, max_a)
        consume(2 * qi - 1, sb_scr, m_mid, max_b)

    _attn_finalize(lam_ref, sg_ref, acc_scr, o_ref)


def _attn_finalize(lam_ref, sg_ref, acc_scr, o_ref):
    tq = o_ref.shape[1]
    lam_v = lam_ref[...]
    lam = (jnp.exp(jnp.sum(lam_v[0:1] * lam_v[1:2], axis=1, keepdims=True))
           - jnp.exp(jnp.sum(lam_v[2:3] * lam_v[3:4], axis=1, keepdims=True)) + LAM_INIT)
    o0 = acc_scr[0:B_V_DIM, 0:tq] / acc_scr[B_V_DIM:B_V_DIM + 1, 0:tq]
    o1 = acc_scr[0:B_V_DIM, tq:2 * tq] / acc_scr[B_V_DIM:B_V_DIM + 1, tq:2 * tq]
    o = o0 - lam * o1
    r = lax.rsqrt(jnp.mean(o * o, axis=0, keepdims=True) + EPS)
    o = ((o * r) * sg_ref[...]) * (1.0 - LAM_INIT)
    o_ref[0] = o.T.astype(BF16)


def _attn_bounded_kernel(lam_ref, sg_ref, q_ref, k_ref, vt_ref, o_ref, acc_scr):
    qi = pl.program_id(2)
    tq = q_ref.shape[2]
    tk = ATT_TK
    nt_dims = (((1,), (1,)), ((), ()))
    tri = (lax.broadcasted_iota(jnp.int32, (tk, tk), 0)
           <= lax.broadcasted_iota(jnp.int32, (tk, tk), 1))

    def fold(key_start, n_keys, q_cols, masked, first):
        vt = vt_ref[0, 0, 0:B_V_DIM, pl.ds(key_start, n_keys)]
        for c in range(2):
            e = lax.dot_general(k_ref[0, c, pl.ds(key_start, n_keys), :], q_ref[0, c, q_cols, :],
                                nt_dims, preferred_element_type=F32)
            if masked:
                e_diag = jnp.where(tri, e[n_keys - tk:, :], NEG_BIG)
                e = e_diag if n_keys == tk else jnp.concatenate([e[:n_keys - tk, :], e_diag], axis=0)
            p = jnp.exp2(e)
            p_sum = jnp.sum(p, axis=0, keepdims=True)
            pv = jnp.dot(vt, p.astype(BF16), preferred_element_type=F32)
            cols = slice(c * tq + q_cols.start, c * tq + q_cols.stop)
            if first:
                acc_scr[0:B_V_DIM, cols] = pv
                acc_scr[B_V_DIM:B_V_DIM + 1, cols] = p_sum
            else:
                acc_scr[0:B_V_DIM, cols] += pv
                acc_scr[B_V_DIM:B_V_DIM + 1, cols] += p_sum

    strips = [slice(j * tk, (j + 1) * tk) for j in range(tq // tk)]
    diag0 = pl.multiple_of(qi * tq, tq)
    for j, strip in enumerate(strips):
        fold(diag0, (j + 1) * tk, strip, True, True)

    def trip(t, carry):
        for strip in strips:
            fold(pl.multiple_of(t * tq, tq), tq, strip, False, False)
        return carry

    lax.fori_loop(0, qi, trip, 0)
    _attn_finalize(lam_ref, sg_ref, acc_scr, o_ref)


def _diff_attention(lam_vecs, sub_g, q_arr, k_arr, vt_arr, bounded):
    bsz, _, seq, _ = q_arr.shape
    tq = ATT_TQ_BOUNDED if bounded else ATT_TQ
    scratch = [pltpu.VMEM((VT_ROWS, 2 * tq), F32)]
    if not bounded:
        scratch += [pltpu.VMEM((2, ATT_TK, tq), F32), pltpu.VMEM((2, ATT_TK, tq), F32)]
    return pl.pallas_call(
        _attn_bounded_kernel if bounded else _attn_kernel,
        grid=(bsz, B_HEADS, seq // tq),
        in_specs=[
            pl.BlockSpec((4, B_HEAD_DIM), lambda b, h, qi: (0, 0)),
            pl.BlockSpec((B_V_DIM, 1), lambda b, h, qi: (0, 0)),
            pl.BlockSpec((1, 2, tq, LANES), lambda b, h, qi: (b, h, qi, 0)),
            pl.BlockSpec((1, 2, seq, LANES), lambda b, h, qi: (b, h, 0, 0)),
            pl.BlockSpec((1, 1, VT_ROWS, seq), lambda b, h, qi: (b, h, 0, 0)),
        ],
        out_specs=pl.BlockSpec((1, tq, B_V_DIM), lambda b, h, qi: (b, qi, h)),
        out_shape=jax.ShapeDtypeStruct((bsz, seq, B_HEADS * B_V_DIM), BF16),
        scratch_shapes=scratch,
        compiler_params=pltpu.CompilerParams(
            dimension_semantics=("arbitrary", "arbitrary", "arbitrary"),
            vmem_limit_bytes=VMEM_LIMIT_BYTES),
        name="diff_attention_bounded" if bounded else "diff_attention",
    )(lam_vecs, sub_g, q_arr, k_arr, vt_arr)


def _first_argmax(vals, iota, top):
    return jnp.min(jnp.where(vals == top, iota, vals.shape[0]), axis=0, keepdims=True)


def _merge_kernel(u_ref, va_ref, yb_ref, ga_ref, gb_ref, x_ref, ws_ref, bs_ref,
                  w3_ref, g2_ref, wr_hi_ref, wr_lo_ref, br_ref,
                  x1_ref, h2_ref, comb_ref, gidx_ref):
    tm = x_ref.shape[0]
    row = lax.broadcasted_iota(jnp.int32, (CHUNK, CHUNK), 0)
    col = lax.broadcasted_iota(jnp.int32, (CHUNK, CHUNK), 1)
    causal = row >= col

    w_tril = [jnp.where(causal, ws_ref[g], 0.0).astype(BF16) for g in range(A_GROUPS)]
    nt_dims = (((1,), (1,)), ((), ()))

    def gate_a(rows):
        n_chunks = (rows.stop - rows.start) // CHUNK
        group_cols = []
        for g in range(A_GROUPS):
            gl = slice(g * LANES, (g + 1) * LANES)
            v_g = jnp.concatenate(
                [va_ref[rows.start + c * CHUNK:rows.start + (c + 1) * CHUNK, gl]
                 for c in range(n_chunks)], axis=1)
            z = jnp.dot(w_tril[g], v_g, preferred_element_type=F32) + bs_ref[g]
            z = jnp.concatenate([z[:, c * LANES:(c + 1) * LANES] for c in range(n_chunks)], axis=0)
            group_cols.append((u_ref[rows, gl].astype(F32) * z).astype(BF16))
        return jnp.concatenate(group_cols, axis=1)

    def merge(rows, y_a):
        up_a = jnp.dot(y_a, w3_ref[0], preferred_element_type=F32)
        up_b = jnp.dot(yb_ref[rows, :], w3_ref[1], preferred_element_type=F32)
        merged = ga_ref[rows, :].astype(F32) * up_a + gb_ref[rows, :].astype(F32) * up_b
        return merged.astype(BF16)

    def project_and_norm(rows, merged):
        x1 = x_ref[rows, :] + jnp.dot(merged, w3_ref[2], preferred_element_type=F32)
        x1_ref[rows, :] = x1
        r = lax.rsqrt(jnp.mean(x1 * x1, axis=-1, keepdims=True) + EPS)
        h2 = (x1 * r) * g2_ref[...]
        h2_hi = h2.astype(BF16)
        h2_ref[rows, :] = h2_hi
        return h2_hi, (h2 - h2_hi.astype(F32)).astype(BF16)

    def route(rows, h2_parts):
        h2_hi, h2_lo = h2_parts
        tm = rows.stop - rows.start
        lt = (lax.dot_general(wr_hi_ref[...], h2_hi, nt_dims, preferred_element_type=F32)
              + lax.dot_general(wr_hi_ref[...], h2_lo, nt_dims, preferred_element_type=F32)
              + lax.dot_general(wr_lo_ref[...], h2_hi, nt_dims, preferred_element_type=F32)
              + br_ref[...])

        iota8 = lax.broadcasted_iota(jnp.int32, (8, tm), 0)
        gl = lt[0:8]
        ge = jnp.exp(gl - jnp.max(gl, axis=0, keepdims=True))
        gp = ge / jnp.sum(ge, axis=0, keepdims=True)
        g_gate = jnp.max(gp, axis=0, keepdims=True)
        g_idx = _first_argmax(gp, iota8, g_gate)

        el = jnp.zeros((EXPERTS_PER_GROUP, tm), F32)
        for gi in range(N_GROUPS):
            el = jnp.where(g_idx == gi, lt[8 + 8 * gi:16 + 8 * gi], el)
        ee = jnp.exp(el - jnp.max(el, axis=0, keepdims=True))
        ep = ee / jnp.sum(ee, axis=0, keepdims=True)
        p_top1 = jnp.max(ep, axis=0, keepdims=True)
        i_top1 = _first_argmax(ep, iota8, p_top1)
        ep_rest = jnp.where(iota8 == i_top1, -1.0, ep)
        p_top2 = jnp.max(ep_rest, axis=0, keepdims=True)
        i_top2 = _first_argmax(ep_rest, iota8, p_top2)
        den = p_top1 + p_top2
        w_top1 = g_gate * (p_top1 / den)
        w_top2 = g_gate * (p_top2 / den)
        in_group = (jnp.where(iota8 == i_top1, w_top1, 0.0)
                    + jnp.where(iota8 == i_top2, w_top2, 0.0))
        blocks = [jnp.where(g_idx == gi, in_group, 0.0) for gi in range(N_GROUPS)]
        blocks.append(jnp.zeros((LANES - N_EXPERTS, tm), F32))
        comb_t = jnp.concatenate(blocks, axis=0)
        comb_ref[rows, :] = comb_t.T
        gidx_ref[:, rows] = jnp.broadcast_to(g_idx, (gidx_ref.shape[0], tm))

    stages = (gate_a, merge, project_and_norm, route)
    subs = [slice(s0, s0 + MERGE_SUB_TM) for s0 in range(0, tm, MERGE_SUB_TM)]
    state = [None] * len(subs)
    for step in range(len(subs) + len(stages) - 1):
        for si, rows in enumerate(subs):
            stage = step - si
            if 0 <= stage < len(stages):
                state[si] = stages[stage](rows, state[si]) if stage else stages[0](rows)


def _merge_and_route(u, va, yb, ga, gb, x2, ws, bs, w_up_a_up_b_out, g2, wr_hi, wr_lo, br):
    n_tok = x2.shape[0]
    tm = MERGE_TM
    row_spec = pl.BlockSpec((tm, D_MODEL), lambda i: (i, 0))
    full = lambda shape: pl.BlockSpec(shape, lambda i: (0,) * len(shape),
                                      pipeline_mode=pl.Buffered(1))
    return pl.pallas_call(
        _merge_kernel,
        grid=(n_tok // tm,),
        in_specs=[
            row_spec, row_spec, row_spec, row_spec, row_spec, row_spec,
            full((A_GROUPS, CHUNK, CHUNK)), full((A_GROUPS, CHUNK, 1)),
            full((3, D_MODEL, D_MODEL)),
            full((1, D_MODEL)),
            full((ROUTER_ROWS, D_MODEL)), full((ROUTER_ROWS, D_MODEL)), full((ROUTER_ROWS, 1)),
        ],
        out_specs=[row_spec, row_spec, pl.BlockSpec((tm, LANES), lambda i: (i, 0)),
                   pl.BlockSpec((8, tm), lambda i: (0, i))],
        out_shape=[
            jax.ShapeDtypeStruct((n_tok, D_MODEL), F32),
            jax.ShapeDtypeStruct((n_tok, D_MODEL), BF16),
            jax.ShapeDtypeStruct((n_tok, LANES), F32),
            jax.ShapeDtypeStruct((8, n_tok), jnp.int32),
        ],
        compiler_params=pltpu.CompilerParams(
            dimension_semantics=("arbitrary",),
            vmem_limit_bytes=BIG_VMEM_LIMIT_BYTES),
        name="merge_and_route",
    )(u, va, yb, ga, gb, x2, ws, bs, w_up_a_up_b_out, g2, wr_hi, wr_lo, br)


def _moe_kernel(g_ref, h_ref, c_ref, x1_ref, w1_ref, w3_ref, w2_ref, o_ref,
                p_scr, xg_scr, cg_scr, y_scr, bounds_ref):
    e = pl.program_id(1)
    n_steps = pl.num_programs(1)
    tt = h_ref.shape[0]
    rows = p_scr.shape[0]
    unit = MOE_ROW_UNIT
    rb = 2 * unit

    @pl.when(e == 0)
    def _():
        g_idx = g_ref[0:1, :]
        slot = lax.broadcasted_iota(jnp.int32, (16, tt), 0)
        onehot = jnp.where(slot == g_idx, 1.0, 0.0)
        upper = (lax.broadcasted_iota(jnp.int32, (LANES, LANES), 0)
                 <= lax.broadcasted_iota(jnp.int32, (LANES, LANES), 1)).astype(BF16)
        carry = jnp.zeros((16, 1), F32)
        counts = []
        for j in range(tt // LANES):
            seg = onehot[:, j * LANES:(j + 1) * LANES].astype(BF16)
            cs = jnp.dot(seg, upper, preferred_element_type=F32) + carry
            counts.append(cs)
            carry = cs[:, LANES - 1:LANES]
        running = jnp.concatenate(counts, axis=1)
        n_units = jnp.floor((carry + (unit - 1)) * (1.0 / unit))
        slot_col = lax.broadcasted_iota(jnp.int32, (16, 1), 0)
        first_unit = jnp.zeros((16, 1), F32)
        for g in range(N_GROUPS - 1):
            first_unit = first_unit + jnp.where(slot_col > g, n_units[g:g + 1, :], 0.0)
        for g in range(N_GROUPS):
            bounds_ref[g] = first_unit[g:g + 1, :][0, 0].astype(jnp.int32)
            bounds_ref[N_GROUPS + g] = n_units[g:g + 1, :][0, 0].astype(jnp.int32)
        pos = jnp.sum(onehot * (first_unit * unit + running - 1.0), axis=0, keepdims=True)
        pos = pos.astype(jnp.int32)

        c = c_ref[...]
        c_hi = c.astype(BF16)
        c_rest = c - c_hi.astype(F32)
        c_mid = c_rest.astype(BF16)
        c_lo = (c_rest - c_mid.astype(F32)).astype(BF16)
        h_ext = jnp.concatenate([h_ref[...], c_hi, c_mid, c_lo], axis=1)
        for ch in range(rows // MXU_DIM):
            sl = slice(ch * MXU_DIM, (ch + 1) * MXU_DIM)
            row_id = lax.broadcasted_iota(jnp.int32, (MXU_DIM, tt), 0) + ch * MXU_DIM
            perm = jnp.where(row_id == pos, 1.0, 0.0).astype(BF16)
            p_scr[sl, :] = perm
            got = jnp.dot(perm, h_ext, preferred_element_type=F32)
            xg_scr[sl, :] = got[:, :D_MODEL].astype(BF16)
            cg_scr[sl, :] = (got[:, D_MODEL:D_MODEL + LANES]
                             + got[:, D_MODEL + LANES:D_MODEL + 2 * LANES]
                             + got[:, D_MODEL + 2 * LANES:])
        y_scr[...] = jnp.zeros_like(y_scr)

    group = e // (EXPERTS_PER_GROUP // MOE_NE)
    to_front = (LANES - e * MOE_NE) & (LANES - 1)

    def row_block(r0, n_rows):
        r0 = pl.multiple_of(r0, unit)
        xb = xg_scr[pl.ds(r0, n_rows), :]
        cb = pltpu.roll(cg_scr[pl.ds(r0, n_rows), :], to_front, axis=1)
        hids = []
        for n in range(MOE_NE):
            a = jnp.dot(xb, w1_ref[n], preferred_element_type=F32)
            b3 = jnp.dot(xb, w3_ref[n], preferred_element_type=F32)
            hid = (a * (1.0 / (1.0 + jnp.exp(-a)))) * b3
            hids.append((hid * cb[:, n:n + 1]).astype(BF16))
        hid_all = jnp.concatenate(hids, axis=1)
        y_scr[pl.ds(r0, n_rows), :] = jnp.dot(hid_all, w2_ref[...],
                                              preferred_element_type=F32).astype(BF16)

    first_row = bounds_ref[group] * unit
    n_units = bounds_ref[N_GROUPS + group]
    n_blocks = n_units // 2

    def pair(i, carry):
        row_block(first_row + (2 * i) * rb, rb)
        row_block(first_row + (2 * i + 1) * rb, rb)
        return carry

    lax.fori_loop(0, n_blocks // 2, pair, 0)

    @pl.when(n_blocks % 2 == 1)
    def _():
        row_block(first_row + (n_blocks - 1) * rb, rb)

    @pl.when(n_units % 2 == 1)
    def _():
        row_block(first_row + n_blocks * rb, unit)

    @pl.when(e == n_steps - 1)
    def _():
        tn_dims = (((0,), (0,)), ((), ()))
        y = lax.dot_general(p_scr[...], y_scr[...], tn_dims,
                            preferred_element_type=F32)
        o_ref[...] = x1_ref[...] + y


def _moe(gidx, h2, comb, x1, w1, w3, w2f):
    assert MOE_NE == EXPERTS_PER_GROUP, "each row block is written by exactly one grid step"
    n_tok = h2.shape[0]
    tt = MOE_TM
    rows = tt + N_GROUPS * MOE_ROW_UNIT
    assert rows % MXU_DIM == 0
    return pl.pallas_call(
        _moe_kernel,
        grid=(n_tok // tt, N_EXPERTS // MOE_NE),
        in_specs=[
            pl.BlockSpec((8, tt), lambda i, e: (0, i)),
            pl.BlockSpec((tt, D_MODEL), lambda i, e: (i, 0)),
            pl.BlockSpec((tt, LANES), lambda i, e: (i, 0)),
            pl.BlockSpec((tt, D_MODEL), lambda i, e: (jnp.where(e == 0, jnp.maximum(i - 1, 0), i), 0)),
            pl.BlockSpec((MOE_NE, D_MODEL, D_EXPERT), lambda i, e: (e, 0, 0)),
            pl.BlockSpec((MOE_NE, D_MODEL, D_EXPERT), lambda i, e: (e, 0, 0)),
            pl.BlockSpec((MOE_NE * D_EXPERT, D_MODEL), lambda i, e: (e, 0)),
        ],
        out_specs=pl.BlockSpec((tt, D_MODEL), lambda i, e: (i, 0)),
        out_shape=jax.ShapeDtypeStruct((n_tok, D_MODEL), F32),
        scratch_shapes=[
            pltpu.VMEM((rows, tt), BF16),
            pltpu.VMEM((rows, D_MODEL), BF16),
            pltpu.VMEM((rows, LANES), F32),
            pltpu.VMEM((rows, D_MODEL), BF16),
            pltpu.SMEM((2 * N_GROUPS,), jnp.int32),
        ],
        compiler_params=pltpu.CompilerParams(
            dimension_semantics=("arbitrary", "arbitrary"),
            vmem_limit_bytes=BIG_VMEM_LIMIT_BYTES),
        name="moe_experts",
    )(gidx, h2, comb, x1, w1, w3, w2f)


def _segment_ones(seg):
    idx = jnp.arange(MXU_DIM) // seg
    return (idx[:, None] == idx[None, :]).astype(BF16)


def kernel(x, norm1_g, w_in, v_norm_g, w_s, b_s, q_norm_g, k_norm_g, lambda_q1, lambda_k1, lambda_q2, lambda_k2, sub_norm_g, w_up_a, w_up_b, w_gate, b_gate, w_out, norm2_g, w_rg, b_rg, w_re, b_re, w1, w3, w2):
    bsz, seq, d = x.shape
    assert d == D_MODEL and seq % PROJ_TM == 0 and seq % ATT_TQ == 0 and ATT_TQ == 2 * ATT_TK
    assert seq % ATT_TQ_BOUNDED == 0 and ATT_TQ_BOUNDED % ATT_TK == 0
    assert norm1_g.shape[0] == 1, "single layer"
    n_tok = bsz * seq
    assert n_tok % MOE_TM == 0 and n_tok % MERGE_TM == 0
    x2 = x.reshape(n_tok, d)

    gv = v_norm_g[0].reshape(1, D_MODEL)
    gq = jnp.tile(q_norm_g[0] * (B_HEAD_DIM ** -0.5 * LOG2_E), 2 * B_HEADS)[None, :]
    gk = jnp.tile(k_norm_g[0], 2 * B_HEADS)[None, :]
    score_bound = (SCORE_BOUND_MARGIN * B_HEAD_DIM ** 0.5 * LOG2_E
                   * jnp.max(jnp.abs(q_norm_g[0] * k_norm_g[0]))).reshape(1, 1)
    u, va, q_arr, k_arr, vt_arr, ga, gb = _input_projection(
        x2, norm1_g, w_in[0].astype(BF16), w_gate[0].astype(BF16), b_gate, gv, gq, gk, _segment_ones(B_HEAD_DIM), _segment_ones(LANES),
        score_bound, bsz, seq)

    lam_vecs = jnp.concatenate([lambda_q1, lambda_k1, lambda_q2, lambda_k2], axis=0)
    attn_args = (lam_vecs, sub_norm_g[0][:, None], q_arr, k_arr, vt_arr)
    yb = lax.cond(score_bound[0, 0] <= MAX_BOUNDED_SCORE,
                  lambda args: _diff_attention(*args, bounded=True),
                  lambda args: _diff_attention(*args, bounded=False), attn_args)
    yb = yb.reshape(n_tok, D_MODEL)

    pad_rows = ROUTER_ROWS - 8 - N_EXPERTS
    wr = jnp.concatenate([w_rg[0].T, jnp.zeros((4, d), F32), w_re[0].T,
                          jnp.zeros((pad_rows, d), F32)], axis=0)
    br = jnp.concatenate([b_rg[0], jnp.full((4,), NEG_BIG, F32), b_re[0],
                          jnp.zeros((pad_rows,), F32)])[:, None]
    wr_hi = wr.astype(BF16)
    wr_lo = (wr - wr_hi.astype(F32)).astype(BF16)
    x1, h2, comb, gidx = _merge_and_route(
        u, va, yb, ga, gb, x2, w_s[0], b_s[0][:, :, None],
        jnp.concatenate([w_up_a, w_up_b, w_out], axis=0).astype(BF16),
        norm2_g, wr_hi, wr_lo, br)

    w2f = w2[0].astype(BF16).reshape(N_EXPERTS * D_EXPERT, D_MODEL)
    out = _moe(gidx, h2, comb, x1, w1[0].astype(BF16), w3[0].astype(BF16), w2f)
    return out.reshape(bsz, seq, d)
```

```python
import functools
import math

import jax
import jax.numpy as jnp
import numpy as np
from jax import lax
from jax.experimental import pallas as pl
from jax.experimental.pallas import tpu as pltpu

F32 = jnp.float32
BF16 = jnp.bfloat16

D_MODEL = 1024
EPS = 1e-6
A_GROUPS = 8
CHUNK = 128
B_HEADS = 8
B_HEAD_DIM = 64
B_V_DIM = 128
N_GROUPS = 4
EXPERTS_PER_GROUP = 8
N_EXPERTS = 32
D_EXPERT = 256
LAM_INIT = 0.8 - 0.6 * math.exp(-0.3 * 0)

LANES = 128
MXU_DIM = 256
VMEM_BYTES = 64 * 1024 * 1024
VMEM_LIMIT_BYTES = VMEM_BYTES - 8 * 1024 * 1024
BIG_VMEM_LIMIT_BYTES = VMEM_BYTES - 1024 * 1024

NEG_BIG = -1e30
LOG2_E = math.log2(math.e)
ALIBI_SLOPES = tuple(2.0 ** (-8.0 * (h + 1) / B_HEADS) for h in range(B_HEADS))
SLOPE_PARTS = 4
VT_ROWS = B_V_DIM + 16
SCORE_BOUND_MARGIN = 1.02
MAX_BOUNDED_SCORE = 48.0

PROJ_TM = 512
ATT_TQ_BOUNDED = 2048
ATT_TQ = 1024
ATT_TK = 512
MERGE_TM = 1024
MERGE_SUB_TM = 512
MOE_TM = 1024
MOE_NE = 8
MOE_ROW_UNIT = 64
ROUTER_ROWS = 48


def _bf16_parts(value, n_parts):
    parts, rest = [], np.float64(value)
    for _ in range(n_parts):
        part = np.float64(np.asarray(rest, np.float32).astype(BF16).astype(np.float32))
        parts.append(float(part))
        rest = rest - part
    return parts


def _segment_rms_scale(acc, seg_ones_ref, seg):
    sq = (acc * acc).astype(BF16)
    parts = []
    for p in range(acc.shape[1] // MXU_DIM):
        parts.append(jnp.dot(sq[:, p * MXU_DIM:(p + 1) * MXU_DIM], seg_ones_ref[...],
                             preferred_element_type=F32))
    ss = jnp.concatenate(parts, axis=1)
    return lax.rsqrt(ss * (1.0 / seg) + EPS)


def _inproj_kernel(x_ref, g1_ref, w_ref, wg_ref, bias_ref, gv_ref, gq_ref, gk_ref, ones64_ref, ones128_ref,
                   bound_ref,
                   u_ref, va_ref, q_ref, k_ref, vt_ref, ga_ref, gb_ref, *, seq_blocks):
    i = pl.program_id(0)
    tm = x_ref.shape[0]
    x = x_ref[...]
    r = lax.rsqrt(jnp.mean(x * x, axis=-1, keepdims=True) + EPS)
    h = ((x * r) * g1_ref[...]).astype(BF16)

    n_in = w_ref.shape[1] // D_MODEL

    def slab(j):
        ref, col = (w_ref, j) if j < n_in else (wg_ref, j - n_in)
        return jnp.dot(h, ref[:, col * D_MODEL:(col + 1) * D_MODEL], preferred_element_type=F32)

    lane = lax.broadcasted_iota(jnp.int32, (1, LANES), 1)
    lo_half = lane < B_HEAD_DIM

    spare_col = lane & (B_HEAD_DIM - 1)
    pos = (i % seq_blocks) * tm + lax.broadcasted_iota(jnp.int32, (tm, LANES), 0)
    pos_lo = (pos & (MXU_DIM - 1)).astype(F32)
    pos_hi = (pos - (pos & (MXU_DIM - 1))).astype(F32)
    pos_cols = jnp.where(spare_col & SLOPE_PARTS == 0, pos_lo, pos_hi)
    bound = bound_ref[...]
    bound_hi = bound.astype(BF16).astype(F32)
    bound_mid = (bound - bound_hi).astype(BF16).astype(F32)
    bound_lo = (bound - bound_hi - bound_mid).astype(BF16).astype(F32)
    bound_cols = jnp.where(spare_col == 4 * SLOPE_PARTS, bound_hi,
                           jnp.where(spare_col == 4 * SLOPE_PARTS + 1, bound_mid,
                                     jnp.where(spare_col == 4 * SLOPE_PARTS + 2, bound_lo, 0.0)))
    q_shared = jnp.where((spare_col >= 2 * SLOPE_PARTS) & (spare_col < 4 * SLOPE_PARTS),
                         pos_cols, bound_cols)
    k_shared = jnp.where(spare_col < 2 * SLOPE_PARTS, pos_cols, 0.0)

    def slope_cols(hd, sign):
        parts = _bf16_parts(ALIBI_SLOPES[hd] * LOG2_E, SLOPE_PARTS)
        cvec = jnp.zeros((1, LANES), F32)
        for n, part in enumerate(parts):
            cvec = jnp.where(spare_col & (SLOPE_PARTS - 1) == n, sign * part, cvec)
        return cvec

    def store_u(acc):
        u_ref[...] = acc.astype(BF16)

    def store_va(acc):
        r = _segment_rms_scale(acc, ones128_ref, LANES)
        va_ref[...] = ((acc * r) * gv_ref[...]).astype(BF16)

    def store_q(acc):
        r = _segment_rms_scale(acc, ones64_ref, B_HEAD_DIM)
        qn = (acc * r) * gq_ref[...]
        for hd in range(B_HEADS):
            extra = jnp.where(spare_col < 2 * SLOPE_PARTS, slope_cols(hd, 1.0), q_shared)
            blk = qn[:, hd * LANES:(hd + 1) * LANES]
            q_ref[0, 2 * hd] = jnp.where(lo_half, blk, extra).astype(BF16)
            q_ref[0, 2 * hd + 1] = jnp.where(lo_half, extra, blk).astype(BF16)

    def store_k(acc):
        r = _segment_rms_scale(acc, ones64_ref, B_HEAD_DIM)
        kn = (acc * r) * gk_ref[...]
        for hd in range(B_HEADS):
            consts = jnp.where(spare_col < 4 * SLOPE_PARTS, slope_cols(hd, -1.0),
                               jnp.where(spare_col < 4 * SLOPE_PARTS + 3, -1.0, 0.0))
            extra = jnp.where(spare_col < 2 * SLOPE_PARTS, k_shared, consts)
            blk = kn[:, hd * LANES:(hd + 1) * LANES]
            k_ref[0, 2 * hd] = jnp.where(lo_half, blk, extra).astype(BF16)
            k_ref[0, 2 * hd + 1] = jnp.where(lo_half, extra, blk).astype(BF16)

    def store_vt(acc):
        for hd in range(B_HEADS):
            vt_ref[0, hd, 0:B_V_DIM, :] = acc[:, hd * LANES:(hd + 1) * LANES].T.astype(BF16)
            vt_ref[0, hd, B_V_DIM:VT_ROWS, :] = jnp.ones((VT_ROWS - B_V_DIM, tm), BF16)

    def store_ga(acc):
        ga_ref[...] = (1.0 / (1.0 + jnp.exp(-(acc + bias_ref[:, 0:D_MODEL])))).astype(BF16)

    def store_gb(acc):
        gb_ref[...] = (1.0 / (1.0 + jnp.exp(-(acc + bias_ref[:, D_MODEL:])))).astype(BF16)

    for j, store in enumerate((store_u, store_va, store_q, store_k, store_vt, store_ga, store_gb)):
        store(slab(j))


def _input_projection(x2, g1, w_in, w_gate, b_gate, gv, gq, gk, ones64, ones128, bound, bsz, seq):
    n_tok = x2.shape[0]
    tm = PROJ_TM
    seq_blocks = seq // tm
    row_spec = pl.BlockSpec((tm, D_MODEL), lambda i: (i, 0))
    resident = lambda shape: pl.BlockSpec(shape, lambda i: (0,) * len(shape),
                                          pipeline_mode=pl.Buffered(1))
    head_map = lambda i: (i // seq_blocks, 0, i % seq_blocks, 0)
    tok_bf16 = jax.ShapeDtypeStruct((n_tok, D_MODEL), BF16)
    return pl.pallas_call(
        functools.partial(_inproj_kernel, seq_blocks=seq_blocks),
        grid=(n_tok // tm,),
        in_specs=[
            row_spec,
            resident((1, D_MODEL)),
            resident(w_in.shape), resident(w_gate.shape),
            resident(b_gate.shape),
            resident((1, D_MODEL)), resident((1, D_MODEL)), resident((1, D_MODEL)),
            resident((MXU_DIM, MXU_DIM)), resident((MXU_DIM, MXU_DIM)),
            resident((1, 1)),
        ],
        out_specs=[
            row_spec, row_spec,
            pl.BlockSpec((1, 2 * B_HEADS, tm, LANES), head_map),
            pl.BlockSpec((1, 2 * B_HEADS, tm, LANES), head_map),
            pl.BlockSpec((1, B_HEADS, VT_ROWS, tm), lambda i: (i // seq_blocks, 0, 0, i % seq_blocks)),
            row_spec, row_spec,
        ],
        out_shape=[
            tok_bf16, tok_bf16,
            jax.ShapeDtypeStruct((bsz, 2 * B_HEADS, seq, LANES), BF16),
            jax.ShapeDtypeStruct((bsz, 2 * B_HEADS, seq, LANES), BF16),
            jax.ShapeDtypeStruct((bsz, B_HEADS, VT_ROWS, seq), BF16),
            tok_bf16, tok_bf16,
        ],
        compiler_params=pltpu.CompilerParams(
            dimension_semantics=("arbitrary",),
            vmem_limit_bytes=VMEM_LIMIT_BYTES),
        name="input_projection",
    )(x2, g1, w_in, w_gate, b_gate, gv, gq, gk, ones64, ones128, bound)


def _attn_kernel(lam_ref, sg_ref, q_ref, k_ref, vt_ref, o_ref, acc_scr, sa_scr, sb_scr):
    qi = pl.program_id(2)
    tq = q_ref.shape[2]
    tk = ATT_TK
    nt_dims = (((1,), (1,)), ((), ()))

    assert tq == 2 * tk
    tri = (lax.broadcasted_iota(jnp.int32, (tk, tk), 0)
           <= lax.broadcasted_iota(jnp.int32, (tk, tk), 1))

    def scores(kb, s_scr, diag=None):
        ks = pl.multiple_of(kb * tk, tk)
        maxima = []
        for c in range(2):
            k_blk = k_ref[0, c, pl.ds(ks, tk), :]
            if diag is None:
                s = lax.dot_general(k_blk, q_ref[0, c], nt_dims, preferred_element_type=F32)
                s_scr[c] = s
                maxima.append(jnp.max(s, axis=0, keepdims=True))
                continue
            parts = []
            for half in range(diag, 2):
                cols = slice(half * tk, (half + 1) * tk)
                s = lax.dot_general(k_blk, q_ref[0, c, cols, :], nt_dims,
                                    preferred_element_type=F32)
                if half == diag:
                    s = jnp.where(tri, s, NEG_BIG)
                s_scr[c, :, cols] = s
                parts.append(jnp.max(s, axis=0, keepdims=True))
            maxima.append(parts[0] if len(parts) == 1 else jnp.concatenate(parts, axis=1))
        return tuple(maxima)

    def consume(kb, s_scr, m_old, blk_max, col0=0, first=False):
        ks = pl.multiple_of(kb * tk, tk)
        vt = vt_ref[0, 0, :, pl.ds(ks, tk)]
        m_new = []
        for c in range(2):
            m_c = blk_max[c] if first else jnp.maximum(m_old[c][:, col0:], blk_max[c])
            p = jnp.exp2(s_scr[c, :, col0:] - m_c).astype(BF16)
            pv = jnp.dot(vt, p, preferred_element_type=F32)
            cols = slice(c * tq + col0, (c + 1) * tq)
            if first:
                acc_scr[:, cols] = pv
            else:
                acc_scr[:, cols] = acc_scr[:, cols] * jnp.exp2(m_old[c][:, col0:] - m_c) + pv
            m_new.append(m_c)
        return tuple(m_new)

    max_a = scores(2 * qi, sa_scr, diag=0)
    max_b = scores(2 * qi + 1, sb_scr, diag=1)
    m = consume(2 * qi, sa_scr, None, max_a, first=True)
    max_a = scores(0, sa_scr)
    m_right = consume(2 * qi + 1, sb_scr, m, max_b, col0=tk)
    m = tuple(jnp.concatenate([m[c][:, :tk], m_right[c]], axis=1) for c in range(2))

    def trip(t, carry):
        m, max_a = carry
        max_b = scores(2 * t + 1, sb_scr)
        m = consume(2 * t, sa_scr, m, max_a)
        max_a = scores(2 * t + 2, sa_scr)
        m = consume(2 * t + 1, sb_scr, m, max_b)
        return m, max_a

    m, max_a = lax.fori_loop(0, qi - 1, trip, (m, max_a))

    @pl.when(qi > 0)
    def _():
        max_b = scores(2 * qi - 1, sb_scr)
        m_mid = consume(2 * qi - 2, sa_scr, m, max_a)
        consume(2 * qi - 1, sb_scr, m_mid, max_b)

    _attn_finalize(lam_ref, sg_ref, acc_scr, o_ref)


def _attn_finalize(lam_ref, sg_ref, acc_scr, o_ref):
    tq = o_ref.shape[1]
    lam_v = lam_ref[...]
    lam = (jnp.exp(jnp.sum(lam_v[0:1] * lam_v[1:2], axis=1, keepdims=True))
           - jnp.exp(jnp.sum(lam_v[2:3] * lam_v[3:4], axis=1, keepdims=True)) + LAM_INIT)
    o0 = acc_scr[0:B_V_DIM, 0:tq] / acc_scr[B_V_DIM:B_V_DIM + 1, 0:tq]
    o1 = acc_scr[0:B_V_DIM, tq:2 * tq] / acc_scr[B_V_DIM:B_V_DIM + 1, tq:2 * tq]
    o = o0 - lam * o1
    r = lax.rsqrt(jnp.mean(o * o, axis=0, keepdims=True) + EPS)
    o = ((o * r) * sg_ref[...]) * (1.0 - LAM_INIT)
    o_ref[0] = o.T.astype(BF16)


def _attn_bounded_kernel(lam_ref, sg_ref, q_ref, k_ref, vt_ref, o_ref, acc_scr):
    qi = pl.program_id(2)
    tq = q_ref.shape[2]
    tk = ATT_TK
    nt_dims = (((1,), (1,)), ((), ()))
    tri = (lax.broadcasted_iota(jnp.int32, (tk, tk), 0)
           <= lax.broadcasted_iota(jnp.int32, (tk, tk), 1))

    def fold(key_start, n_keys, q_cols, masked, first):
        vt = vt_ref[0, 0, 0:B_V_DIM, pl.ds(key_start, n_keys)]
        for c in range(2):
            e = lax.dot_general(k_ref[0, c, pl.ds(key_start, n_keys), :], q_ref[0, c, q_cols, :],
                                nt_dims, preferred_element_type=F32)
            if masked:
                e_diag = jnp.where(tri, e[n_keys - tk:, :], NEG_BIG)
                e = e_diag if n_keys == tk else jnp.concatenate([e[:n_keys - tk, :], e_diag], axis=0)
            p = jnp.exp2(e)
            p_sum = jnp.sum(p, axis=0, keepdims=True)
            pv = jnp.dot(vt, p.astype(BF16), preferred_element_type=F32)
            cols = slice(c * tq + q_cols.start, c * tq + q_cols.stop)
            if first:
                acc_scr[0:B_V_DIM, cols] = pv
                acc_scr[B_V_DIM:B_V_DIM + 1, cols] = p_sum
            else:
                acc_scr[0:B_V_DIM, cols] += pv
                acc_scr[B_V_DIM:B_V_DIM + 1, cols] += p_sum

    strips = [slice(j * tk, (j + 1) * tk) for j in range(tq // tk)]
    diag0 = pl.multiple_of(qi * tq, tq)
    for j, strip in enumerate(strips):
        fold(diag0, (j + 1) * tk, strip, True, True)

    def trip(t, carry):
        for strip in strips:
            fold(pl.multiple_of(t * tq, tq), tq, strip, False, False)
        return carry

    lax.fori_loop(0, qi, trip, 0)
    _attn_finalize(lam_ref, sg_ref, acc_scr, o_ref)


def _diff_attention(lam_vecs, sub_g, q_arr, k_arr, vt_arr, bounded):
    bsz, _, seq, _ = q_arr.shape
    tq = ATT_TQ_BOUNDED if bounded else ATT_TQ
    scratch = [pltpu.VMEM((VT_ROWS, 2 * tq), F32)]
    if not bounded:
        scratch += [pltpu.VMEM((2, ATT_TK, tq), F32), pltpu.VMEM((2, ATT_TK, tq), F32)]
    return pl.pallas_call(
        _attn_bounded_kernel if bounded else _attn_kernel,
        grid=(bsz, B_HEADS, seq // tq),
        in_specs=[
            pl.BlockSpec((4, B_HEAD_DIM), lambda b, h, qi: (0, 0)),
            pl.BlockSpec((B_V_DIM, 1), lambda b, h, qi: (0, 0)),
            pl.BlockSpec((1, 2, tq, LANES), lambda b, h, qi: (b, h, qi, 0)),
            pl.BlockSpec((1, 2, seq, LANES), lambda b, h, qi: (b, h, 0, 0)),
            pl.BlockSpec((1, 1, VT_ROWS, seq), lambda b, h, qi: (b, h, 0, 0)),
        ],
        out_specs=pl.BlockSpec((1, tq, B_V_DIM), lambda b, h, qi: (b, qi, h)),
        out_shape=jax.ShapeDtypeStruct((bsz, seq, B_HEADS * B_V_DIM), BF16),
        scratch_shapes=scratch,
        compiler_params=pltpu.CompilerParams(
            dimension_semantics=("arbitrary", "arbitrary", "arbitrary"),
            vmem_limit_bytes=VMEM_LIMIT_BYTES),
        name="diff_attention_bounded" if bounded else "diff_attention",
    )(lam_vecs, sub_g, q_arr, k_arr, vt_arr)


def _first_argmax(vals, iota, top):
    return jnp.min(jnp.where(vals == top, iota, vals.shape[0]), axis=0, keepdims=True)


def _merge_kernel(u_ref, va_ref, yb_ref, ga_ref, gb_ref, x_ref, ws_ref, bs_ref,
                  w3_ref, g2_ref, wr_hi_ref, wr_lo_ref, br_ref,
                  x1_ref, h2_ref, comb_ref, gidx_ref):
    tm = x_ref.shape[0]
    row = lax.broadcasted_iota(jnp.int32, (CHUNK, CHUNK), 0)
    col = lax.broadcasted_iota(jnp.int32, (CHUNK, CHUNK), 1)
    causal = row >= col

    w_tril = [jnp.where(causal, ws_ref[g], 0.0).astype(BF16) for g in range(A_GROUPS)]
    nt_dims = (((1,), (1,)), ((), ()))

    def gate_a(rows):
        n_chunks = (rows.stop - rows.start) // CHUNK
        group_cols = []
        for g in range(A_GROUPS):
            gl = slice(g * LANES, (g + 1) * LANES)
            v_g = jnp.concatenate(
                [va_ref[rows.start + c * CHUNK:rows.start + (c + 1) * CHUNK, gl]
                 for c in range(n_chunks)], axis=1)
            z = jnp.dot(w_tril[g], v_g, preferred_element_type=F32) + bs_ref[g]
            z = jnp.concatenate([z[:, c * LANES:(c + 1) * LANES] for c in range(n_chunks)], axis=0)
            group_cols.append((u_ref[rows, gl].astype(F32) * z).astype(BF16))
        return jnp.concatenate(group_cols, axis=1)

    def merge(rows, y_a):
        up_a = jnp.dot(y_a, w3_ref[0], preferred_element_type=F32)
        up_b = jnp.dot(yb_ref[rows, :], w3_ref[1], preferred_element_type=F32)
        merged = ga_ref[rows, :].astype(F32) * up_a + gb_ref[rows, :].astype(F32) * up_b
        return merged.astype(BF16)

    def project_and_norm(rows, merged):
        x1 = x_ref[rows, :] + jnp.dot(merged, w3_ref[2], preferred_element_type=F32)
        x1_ref[rows, :] = x1
        r = lax.rsqrt(jnp.mean(x1 * x1, axis=-1, keepdims=True) + EPS)
        h2 = (x1 * r) * g2_ref[...]
        h2_hi = h2.astype(BF16)
        h2_ref[rows, :] = h2_hi
        return h2_hi, (h2 - h2_hi.astype(F32)).astype(BF16)

    def route(rows, h2_parts):
        h2_hi, h2_lo = h2_parts
        tm = rows.stop - rows.start
        lt = (lax.dot_general(wr_hi_ref[...], h2_hi, nt_dims, preferred_element_type=F32)
              + lax.dot_general(wr_hi_ref[...], h2_lo, nt_dims, preferred_element_type=F32)
              + lax.dot_general(wr_lo_ref[...], h2_hi, nt_dims, preferred_element_type=F32)
              + br_ref[...])

        iota8 = lax.broadcasted_iota(jnp.int32, (8, tm), 0)
        gl = lt[0:8]
        ge = jnp.exp(gl - jnp.max(gl, axis=0, keepdims=True))
        gp = ge / jnp.sum(ge, axis=0, keepdims=True)
        g_gate = jnp.max(gp, axis=0, keepdims=True)
        g_idx = _first_argmax(gp, iota8, g_gate)

        el = jnp.zeros((EXPERTS_PER_GROUP, tm), F32)
        for gi in range(N_GROUPS):
            el = jnp.where(g_idx == gi, lt[8 + 8 * gi:16 + 8 * gi], el)
        ee = jnp.exp(el - jnp.max(el, axis=0, keepdims=True))
        ep = ee / jnp.sum(ee, axis=0, keepdims=True)
        p_top1 = jnp.max(ep, axis=0, keepdims=True)
        i_top1 = _first_argmax(ep, iota8, p_top1)
        ep_rest = jnp.where(iota8 == i_top1, -1.0, ep)
        p_top2 = jnp.max(ep_rest, axis=0, keepdims=True)
        i_top2 = _first_argmax(ep_rest, iota8, p_top2)
        den = p_top1 + p_top2
        w_top1 = g_gate * (p_top1 / den)
        w_top2 = g_gate * (p_top2 / den)
        in_group = (jnp.where(iota8 == i_top1, w_top1, 0.0)
                    + jnp.where(iota8 == i_top2, w_top2, 0.0))
        c_hi = in_group.astype(BF16).astype(F32)
        c_mid = (in_group - c_hi).astype(BF16).astype(F32)
        c_lo = (in_group - c_hi - c_mid).astype(BF16).astype(F32)
        packed_t = jnp.concatenate(
            [c_hi, c_mid, c_lo, jnp.zeros((LANES - 3 * EXPERTS_PER_GROUP, tm), F32)], axis=0)
        comb_ref[rows, :] = packed_t.T.astype(BF16)
        gidx_ref[:, rows] = jnp.broadcast_to(g_idx, (gidx_ref.shape[0], tm))

    stages = (gate_a, merge, project_and_norm, route)
    subs = [slice(s0, s0 + MERGE_SUB_TM) for s0 in range(0, tm, MERGE_SUB_TM)]
    state = [None] * len(subs)
    for step in range(len(subs) + len(stages) - 1):
        for si, rows in enumerate(subs):
            stage = step - si
            if 0 <= stage < len(stages):
                state[si] = stages[stage](rows, state[si]) if stage else stages[0](rows)


def _merge_and_route(u, va, yb, ga, gb, x2, ws, bs, w_up_a_up_b_out, g2, wr_hi, wr_lo, br):
    n_tok = x2.shape[0]
    tm = MERGE_TM
    row_spec = pl.BlockSpec((tm, D_MODEL), lambda i: (i, 0))
    full = lambda shape: pl.BlockSpec(shape, lambda i: (0,) * len(shape),
                                      pipeline_mode=pl.Buffered(1))
    return pl.pallas_call(
        _merge_kernel,
        grid=(n_tok // tm,),
        in_specs=[
            row_spec, row_spec, row_spec, row_spec, row_spec, row_spec,
            full((A_GROUPS, CHUNK, CHUNK)), full((A_GROUPS, CHUNK, 1)),
            full((3, D_MODEL, D_MODEL)),
            full((1, D_MODEL)),
            full((ROUTER_ROWS, D_MODEL)), full((ROUTER_ROWS, D_MODEL)), full((ROUTER_ROWS, 1)),
        ],
        out_specs=[row_spec, row_spec, pl.BlockSpec((tm, LANES), lambda i: (i, 0)),
                   pl.BlockSpec((8, tm), lambda i: (0, i))],
        out_shape=[
            jax.ShapeDtypeStruct((n_tok, D_MODEL), F32),
            jax.ShapeDtypeStruct((n_tok, D_MODEL), BF16),
            jax.ShapeDtypeStruct((n_tok, LANES), BF16),
            jax.ShapeDtypeStruct((8, n_tok), jnp.int32),
        ],
        compiler_params=pltpu.CompilerParams(
            dimension_semantics=("arbitrary",),
            vmem_limit_bytes=BIG_VMEM_LIMIT_BYTES),
        name="merge_and_route",
    )(u, va, yb, ga, gb, x2, ws, bs, w_up_a_up_b_out, g2, wr_hi, wr_lo, br)


def _moe_kernel(g_ref, h_ref, c_ref, x1_ref, w1_ref, w3_ref, w2_ref, o_ref,
                p_scr, xg_scr, cg_scr, y_scr, bounds_ref):
    e = pl.program_id(1)
    n_steps = pl.num_programs(1)
    tt = h_ref.shape[0]
    rows = p_scr.shape[0]
    unit = MOE_ROW_UNIT
    rb = 2 * unit

    @pl.when(e == 0)
    def _():
        g_idx = g_ref[0:1, :]
        slot = lax.broadcasted_iota(jnp.int32, (16, tt), 0)
        onehot = jnp.where(slot == g_idx, 1.0, 0.0)
        upper = (lax.broadcasted_iota(jnp.int32, (LANES, LANES), 0)
                 <= lax.broadcasted_iota(jnp.int32, (LANES, LANES), 1)).astype(BF16)
        carry = jnp.zeros((16, 1), F32)
        counts = []
        for j in range(tt // LANES):
            seg = onehot[:, j * LANES:(j + 1) * LANES].astype(BF16)
            cs = jnp.dot(seg, upper, preferred_element_type=F32) + carry
            counts.append(cs)
            carry = cs[:, LANES - 1:LANES]
        running = jnp.concatenate(counts, axis=1)
        n_units = jnp.floor((carry + (unit - 1)) * (1.0 / unit))
        slot_col = lax.broadcasted_iota(jnp.int32, (16, 1), 0)
        first_unit = jnp.zeros((16, 1), F32)
        for g in range(N_GROUPS - 1):
            first_unit = first_unit + jnp.where(slot_col > g, n_units[g:g + 1, :], 0.0)
        for g in range(N_GROUPS):
            bounds_ref[g] = first_unit[g:g + 1, :][0, 0].astype(jnp.int32)
            bounds_ref[N_GROUPS + g] = n_units[g:g + 1, :][0, 0].astype(jnp.int32)
        pos = jnp.sum(onehot * (first_unit * unit + running - 1.0), axis=0, keepdims=True)
        pos = pos.astype(jnp.int32)

        h_ext = jnp.concatenate([h_ref[...], c_ref[...]], axis=1)
        for ch in range(rows // MXU_DIM):
            sl = slice(ch * MXU_DIM, (ch + 1) * MXU_DIM)
            row_id = lax.broadcasted_iota(jnp.int32, (MXU_DIM, tt), 0) + ch * MXU_DIM
            perm = jnp.where(row_id == pos, 1.0, 0.0).astype(BF16)
            p_scr[sl, :] = perm
            got = jnp.dot(perm, h_ext, preferred_element_type=F32)
            xg_scr[sl, :] = got[:, :D_MODEL].astype(BF16)
            parts = got[:, D_MODEL:]
            cg_scr[sl, :] = (parts + pltpu.roll(parts, LANES - EXPERTS_PER_GROUP, axis=1)
                             + pltpu.roll(parts, LANES - 2 * EXPERTS_PER_GROUP, axis=1))
        y_scr[...] = jnp.zeros_like(y_scr)

    group = e // (EXPERTS_PER_GROUP // MOE_NE)

    def row_block(r0, n_rows):
        r0 = pl.multiple_of(r0, unit)
        xb = xg_scr[pl.ds(r0, n_rows), :]
        cb = cg_scr[pl.ds(r0, n_rows), :]
        hids = []
        for n in range(MOE_NE):
            a = jnp.dot(xb, w1_ref[n], preferred_element_type=F32)
            b3 = jnp.dot(xb, w3_ref[n], preferred_element_type=F32)
            hid = (a * (1.0 / (1.0 + jnp.exp(-a)))) * b3
            hids.append((hid * cb[:, n:n + 1]).astype(BF16))
        hid_all = jnp.concatenate(hids, axis=1)
        y_scr[pl.ds(r0, n_rows), :] = jnp.dot(hid_all, w2_ref[...],
                                              preferred_element_type=F32).astype(BF16)

    first_row = bounds_ref[group] * unit
    n_units = bounds_ref[N_GROUPS + group]
    n_blocks = n_units // 2

    def pair(i, carry):
        row_block(first_row + (2 * i) * rb, rb)
        row_block(first_row + (2 * i + 1) * rb, rb)
        return carry

    lax.fori_loop(0, n_blocks // 2, pair, 0)

    @pl.when(n_blocks % 2 == 1)
    def _():
        row_block(first_row + (n_blocks - 1) * rb, rb)

    @pl.when(n_units % 2 == 1)
    def _():
        row_block(first_row + n_blocks * rb, unit)

    @pl.when(e == n_steps - 1)
    def _():
        tn_dims = (((0,), (0,)), ((), ()))
        y = lax.dot_general(p_scr[...], y_scr[...], tn_dims,
                            preferred_element_type=F32)
        o_ref[...] = x1_ref[...] + y


def _moe(gidx, h2, comb, x1, w1, w3, w2f):
    assert MOE_NE == EXPERTS_PER_GROUP, "each row block is written by exactly one grid step"
    n_tok = h2.shape[0]
    tt = MOE_TM
    rows = tt + N_GROUPS * MOE_ROW_UNIT
    assert rows % MXU_DIM == 0
    return pl.pallas_call(
        _moe_kernel,
        grid=(n_tok // tt, N_EXPERTS // MOE_NE),
        in_specs=[
            pl.BlockSpec((8, tt), lambda i, e: (0, i)),
            pl.BlockSpec((tt, D_MODEL), lambda i, e: (i, 0)),
            pl.BlockSpec((tt, LANES), lambda i, e: (i, 0)),
            pl.BlockSpec((tt, D_MODEL), lambda i, e: (i, 0)),
            pl.BlockSpec((MOE_NE, D_MODEL, D_EXPERT), lambda i, e: (e, 0, 0)),
            pl.BlockSpec((MOE_NE, D_MODEL, D_EXPERT), lambda i, e: (e, 0, 0)),
            pl.BlockSpec((MOE_NE * D_EXPERT, D_MODEL), lambda i, e: (e, 0)),
        ],
        out_specs=pl.BlockSpec((tt, D_MODEL), lambda i, e: (i, 0)),
        out_shape=jax.ShapeDtypeStruct((n_tok, D_MODEL), F32),
        scratch_shapes=[
            pltpu.VMEM((rows, tt), BF16),
            pltpu.VMEM((rows, D_MODEL), BF16),
            pltpu.VMEM((rows, LANES), F32),
            pltpu.VMEM((rows, D_MODEL), BF16),
            pltpu.SMEM((2 * N_GROUPS,), jnp.int32),
        ],
        compiler_params=pltpu.CompilerParams(
            dimension_semantics=("arbitrary", "arbitrary"),
            vmem_limit_bytes=BIG_VMEM_LIMIT_BYTES),
        name="moe_experts",
    )(gidx, h2, comb, x1, w1, w3, w2f)


def _segment_ones(seg):
    idx = jnp.arange(MXU_DIM) // seg
    return (idx[:, None] == idx[None, :]).astype(BF16)


def kernel(x, norm1_g, w_in, v_norm_g, w_s, b_s, q_norm_g, k_norm_g, lambda_q1, lambda_k1, lambda_q2, lambda_k2, sub_norm_g, w_up_a, w_up_b, w_gate, b_gate, w_out, norm2_g, w_rg, b_rg, w_re, b_re, w1, w3, w2):
    bsz, seq, d = x.shape
    assert d == D_MODEL and seq % PROJ_TM == 0 and seq % ATT_TQ == 0 and ATT_TQ == 2 * ATT_TK
    assert seq % ATT_TQ_BOUNDED == 0 and ATT_TQ_BOUNDED % ATT_TK == 0
    assert norm1_g.shape[0] == 1, "single layer"
    n_tok = bsz * seq
    assert n_tok % MOE_TM == 0 and n_tok % MERGE_TM == 0
    x2 = x.reshape(n_tok, d)

    gv = v_norm_g[0].reshape(1, D_MODEL)
    gq = jnp.tile(q_norm_g[0] * (B_HEAD_DIM ** -0.5 * LOG2_E), 2 * B_HEADS)[None, :]
    gk = jnp.tile(k_norm_g[0], 2 * B_HEADS)[None, :]
    score_bound = (SCORE_BOUND_MARGIN * B_HEAD_DIM ** 0.5 * LOG2_E
                   * jnp.max(jnp.abs(q_norm_g[0] * k_norm_g[0]))).reshape(1, 1)
    u, va, q_arr, k_arr, vt_arr, ga, gb = _input_projection(
        x2, norm1_g, w_in[0].astype(BF16), w_gate[0].astype(BF16), b_gate, gv, gq, gk, _segment_ones(B_HEAD_DIM), _segment_ones(LANES),
        score_bound, bsz, seq)

    lam_vecs = jnp.concatenate([lambda_q1, lambda_k1, lambda_q2, lambda_k2], axis=0)
    attn_args = (lam_vecs, sub_norm_g[0][:, None], q_arr, k_arr, vt_arr)
    yb = lax.cond(score_bound[0, 0] <= MAX_BOUNDED_SCORE,
                  lambda args: _diff_attention(*args, bounded=True),
                  lambda args: _diff_attention(*args, bounded=False), attn_args)
    yb = yb.reshape(n_tok, D_MODEL)

    pad_rows = ROUTER_ROWS - 8 - N_EXPERTS
    wr = jnp.concatenate([w_rg[0].T, jnp.zeros((4, d), F32), w_re[0].T,
                          jnp.zeros((pad_rows, d), F32)], axis=0)
    br = jnp.concatenate([b_rg[0], jnp.full((4,), NEG_BIG, F32), b_re[0],
                          jnp.zeros((pad_rows,), F32)])[:, None]
    wr_hi = wr.astype(BF16)
    wr_lo = (wr - wr_hi.astype(F32)).astype(BF16)
    x1, h2, comb, gidx = _merge_and_route(
        u, va, yb, ga, gb, x2, w_s[0], b_s[0][:, :, None],
        jnp.concatenate([w_up_a, w_up_b, w_out], axis=0).astype(BF16),
        norm2_g, wr_hi, wr_lo, br)

    w2f = w2[0].astype(BF16).reshape(N_EXPERTS * D_EXPERT, D_MODEL)
    out = _moe(gidx, h2, comb, x1, w1[0].astype(BF16), w3[0].astype(BF16), w2f)
    return out.reshape(bsz, seq, d)
```

```python
import functools
import math

import jax
import jax.numpy as jnp
import numpy as np
from jax import lax
from jax.experimental import pallas as pl
from jax.experimental.pallas import tpu as pltpu

F32 = jnp.float32
BF16 = jnp.bfloat16

D_MODEL = 1024
EPS = 1e-6
A_GROUPS = 8
CHUNK = 128
B_HEADS = 8
B_HEAD_DIM = 64
B_V_DIM = 128
N_GROUPS = 4
EXPERTS_PER_GROUP = 8
N_EXPERTS = 32
D_EXPERT = 256
LAM_INIT = 0.8 - 0.6 * math.exp(-0.3 * 0)

LANES = 128
MXU_DIM = 256
VMEM_BYTES = 64 * 1024 * 1024
VMEM_LIMIT_BYTES = VMEM_BYTES - 8 * 1024 * 1024
BIG_VMEM_LIMIT_BYTES = VMEM_BYTES - 1024 * 1024

NEG_BIG = -1e30
LOG2_E = math.log2(math.e)
ALIBI_SLOPES = tuple(2.0 ** (-8.0 * (h + 1) / B_HEADS) for h in range(B_HEADS))
SLOPE_PARTS = 4
VT_ROWS = B_V_DIM + 16
SCORE_BOUND_MARGIN = 1.02
MAX_BOUNDED_SCORE = 48.0

PROJ_TM = 512
ATT_TQ_BOUNDED = 2048
ATT_TQ = 1024
ATT_TK = 512
MERGE_TM = 1024
MERGE_SUB_TM = 512
MOE_TM = 1024
MOE_NE = 8
MOE_ROW_UNIT = 64
ROUTER_ROWS = 48


def _bf16_parts(value, n_parts):
    parts, rest = [], np.float64(value)
    for _ in range(n_parts):
        part = np.float64(np.asarray(rest, np.float32).astype(BF16).astype(np.float32))
        parts.append(float(part))
        rest = rest - part
    return parts


def _segment_rms_scale(acc, seg_ones_ref, seg):
    sq = (acc * acc).astype(BF16)
    parts = []
    for p in range(acc.shape[1] // MXU_DIM):
        parts.append(jnp.dot(sq[:, p * MXU_DIM:(p + 1) * MXU_DIM], seg_ones_ref[...],
                             preferred_element_type=F32))
    ss = jnp.concatenate(parts, axis=1)
    return lax.rsqrt(ss * (1.0 / seg) + EPS)


def _inproj_kernel(x_ref, g1_ref, w_ref, wg_ref, bias_ref, gv_ref, gq_ref, gk_ref, ones64_ref, ones128_ref,
                   bound_ref, ws_ref, bs_ref,
                   ya_ref, q_ref, k_ref, vt_ref, ga_ref, gb_ref, *, seq_blocks):
    i = pl.program_id(0)
    tm = x_ref.shape[0]
    x = x_ref[...]
    r = lax.rsqrt(jnp.mean(x * x, axis=-1, keepdims=True) + EPS)
    h = ((x * r) * g1_ref[...]).astype(BF16)

    n_in = w_ref.shape[1] // D_MODEL

    def slab(j):
        ref, col = (w_ref, j) if j < n_in else (wg_ref, j - n_in)
        return jnp.dot(h, ref[:, col * D_MODEL:(col + 1) * D_MODEL], preferred_element_type=F32)

    lane = lax.broadcasted_iota(jnp.int32, (1, LANES), 1)
    lo_half = lane < B_HEAD_DIM

    spare_col = lane & (B_HEAD_DIM - 1)
    pos = (i % seq_blocks) * tm + lax.broadcasted_iota(jnp.int32, (tm, LANES), 0)
    pos_lo = (pos & (MXU_DIM - 1)).astype(F32)
    pos_hi = (pos - (pos & (MXU_DIM - 1))).astype(F32)
    pos_cols = jnp.where(spare_col & SLOPE_PARTS == 0, pos_lo, pos_hi)
    bound = bound_ref[...]
    bound_hi = bound.astype(BF16).astype(F32)
    bound_mid = (bound - bound_hi).astype(BF16).astype(F32)
    bound_lo = (bound - bound_hi - bound_mid).astype(BF16).astype(F32)
    bound_cols = jnp.where(spare_col == 4 * SLOPE_PARTS, bound_hi,
                           jnp.where(spare_col == 4 * SLOPE_PARTS + 1, bound_mid,
                                     jnp.where(spare_col == 4 * SLOPE_PARTS + 2, bound_lo, 0.0)))
    q_shared = jnp.where((spare_col >= 2 * SLOPE_PARTS) & (spare_col < 4 * SLOPE_PARTS),
                         pos_cols, bound_cols)
    k_shared = jnp.where(spare_col < 2 * SLOPE_PARTS, pos_cols, 0.0)

    def slope_cols(hd, sign):
        parts = _bf16_parts(ALIBI_SLOPES[hd] * LOG2_E, SLOPE_PARTS)
        cvec = jnp.zeros((1, LANES), F32)
        for n, part in enumerate(parts):
            cvec = jnp.where(spare_col & (SLOPE_PARTS - 1) == n, sign * part, cvec)
        return cvec

    def store_ya(u_acc, va_acc):
        r = _segment_rms_scale(va_acc, ones128_ref, LANES)
        vn = ((va_acc * r) * gv_ref[...]).astype(BF16)
        causal = (lax.broadcasted_iota(jnp.int32, (CHUNK, CHUNK), 0)
                  >= lax.broadcasted_iota(jnp.int32, (CHUNK, CHUNK), 1))
        n_chunks = tm // CHUNK
        for g in range(A_GROUPS):
            gl = slice(g * LANES, (g + 1) * LANES)
            w_g = jnp.where(causal, ws_ref[g], 0.0).astype(BF16)
            v_g = jnp.concatenate([vn[c * CHUNK:(c + 1) * CHUNK, gl] for c in range(n_chunks)],
                                  axis=1)
            z = jnp.dot(w_g, v_g, preferred_element_type=F32) + bs_ref[g]
            z = jnp.concatenate([z[:, c * LANES:(c + 1) * LANES] for c in range(n_chunks)], axis=0)
            ya_ref[:, gl] = (u_acc[:, gl] * z).astype(BF16)

    def store_q(acc):
        r = _segment_rms_scale(acc, ones64_ref, B_HEAD_DIM)
        qn = (acc * r) * gq_ref[...]
        for hd in range(B_HEADS):
            extra = jnp.where(spare_col < 2 * SLOPE_PARTS, slope_cols(hd, 1.0), q_shared)
            blk = qn[:, hd * LANES:(hd + 1) * LANES]
            q_ref[0, 2 * hd] = jnp.where(lo_half, blk, extra).astype(BF16)
            q_ref[0, 2 * hd + 1] = jnp.where(lo_half, extra, blk).astype(BF16)

    def store_k(acc):
        r = _segment_rms_scale(acc, ones64_ref, B_HEAD_DIM)
        kn = (acc * r) * gk_ref[...]
        for hd in range(B_HEADS):
            consts = jnp.where(spare_col < 4 * SLOPE_PARTS, slope_cols(hd, -1.0),
                               jnp.where(spare_col < 4 * SLOPE_PARTS + 3, -1.0, 0.0))
            extra = jnp.where(spare_col < 2 * SLOPE_PARTS, k_shared, consts)
            blk = kn[:, hd * LANES:(hd + 1) * LANES]
            k_ref[0, 2 * hd] = jnp.where(lo_half, blk, extra).astype(BF16)
            k_ref[0, 2 * hd + 1] = jnp.where(lo_half, extra, blk).astype(BF16)

    def store_vt(acc):
        for hd in range(B_HEADS):
            vt_ref[0, hd, 0:B_V_DIM, :] = acc[:, hd * LANES:(hd + 1) * LANES].T.astype(BF16)
            vt_ref[0, hd, B_V_DIM:VT_ROWS, :] = jnp.ones((VT_ROWS - B_V_DIM, tm), BF16)

    def store_ga(acc):
        ga_ref[...] = (1.0 / (1.0 + jnp.exp(-(acc + bias_ref[:, 0:D_MODEL])))).astype(BF16)

    def store_gb(acc):
        gb_ref[...] = (1.0 / (1.0 + jnp.exp(-(acc + bias_ref[:, D_MODEL:])))).astype(BF16)

    store_ya(slab(0), slab(1))
    for j, store in enumerate((store_q, store_k, store_vt, store_ga, store_gb), start=2):
        store(slab(j))


def _input_projection(x2, g1, w_in, w_gate, b_gate, gv, gq, gk, ones64, ones128, bound, ws, bs,
                      bsz, seq):
    n_tok = x2.shape[0]
    tm = PROJ_TM
    seq_blocks = seq // tm
    row_spec = pl.BlockSpec((tm, D_MODEL), lambda i: (i, 0))
    resident = lambda shape: pl.BlockSpec(shape, lambda i: (0,) * len(shape),
                                          pipeline_mode=pl.Buffered(1))
    head_map = lambda i: (i // seq_blocks, 0, i % seq_blocks, 0)
    tok_bf16 = jax.ShapeDtypeStruct((n_tok, D_MODEL), BF16)
    return pl.pallas_call(
        functools.partial(_inproj_kernel, seq_blocks=seq_blocks),
        grid=(n_tok // tm,),
        in_specs=[
            row_spec,
            resident((1, D_MODEL)),
            resident(w_in.shape), resident(w_gate.shape),
            resident(b_gate.shape),
            resident((1, D_MODEL)), resident((1, D_MODEL)), resident((1, D_MODEL)),
            resident((MXU_DIM, MXU_DIM)), resident((MXU_DIM, MXU_DIM)),
            resident((1, 1)),
            resident((A_GROUPS, CHUNK, CHUNK)), resident((A_GROUPS, CHUNK, 1)),
        ],
        out_specs=[
            row_spec,
            pl.BlockSpec((1, 2 * B_HEADS, tm, LANES), head_map),
            pl.BlockSpec((1, 2 * B_HEADS, tm, LANES), head_map),
            pl.BlockSpec((1, B_HEADS, VT_ROWS, tm), lambda i: (i // seq_blocks, 0, 0, i % seq_blocks)),
            row_spec, row_spec,
        ],
        out_shape=[
            tok_bf16,
            jax.ShapeDtypeStruct((bsz, 2 * B_HEADS, seq, LANES), BF16),
            jax.ShapeDtypeStruct((bsz, 2 * B_HEADS, seq, LANES), BF16),
            jax.ShapeDtypeStruct((bsz, B_HEADS, VT_ROWS, seq), BF16),
            tok_bf16, tok_bf16,
        ],
        compiler_params=pltpu.CompilerParams(
            dimension_semantics=("arbitrary",),
            vmem_limit_bytes=VMEM_LIMIT_BYTES),
        name="input_projection",
    )(x2, g1, w_in, w_gate, b_gate, gv, gq, gk, ones64, ones128, bound, ws, bs)


def _attn_kernel(lam_ref, sg_ref, q_ref, k_ref, vt_ref, o_ref, acc_scr, sa_scr, sb_scr):
    qi = pl.program_id(2)
    tq = q_ref.shape[2]
    tk = ATT_TK
    nt_dims = (((1,), (1,)), ((), ()))

    assert tq == 2 * tk
    tri = (lax.broadcasted_iota(jnp.int32, (tk, tk), 0)
           <= lax.broadcasted_iota(jnp.int32, (tk, tk), 1))

    def scores(kb, s_scr, diag=None):
        ks = pl.multiple_of(kb * tk, tk)
        maxima = []
        for c in range(2):
            k_blk = k_ref[0, c, pl.ds(ks, tk), :]
            if diag is None:
                s = lax.dot_general(k_blk, q_ref[0, c], nt_dims, preferred_element_type=F32)
                s_scr[c] = s
                maxima.append(jnp.max(s, axis=0, keepdims=True))
                continue
            parts = []
            for half in range(diag, 2):
                cols = slice(half * tk, (half + 1) * tk)
                s = lax.dot_general(k_blk, q_ref[0, c, cols, :], nt_dims,
                                    preferred_element_type=F32)
                if half == diag:
                    s = jnp.where(tri, s, NEG_BIG)
                s_scr[c, :, cols] = s
                parts.append(jnp.max(s, axis=0, keepdims=True))
            maxima.append(parts[0] if len(parts) == 1 else jnp.concatenate(parts, axis=1))
        return tuple(maxima)

    def consume(kb, s_scr, m_old, blk_max, col0=0, first=False):
        ks = pl.multiple_of(kb * tk, tk)
        vt = vt_ref[0, 0, :, pl.ds(ks, tk)]
        m_new = []
        for c in range(2):
            m_c = blk_max[c] if first else jnp.maximum(m_old[c][:, col0:], blk_max[c])
            p = jnp.exp2(s_scr[c, :, col0:] - m_c).astype(BF16)
            pv = jnp.dot(vt, p, preferred_element_type=F32)
            cols = slice(c * tq + col0, (c + 1) * tq)
            if first:
                acc_scr[:, cols] = pv
            else:
                acc_scr[:, cols] = acc_scr[:, cols] * jnp.exp2(m_old[c][:, col0:] - m_c) + pv
            m_new.append(m_c)
        return tuple(m_new)

    max_a = scores(2 * qi, sa_scr, diag=0)
    max_b = scores(2 * qi + 1, sb_scr, diag=1)
    m = consume(2 * qi, sa_scr, None, max_a, first=True)
    max_a = scores(0, sa_scr)
    m_right = consume(2 * qi + 1, sb_scr, m, max_b, col0=tk)
    m = tuple(jnp.concatenate([m[c][:, :tk], m_right[c]], axis=1) for c in range(2))

    def trip(t, carry):
        m, max_a = carry
        max_b = scores(2 * t + 1, sb_scr)
        m = consume(2 * t, sa_scr, m, max_a)
        max_a = scores(2 * t + 2, sa_scr)
        m = consume(2 * t + 1, sb_scr, m, max_b)
        return m, max_a

    m, max_a = lax.fori_loop(0, qi - 1, trip, (m, max_a))

    @pl.when(qi > 0)
    def _():
        max_b = scores(2 * qi - 1, sb_scr)
        m_mid = consume(2 * qi - 2, sa_scr, m, max_a)
        consume(2 * qi - 1, sb_scr, m_mid, max_b)

    _attn_finalize(lam_ref, sg_ref, acc_scr, o_ref)


def _attn_finalize(lam_ref, sg_ref, acc_scr, o_ref):
    tq = o_ref.shape[1]
    lam_v = lam_ref[...]
    lam = (jnp.exp(jnp.sum(lam_v[0:1] * lam_v[1:2], axis=1, keepdims=True))
           - jnp.exp(jnp.sum(lam_v[2:3] * lam_v[3:4], axis=1, keepdims=True)) + LAM_INIT)
    o0 = acc_scr[0:B_V_DIM, 0:tq] / acc_scr[B_V_DIM:B_V_DIM + 1, 0:tq]
    o1 = acc_scr[0:B_V_DIM, tq:2 * tq] / acc_scr[B_V_DIM:B_V_DIM + 1, tq:2 * tq]
    o = o0 - lam * o1
    r = lax.rsqrt(jnp.mean(o * o, axis=0, keepdims=True) + EPS)
    o = ((o * r) * sg_ref[...]) * (1.0 - LAM_INIT)
    o_ref[0] = o.T.astype(BF16)


def _attn_bounded_kernel(lam_ref, sg_ref, q_ref, k_ref, vt_ref, o_ref, acc_scr):
    qi = pl.program_id(2)
    tq = q_ref.shape[2]
    tk = ATT_TK
    nt_dims = (((1,), (1,)), ((), ()))
    tri = (lax.broadcasted_iota(jnp.int32, (tk, tk), 0)
           <= lax.broadcasted_iota(jnp.int32, (tk, tk), 1))

    def fold(key_start, n_keys, q_cols, masked, first):
        vt = vt_ref[0, 0, 0:B_V_DIM, pl.ds(key_start, n_keys)]
        for c in range(2):
            e = lax.dot_general(k_ref[0, c, pl.ds(key_start, n_keys), :], q_ref[0, c, q_cols, :],
                                nt_dims, preferred_element_type=F32)
            if masked:
                e_diag = jnp.where(tri, e[n_keys - tk:, :], NEG_BIG)
                e = e_diag if n_keys == tk else jnp.concatenate([e[:n_keys - tk, :], e_diag], axis=0)
            p = jnp.exp2(e)
            p_sum = jnp.sum(p, axis=0, keepdims=True)
            pv = jnp.dot(vt, p.astype(BF16), preferred_element_type=F32)
            cols = slice(c * tq + q_cols.start, c * tq + q_cols.stop)
            if first:
                acc_scr[0:B_V_DIM, cols] = pv
                acc_scr[B_V_DIM:B_V_DIM + 1, cols] = p_sum
            else:
                acc_scr[0:B_V_DIM, cols] += pv
                acc_scr[B_V_DIM:B_V_DIM + 1, cols] += p_sum

    strips = [slice(j * tk, (j + 1) * tk) for j in range(tq // tk)]
    diag0 = pl.multiple_of(qi * tq, tq)
    for j, strip in enumerate(strips):
        fold(diag0, (j + 1) * tk, strip, True, True)

    def trip(t, carry):
        for strip in strips:
            fold(pl.multiple_of(t * tq, tq), tq, strip, False, False)
        return carry

    lax.fori_loop(0, qi, trip, 0)
    _attn_finalize(lam_ref, sg_ref, acc_scr, o_ref)


def _diff_attention(lam_vecs, sub_g, q_arr, k_arr, vt_arr, bounded):
    bsz, _, seq, _ = q_arr.shape
    tq = ATT_TQ_BOUNDED if bounded else ATT_TQ
    scratch = [pltpu.VMEM((VT_ROWS, 2 * tq), F32)]
    if not bounded:
        scratch += [pltpu.VMEM((2, ATT_TK, tq), F32), pltpu.VMEM((2, ATT_TK, tq), F32)]
    return pl.pallas_call(
        _attn_bounded_kernel if bounded else _attn_kernel,
        grid=(bsz, B_HEADS, seq // tq),
        in_specs=[
            pl.BlockSpec((4, B_HEAD_DIM), lambda b, h, qi: (0, 0)),
            pl.BlockSpec((B_V_DIM, 1), lambda b, h, qi: (0, 0)),
            pl.BlockSpec((1, 2, tq, LANES), lambda b, h, qi: (b, h, qi, 0)),
            pl.BlockSpec((1, 2, seq, LANES), lambda b, h, qi: (b, h, 0, 0)),
            pl.BlockSpec((1, 1, VT_ROWS, seq), lambda b, h, qi: (b, h, 0, 0)),
        ],
        out_specs=pl.BlockSpec((1, tq, B_V_DIM), lambda b, h, qi: (b, qi, h)),
        out_shape=jax.ShapeDtypeStruct((bsz, seq, B_HEADS * B_V_DIM), BF16),
        scratch_shapes=scratch,
        compiler_params=pltpu.CompilerParams(
            dimension_semantics=("arbitrary", "arbitrary", "arbitrary"),
            vmem_limit_bytes=VMEM_LIMIT_BYTES),
        name="diff_attention_bounded" if bounded else "diff_attention",
    )(lam_vecs, sub_g, q_arr, k_arr, vt_arr)


def _first_argmax(vals, iota, top):
    return jnp.min(jnp.where(vals == top, iota, vals.shape[0]), axis=0, keepdims=True)


def _merge_kernel(ya_ref, yb_ref, ga_ref, gb_ref, x_ref,
                  w3_ref, g2_ref, wr_hi_ref, wr_lo_ref, br_ref,
                  x1_ref, h2_ref, comb_ref, gidx_ref):
    tm = x_ref.shape[0]
    nt_dims = (((1,), (1,)), ((), ()))

    def merge(rows):
        up_a = jnp.dot(ya_ref[rows, :], w3_ref[0], preferred_element_type=F32)
        up_b = jnp.dot(yb_ref[rows, :], w3_ref[1], preferred_element_type=F32)
        merged = ga_ref[rows, :].astype(F32) * up_a + gb_ref[rows, :].astype(F32) * up_b
        return merged.astype(BF16)

    def project_and_norm(rows, merged):
        x1 = x_ref[rows, :] + jnp.dot(merged, w3_ref[2], preferred_element_type=F32)
        x1_ref[rows, :] = x1
        r = lax.rsqrt(jnp.mean(x1 * x1, axis=-1, keepdims=True) + EPS)
        h2 = (x1 * r) * g2_ref[...]
        h2_hi = h2.astype(BF16)
        h2_ref[rows, :] = h2_hi
        return h2_hi, (h2 - h2_hi.astype(F32)).astype(BF16)

    def route(rows, h2_parts):
        h2_hi, h2_lo = h2_parts
        tm = rows.stop - rows.start
        lt = (lax.dot_general(wr_hi_ref[...], h2_hi, nt_dims, preferred_element_type=F32)
              + lax.dot_general(wr_hi_ref[...], h2_lo, nt_dims, preferred_element_type=F32)
              + lax.dot_general(wr_lo_ref[...], h2_hi, nt_dims, preferred_element_type=F32)
              + br_ref[...])

        iota8 = lax.broadcasted_iota(jnp.int32, (8, tm), 0)
        gl = lt[0:8]
        ge = jnp.exp(gl - jnp.max(gl, axis=0, keepdims=True))
        gp = ge / jnp.sum(ge, axis=0, keepdims=True)
        g_gate = jnp.max(gp, axis=0, keepdims=True)
        g_idx = _first_argmax(gp, iota8, g_gate)

        el = jnp.zeros((EXPERTS_PER_GROUP, tm), F32)
        for gi in range(N_GROUPS):
            el = jnp.where(g_idx == gi, lt[8 + 8 * gi:16 + 8 * gi], el)
        ee = jnp.exp(el - jnp.max(el, axis=0, keepdims=True))
        ep = ee / jnp.sum(ee, axis=0, keepdims=True)
        p_top1 = jnp.max(ep, axis=0, keepdims=True)
        i_top1 = _first_argmax(ep, iota8, p_top1)
        ep_rest = jnp.where(iota8 == i_top1, -1.0, ep)
        p_top2 = jnp.max(ep_rest, axis=0, keepdims=True)
        i_top2 = _first_argmax(ep_rest, iota8, p_top2)
        den = p_top1 + p_top2
        w_top1 = g_gate * (p_top1 / den)
        w_top2 = g_gate * (p_top2 / den)
        in_group = (jnp.where(iota8 == i_top1, w_top1, 0.0)
                    + jnp.where(iota8 == i_top2, w_top2, 0.0))
        c_hi = in_group.astype(BF16).astype(F32)
        c_mid = (in_group - c_hi).astype(BF16).astype(F32)
        c_lo = (in_group - c_hi - c_mid).astype(BF16).astype(F32)
        packed_t = jnp.concatenate(
            [c_hi, c_mid, c_lo, jnp.zeros((LANES - 3 * EXPERTS_PER_GROUP, tm), F32)], axis=0)
        comb_ref[rows, :] = packed_t.T.astype(BF16)
        gidx_ref[:, rows] = jnp.broadcast_to(g_idx, (gidx_ref.shape[0], tm))

    stages = (merge, project_and_norm, route)
    subs = [slice(s0, s0 + MERGE_SUB_TM) for s0 in range(0, tm, MERGE_SUB_TM)]
    state = [None] * len(subs)
    for step in range(len(subs) + len(stages) - 1):
        for si, rows in enumerate(subs):
            stage = step - si
            if 0 <= stage < len(stages):
                state[si] = stages[stage](rows, state[si]) if stage else stages[0](rows)


def _merge_and_route(ya, yb, ga, gb, x2, w_up_a_up_b_out, g2, wr_hi, wr_lo, br):
    n_tok = x2.shape[0]
    tm = MERGE_TM
    row_spec = pl.BlockSpec((tm, D_MODEL), lambda i: (i, 0))
    full = lambda shape: pl.BlockSpec(shape, lambda i: (0,) * len(shape),
                                      pipeline_mode=pl.Buffered(1))
    return pl.pallas_call(
        _merge_kernel,
        grid=(n_tok // tm,),
        in_specs=[
            row_spec, row_spec, row_spec, row_spec, row_spec,
            full((3, D_MODEL, D_MODEL)),
            full((1, D_MODEL)),
            full((ROUTER_ROWS, D_MODEL)), full((ROUTER_ROWS, D_MODEL)), full((ROUTER_ROWS, 1)),
        ],
        out_specs=[row_spec, row_spec, pl.BlockSpec((tm, LANES), lambda i: (i, 0)),
                   pl.BlockSpec((8, tm), lambda i: (0, i))],
        out_shape=[
            jax.ShapeDtypeStruct((n_tok, D_MODEL), F32),
            jax.ShapeDtypeStruct((n_tok, D_MODEL), BF16),
            jax.ShapeDtypeStruct((n_tok, LANES), BF16),
            jax.ShapeDtypeStruct((8, n_tok), jnp.int32),
        ],
        compiler_params=pltpu.CompilerParams(
            dimension_semantics=("arbitrary",),
            vmem_limit_bytes=BIG_VMEM_LIMIT_BYTES),
        name="merge_and_route",
    )(ya, yb, ga, gb, x2, w_up_a_up_b_out, g2, wr_hi, wr_lo, br)


def _moe_kernel(g_ref, h_ref, c_ref, x1_ref, w1_ref, w3_ref, w2_ref, o_ref,
                p_scr, xg_scr, cg_scr, y_scr, bounds_ref):
    e = pl.program_id(1)
    n_steps = pl.num_programs(1)
    tt = h_ref.shape[0]
    rows = p_scr.shape[0]
    unit = MOE_ROW_UNIT
    rb = 2 * unit

    @pl.when(e == 0)
    def _():
        g_idx = g_ref[0:1, :]
        slot = lax.broadcasted_iota(jnp.int32, (16, tt), 0)
        onehot = jnp.where(slot == g_idx, 1.0, 0.0)
        upper = (lax.broadcasted_iota(jnp.int32, (LANES, LANES), 0)
                 <= lax.broadcasted_iota(jnp.int32, (LANES, LANES), 1)).astype(BF16)
        carry = jnp.zeros((16, 1), F32)
        counts = []
        for j in range(tt // LANES):
            seg = onehot[:, j * LANES:(j + 1) * LANES].astype(BF16)
            cs = jnp.dot(seg, upper, preferred_element_type=F32) + carry
            counts.append(cs)
            carry = cs[:, LANES - 1:LANES]
        running = jnp.concatenate(counts, axis=1)
        n_units = jnp.floor((carry + (unit - 1)) * (1.0 / unit))
        slot_col = lax.broadcasted_iota(jnp.int32, (16, 1), 0)
        first_unit = jnp.zeros((16, 1), F32)
        for g in range(N_GROUPS - 1):
            first_unit = first_unit + jnp.where(slot_col > g, n_units[g:g + 1, :], 0.0)
        for g in range(N_GROUPS):
            bounds_ref[g] = first_unit[g:g + 1, :][0, 0].astype(jnp.int32)
            bounds_ref[N_GROUPS + g] = n_units[g:g + 1, :][0, 0].astype(jnp.int32)
        pos = jnp.sum(onehot * (first_unit * unit + running - 1.0), axis=0, keepdims=True)
        pos = pos.astype(jnp.int32)

        h_ext = jnp.concatenate([h_ref[...], c_ref[...]], axis=1)
        for ch in range(rows // MXU_DIM):
            sl = slice(ch * MXU_DIM, (ch + 1) * MXU_DIM)
            row_id = lax.broadcasted_iota(jnp.int32, (MXU_DIM, tt), 0) + ch * MXU_DIM
            perm = jnp.where(row_id == pos, 1.0, 0.0).astype(BF16)
            p_scr[sl, :] = perm
            got = jnp.dot(perm, h_ext, preferred_element_type=F32)
            xg_scr[sl, :] = got[:, :D_MODEL].astype(BF16)
            parts = got[:, D_MODEL:]
            cg_scr[sl, :] = (parts + pltpu.roll(parts, LANES - EXPERTS_PER_GROUP, axis=1)
                             + pltpu.roll(parts, LANES - 2 * EXPERTS_PER_GROUP, axis=1))
        y_scr[...] = jnp.zeros_like(y_scr)

    group = e // (EXPERTS_PER_GROUP // MOE_NE)

    def row_block(r0, n_rows):
        r0 = pl.multiple_of(r0, unit)
        xb = xg_scr[pl.ds(r0, n_rows), :]
        cb = cg_scr[pl.ds(r0, n_rows), :]
        hids = []
        for n in range(MOE_NE):
            a = jnp.dot(xb, w1_ref[n], preferred_element_type=F32)
            b3 = jnp.dot(xb, w3_ref[n], preferred_element_type=F32)
            hid = (a * (1.0 / (1.0 + jnp.exp(-a)))) * b3
            hids.append((hid * cb[:, n:n + 1]).astype(BF16))
        hid_all = jnp.concatenate(hids, axis=1)
        y_scr[pl.ds(r0, n_rows), :] = jnp.dot(hid_all, w2_ref[...],
                                              preferred_element_type=F32).astype(BF16)

    first_row = bounds_ref[group] * unit
    n_units = bounds_ref[N_GROUPS + group]
    n_blocks = n_units // 2

    def pair(i, carry):
        row_block(first_row + (2 * i) * rb, rb)
        row_block(first_row + (2 * i + 1) * rb, rb)
        return carry

    lax.fori_loop(0, n_blocks // 2, pair, 0)

    @pl.when(n_blocks % 2 == 1)
    def _():
        row_block(first_row + (n_blocks - 1) * rb, rb)

    @pl.when(n_units % 2 == 1)
    def _():
        row_block(first_row + n_blocks * rb, unit)

    @pl.when(e == n_steps - 1)
    def _():
        tn_dims = (((0,), (0,)), ((), ()))
        y = lax.dot_general(p_scr[...], y_scr[...], tn_dims,
                            preferred_element_type=F32)
        o_ref[...] = x1_ref[...] + y


def _moe(gidx, h2, comb, x1, w1, w3, w2f):
    assert MOE_NE == EXPERTS_PER_GROUP, "each row block is written by exactly one grid step"
    n_tok = h2.shape[0]
    tt = MOE_TM
    rows = tt + N_GROUPS * MOE_ROW_UNIT
    assert rows % MXU_DIM == 0
    return pl.pallas_call(
        _moe_kernel,
        grid=(n_tok // tt, N_EXPERTS // MOE_NE),
        in_specs=[
            pl.BlockSpec((8, tt), lambda i, e: (0, i)),
            pl.BlockSpec((tt, D_MODEL), lambda i, e: (i, 0)),
            pl.BlockSpec((tt, LANES), lambda i, e: (i, 0)),
            pl.BlockSpec((tt, D_MODEL), lambda i, e: (i, 0)),
            pl.BlockSpec((MOE_NE, D_MODEL, D_EXPERT), lambda i, e: (e, 0, 0)),
            pl.BlockSpec((MOE_NE, D_MODEL, D_EXPERT), lambda i, e: (e, 0, 0)),
            pl.BlockSpec((MOE_NE * D_EXPERT, D_MODEL), lambda i, e: (e, 0)),
        ],
        out_specs=pl.BlockSpec((tt, D_MODEL), lambda i, e: (i, 0)),
        out_shape=jax.ShapeDtypeStruct((n_tok, D_MODEL), F32),
        scratch_shapes=[
            pltpu.VMEM((rows, tt), BF16),
            pltpu.VMEM((rows, D_MODEL), BF16),
            pltpu.VMEM((rows, LANES), F32),
            pltpu.VMEM((rows, D_MODEL), BF16),
            pltpu.SMEM((2 * N_GROUPS,), jnp.int32),
        ],
        compiler_params=pltpu.CompilerParams(
            dimension_semantics=("arbitrary", "arbitrary"),
            vmem_limit_bytes=BIG_VMEM_LIMIT_BYTES),
        name="moe_experts",
    )(gidx, h2, comb, x1, w1, w3, w2f)


def _segment_ones(seg):
    idx = jnp.arange(MXU_DIM) // seg
    return (idx[:, None] == idx[None, :]).astype(BF16)


def kernel(x, norm1_g, w_in, v_norm_g, w_s, b_s, q_norm_g, k_norm_g, lambda_q1, lambda_k1, lambda_q2, lambda_k2, sub_norm_g, w_up_a, w_up_b, w_gate, b_gate, w_out, norm2_g, w_rg, b_rg, w_re, b_re, w1, w3, w2):
    bsz, seq, d = x.shape
    assert d == D_MODEL and seq % PROJ_TM == 0 and seq % ATT_TQ == 0 and ATT_TQ == 2 * ATT_TK
    assert seq % ATT_TQ_BOUNDED == 0 and ATT_TQ_BOUNDED % ATT_TK == 0
    assert norm1_g.shape[0] == 1, "single layer"
    n_tok = bsz * seq
    assert n_tok % MOE_TM == 0 and n_tok % MERGE_TM == 0
    x2 = x.reshape(n_tok, d)

    gv = v_norm_g[0].reshape(1, D_MODEL)
    gq = jnp.tile(q_norm_g[0] * (B_HEAD_DIM ** -0.5 * LOG2_E), 2 * B_HEADS)[None, :]
    gk = jnp.tile(k_norm_g[0], 2 * B_HEADS)[None, :]
    score_bound = (SCORE_BOUND_MARGIN * B_HEAD_DIM ** 0.5 * LOG2_E
                   * jnp.max(jnp.abs(q_norm_g[0] * k_norm_g[0]))).reshape(1, 1)
    ya, q_arr, k_arr, vt_arr, ga, gb = _input_projection(
        x2, norm1_g, w_in[0].astype(BF16), w_gate[0].astype(BF16), b_gate, gv, gq, gk, _segment_ones(B_HEAD_DIM), _segment_ones(LANES),
        score_bound, w_s[0], b_s[0][:, :, None], bsz, seq)

    lam_vecs = jnp.concatenate([lambda_q1, lambda_k1, lambda_q2, lambda_k2], axis=0)
    attn_args = (lam_vecs, sub_norm_g[0][:, None], q_arr, k_arr, vt_arr)
    yb = lax.cond(score_bound[0, 0] <= MAX_BOUNDED_SCORE,
                  lambda args: _diff_attention(*args, bounded=True),
                  lambda args: _diff_attention(*args, bounded=False), attn_args)
    yb = yb.reshape(n_tok, D_MODEL)

    pad_rows = ROUTER_ROWS - 8 - N_EXPERTS
    wr = jnp.concatenate([w_rg[0].T, jnp.zeros((4, d), F32), w_re[0].T,
                          jnp.zeros((pad_rows, d), F32)], axis=0)
    br = jnp.concatenate([b_rg[0], jnp.full((4,), NEG_BIG, F32), b_re[0],
                          jnp.zeros((pad_rows,), F32)])[:, None]
    wr_hi = wr.astype(BF16)
    wr_lo = (wr - wr_hi.astype(F32)).astype(BF16)
    x1, h2, comb, gidx = _merge_and_route(
        ya, yb, ga, gb, x2,
        jnp.concatenate([w_up_a, w_up_b, w_out], axis=0).astype(BF16),
        norm2_g, wr_hi, wr_lo, br)

    w2f = w2[0].astype(BF16).reshape(N_EXPERTS * D_EXPERT, D_MODEL)
    out = _moe(gidx, h2, comb, x1, w1[0].astype(BF16), w3[0].astype(BF16), w2f)
    return out.reshape(bsz, seq, d)
```
